```python
import math
import jax
import jax.numpy as jnp
from jax import lax
import numpy as np

D_MODEL = 4096
BATCH = 4
SEQ = 2048
DEPTH = 2

N_META = 16
BLOCK = 128
MLA_HEADS = 16
MLA_Q_RANK = 1024
MLA_KV_RANK = 512
MLA_NOPE = 128
MLA_ROPE = 64
MLA_V = 128
ROPE_THETA = 10000.0
SWA_HEADS = 32
SWA_KV_HEADS = 8
SWA_HEAD_DIM = 64
WINDOW = 128
REL_BUCKETS = 32
REL_MAX_DIST = 128
DENSE_FF = 11008
N_EXPERTS = 8
TOP_K = 2
EXPERT_FF = 5632
N_DENSE = (DEPTH + 1) // 2
N_MOE = DEPTH // 2
N_BRANCH = 2
EPS = 1e-6
NEG_INF = -1e30
IN_SPLIT_SIZES = (MLA_Q_RANK, MLA_KV_RANK, MLA_ROPE, SWA_HEADS * SWA_HEAD_DIM,
                  SWA_KV_HEADS * SWA_HEAD_DIM, SWA_KV_HEADS * SWA_HEAD_DIM, D_MODEL, D_MODEL)
IN_COLS = (MLA_Q_RANK + MLA_KV_RANK + MLA_ROPE + SWA_HEADS * SWA_HEAD_DIM
           + 2 * SWA_KV_HEADS * SWA_HEAD_DIM + N_BRANCH * D_MODEL)

kernel_name = 'hybrid_mla_swa_moe_block'


def rms_norm(x, g):
    xf = x.astype(jnp.float32)
    y = xf * lax.rsqrt(jnp.mean(xf * xf, axis=-1, keepdims=True) + EPS)
    return (y * g.astype(jnp.float32)).astype(x.dtype)


def rope(x, pos):
    half = x.shape[-1] // 2
    inv_freq = ROPE_THETA ** (-jnp.arange(half, dtype=jnp.float32) / half)
    ang = pos.astype(jnp.float32)[:, None] * inv_freq[None, :]
    cos = jnp.cos(ang)[:, None, :]
    sin = jnp.sin(ang)[:, None, :]
    xf = x.astype(jnp.float32)
    x1, x2 = xf[..., :half], xf[..., half:]
    return jnp.concatenate([x1 * cos - x2 * sin, x2 * cos + x1 * sin], axis=-1).astype(x.dtype)


def t5_bucket(rel):
    n = jnp.maximum(rel, 0)
    max_exact = REL_BUCKETS // 2
    nf = jnp.maximum(n, 1).astype(jnp.float32)
    large = max_exact + (jnp.log(nf / max_exact) / math.log(REL_MAX_DIST / max_exact)
                         * (REL_BUCKETS - max_exact)).astype(jnp.int32)
    large = jnp.minimum(large, REL_BUCKETS - 1)
    return jnp.where(n < max_exact, n, large)


def causal_block_attention(q, k, v, scale):
    L = q.shape[1]
    starts = [0] + list(range(N_META, L, BLOCK))
    ends = starts[1:] + [L]
    outs = []
    for s, e in zip(starts, ends):
        logits = jnp.einsum('bqhd,bkhd->bhqk', q[:, s:e], k[:, :e]).astype(jnp.float32) * scale
        mask = jnp.arange(s, e)[:, None] >= jnp.arange(e)[None, :]
        logits = jnp.where(mask, logits, NEG_INF)
        probs = jax.nn.softmax(logits, axis=-1).astype(v.dtype)
        outs.append(jnp.einsum('bhqk,bkhd->bqhd', probs, v[:, :e]))
    return jnp.concatenate(outs, axis=1)


def mla_branch(c_q, c_kv, k_pe, cq_g, ckv_g, w_uq, w_ukv, qn_g, kn_g):
    B, L = c_q.shape[0], c_q.shape[1]
    q = (rms_norm(c_q, cq_g) @ w_uq).reshape(B, L, MLA_HEADS, MLA_NOPE + MLA_ROPE)
    kv = (rms_norm(c_kv, ckv_g) @ w_ukv).reshape(B, L, MLA_HEADS, MLA_NOPE + MLA_V)
    k_nope, v = kv[..., :MLA_NOPE], kv[..., MLA_NOPE:]
    k = jnp.concatenate([k_nope, jnp.broadcast_to(k_pe[:, :, None, :], (B, L, MLA_HEADS, MLA_ROPE))], axis=-1)
    q = rms_norm(q, qn_g)
    k = rms_norm(k, kn_g)
    pos = jnp.arange(L)
    q = jnp.concatenate([q[..., :MLA_NOPE], rope(q[..., MLA_NOPE:], pos)], axis=-1)
    k = jnp.concatenate([k[..., :MLA_NOPE], rope(k[..., MLA_NOPE:], pos)], axis=-1)
    out = causal_block_attention(q, k, v, (MLA_NOPE + MLA_ROPE) ** -0.5)
    return out.reshape(B, L, MLA_HEADS * MLA_V)


def swa_branch(q, k, v, sinks, rel_bias):
    B, L = q.shape[0], q.shape[1]
    pad = (-N_META) % BLOCK
    tail = (-(L + pad)) % BLOCK
    nb = (L + pad + tail) // BLOCK
    group = SWA_HEADS // SWA_KV_HEADS

    def to_blocks(t):
        t = jnp.pad(t, ((0, 0), (pad, tail), (0, 0), (0, 0)))
        return t.reshape((B, nb, BLOCK) + t.shape[2:])

    def band(t):
        prev = jnp.concatenate([jnp.zeros_like(t[:, :1]), t[:, :-1]], axis=1)
        return jnp.concatenate([prev, t], axis=2)

    qb = to_blocks(q).reshape(B, nb, BLOCK, SWA_KV_HEADS, group, SWA_HEAD_DIM)
    kk, vv = band(to_blocks(k)), band(to_blocks(v))
    qi = jnp.arange(BLOCK)[:, None]
    sj = jnp.arange(2 * BLOCK)[None, :]
    rel = qi + BLOCK - sj
    kpos = jnp.arange(nb)[:, None, None] * BLOCK - BLOCK + sj[None]
    valid = (rel >= 0)[None] & (rel < WINDOW)[None] & (kpos >= pad) & (kpos < pad + L)
    bias = rel_bias[t5_bucket(rel)].astype(jnp.float32)
    bias = jnp.transpose(bias, (2, 0, 1)).reshape(SWA_KV_HEADS, group, BLOCK, 2 * BLOCK)
    logits = jnp.einsum('bnqhgd,bnshd->bnhgqs', qb, kk).astype(jnp.float32) * (SWA_HEAD_DIM ** -0.5) + bias
    logits = jnp.where(valid[None, :, None, None], logits, NEG_INF)
    sink = jnp.broadcast_to(sinks.astype(jnp.float32).reshape(1, 1, SWA_KV_HEADS, group, 1, 1),
                            logits.shape[:-1] + (1,))
    probs = jax.nn.softmax(jnp.concatenate([logits, sink], axis=-1), axis=-1)[..., :-1]
    out = jnp.einsum('bnhgqs,bnshd->bnqhgd', probs.astype(v.dtype), vv)
    return out.reshape(B, nb * BLOCK, SWA_HEADS * SWA_HEAD_DIM)[:, pad:pad + L]


def swiglu(h, w1, w3, w2):
    return (jax.nn.silu(h @ w1) * (h @ w3)) @ w2


def moe_swiglu(h, router, w1, w3, w2):
    logits = (h @ router).astype(jnp.float32)
    top_vals, top_idx = lax.top_k(logits, TOP_K)
    top_w = jax.nn.softmax(top_vals, axis=-1)
    gate = jnp.sum(jax.nn.one_hot(top_idx, N_EXPERTS, dtype=jnp.float32) * top_w[..., None], axis=-2).astype(h.dtype)
    out = jnp.zeros_like(h)
    for e in range(N_EXPERTS):
        out = out + gate[..., e:e + 1] * swiglu(h, w1[e], w3[e], w2[e])
    return out


def setup_inputs(seed: int = 0) -> dict:
    key = jax.random.key(seed)
    ks = iter(jax.random.split(key, 32))

    def nrm(shape, scale):
        return jax.random.normal(next(ks), shape, jnp.float32) * scale

    def gain(shape):
        return 1.0 + 0.02 * jax.random.normal(next(ks), shape, jnp.float32)

    D = D_MODEL
    return {
        'x': nrm((BATCH, SEQ, D), 1.0),
        'meta_tokens': nrm((N_META, D), 1.0),
        'rel_bias': nrm((REL_BUCKETS, SWA_HEADS), 0.5),
        'attn_norm': gain((DEPTH, D)),
        'w_in': nrm((DEPTH, D, IN_COLS), D ** -0.5),
        'mla_cq_norm': gain((DEPTH, MLA_Q_RANK)),
        'mla_ckv_norm': gain((DEPTH, MLA_KV_RANK)),
        'mla_w_uq': nrm((DEPTH, MLA_Q_RANK, MLA_HEADS * (MLA_NOPE + MLA_ROPE)), MLA_Q_RANK ** -0.5),
        'mla_w_ukv': nrm((DEPTH, MLA_KV_RANK, MLA_HEADS * (MLA_NOPE + MLA_V)), MLA_KV_RANK ** -0.5),
        'mla_q_norm': gain((DEPTH, MLA_NOPE + MLA_ROPE)),
        'mla_k_norm': gain((DEPTH, MLA_NOPE + MLA_ROPE)),
        'swa_q_norm': gain((DEPTH, SWA_HEAD_DIM)),
        'swa_k_norm': gain((DEPTH, SWA_HEAD_DIM)),
        'swa_sinks': nrm((DEPTH, SWA_HEADS), 1.0),
        'w_branch_mla': nrm((DEPTH, MLA_HEADS * MLA_V, D), (MLA_HEADS * MLA_V) ** -0.5),
        'w_branch_swa': nrm((DEPTH, SWA_HEADS * SWA_HEAD_DIM, D), (SWA_HEADS * SWA_HEAD_DIM) ** -0.5),
        'w_out': nrm((DEPTH, D, D), D ** -0.5),
        'ffn_norm': gain((DEPTH, D)),
        'dense_w1': nrm((N_DENSE, D, DENSE_FF), D ** -0.5),
        'dense_w3': nrm((N_DENSE, D, DENSE_FF), D ** -0.5),
        'dense_w2': nrm((N_DENSE, DENSE_FF, D), DENSE_FF ** -0.5),
        'moe_router': nrm((N_MOE, D, N_EXPERTS), D ** -0.5),
        'moe_w1': nrm((N_MOE, N_EXPERTS, D, EXPERT_FF), D ** -0.5),
        'moe_w3': nrm((N_MOE, N_EXPERTS, D, EXPERT_FF), D ** -0.5),
        'moe_w2': nrm((N_MOE, N_EXPERTS, EXPERT_FF, D), EXPERT_FF ** -0.5),
    }


def reference(x, meta_tokens, rel_bias, attn_norm, w_in, mla_cq_norm, mla_ckv_norm, mla_w_uq,
              mla_w_ukv, mla_q_norm, mla_k_norm, swa_q_norm, swa_k_norm, swa_sinks,
              w_branch_mla, w_branch_swa, w_out, ffn_norm, dense_w1, dense_w3, dense_w2,
              moe_router, moe_w1, moe_w3, moe_w2):
    B = x.shape[0]
    meta = jnp.broadcast_to(meta_tokens.astype(x.dtype)[None], (B, N_META, D_MODEL))
    h_res = jnp.concatenate([meta, x], axis=1)
    L = h_res.shape[1]
    offsets = []
    acc = 0
    for s in IN_SPLIT_SIZES[:-1]:
        acc += s
        offsets.append(acc)
    for i in range(DEPTH):
        h = rms_norm(h_res, attn_norm[i])
        proj = h @ w_in[i]
        c_q, c_kv, k_pe, q_s, k_s, v_s, g_a, g_b = jnp.split(proj, offsets, axis=-1)
        a = mla_branch(c_q, c_kv, k_pe, mla_cq_norm[i], mla_ckv_norm[i], mla_w_uq[i],
                       mla_w_ukv[i], mla_q_norm[i], mla_k_norm[i])
        q_s = rms_norm(q_s.reshape(B, L, SWA_HEADS, SWA_HEAD_DIM), swa_q_norm[i])
        k_s = rms_norm(k_s.reshape(B, L, SWA_KV_HEADS, SWA_HEAD_DIM), swa_k_norm[i])
        v_s = v_s.reshape(B, L, SWA_KV_HEADS, SWA_HEAD_DIM)
        b = swa_branch(q_s, k_s, v_s, swa_sinks[i], rel_bias)
        merged = jax.nn.sigmoid(g_a) * (a @ w_branch_mla[i]) + jax.nn.sigmoid(g_b) * (b @ w_branch_swa[i])
        h_res = h_res + merged @ w_out[i]
        h = rms_norm(h_res, ffn_norm[i])
        if i % 2 == 0:
            f = swiglu(h, dense_w1[i // 2], dense_w3[i // 2], dense_w2[i // 2])
        else:
            f = moe_swiglu(h, moe_router[i // 2], moe_w1[i // 2], moe_w3[i // 2], moe_w2[i // 2])
        h_res = h_res + f
    return h_res[:, N_META:]
```

```python
import functools
import math

import jax
import jax.numpy as jnp
from jax import lax
from jax.experimental import pallas as pl
from jax.experimental.pallas import tpu as pltpu

F32 = jnp.float32
BF16 = jnp.bfloat16

N_META = 16
BLOCK = 128
WINDOW = 128
MLA_NOPE = 128
MLA_ROPE = 64
MLA_V = 128
MLA_QK = MLA_NOPE + MLA_ROPE
MLA_QK_PAD = 256
ROPE_THETA = 10000.0
SWA_HEAD_DIM = 64
REL_MAX_DIST = 128
TOP_K = 2
EPS = 1e-6
NEG_INF = -1e30
LANE = 128
META_LO = BLOCK - N_META

VMEM_LIMIT_BIG = 56 * 1024 * 1024
VMEM_LIMIT_MID = 44 * 1024 * 1024


def _params(sem, vmem=VMEM_LIMIT_MID):
    return pltpu.CompilerParams(dimension_semantics=sem, vmem_limit_bytes=vmem)


def _pick(n, target, mult):
    best = None
    d = mult
    while d <= min(n, target):
        if n % d == 0:
            best = d
        d += mult
    return best if best is not None else n


def _round_up(a, b):
    return (a + b - 1) // b * b


def _cast_weight(w_ref, wb_ref):
    k = w_ref.shape[0]
    ch = 512 if k % 512 == 0 else (256 if k % 256 == 0 else 128)
    if k % ch != 0:
        wb_ref[...] = w_ref[...].astype(BF16)
        return

    def body(c, carry):
        r = pl.multiple_of(c * ch, ch)
        wb_ref[pl.ds(r, ch), :] = w_ref[pl.ds(r, ch), :].astype(BF16)
        return carry

    lax.fori_loop(0, k // ch, body, 0)


def _sigmoid(x):
    return 1.0 / (1.0 + jnp.exp(-x))


def _dot(a, b):
    return jnp.dot(a, b, preferred_element_type=F32)


def _dot_nt(a, b):
    return lax.dot_general(a, b, (((1,), (1,)), ((), ())), preferred_element_type=F32)


def _rms_kernel(x_ref, g_ref, o_ref):
    x = x_ref[...]
    ms = jnp.mean(x * x, axis=-1, keepdims=True)
    o_ref[...] = (x * lax.rsqrt(ms + EPS) * g_ref[...]).astype(o_ref.dtype)


def _rms_norm(h, gain):
    m, d = h.shape
    tm = _pick(m, 640, 128)
    return pl.pallas_call(
        _rms_kernel,
        grid=(m // tm,),
        in_specs=[pl.BlockSpec((tm, d), lambda i: (i, 0)),
                  pl.BlockSpec((1, d), lambda i: (0, 0))],
        out_specs=pl.BlockSpec((tm, d), lambda i: (i, 0)),
        out_shape=jax.ShapeDtypeStruct((m, d), BF16),
        compiler_params=_params(("arbitrary",)),
        name="rms_norm",
    )(h, gain.reshape(1, d))


def _proj_kernel(x_ref, w_ref, o_ref):
    o_ref[...] = _dot(x_ref[...], w_ref[...]).astype(o_ref.dtype)


def _proj(hn, wp, tn):
    m, k = hn.shape
    npad = wp.shape[1]
    tm = _pick(m, 640, 128)
    return pl.pallas_call(
        _proj_kernel,
        grid=(npad // tn, m // tm),
        in_specs=[pl.BlockSpec((tm, k), lambda j, i: (i, 0)),
                  pl.BlockSpec((k, tn), lambda j, i: (0, j))],
        out_specs=pl.BlockSpec((tm, tn), lambda j, i: (i, j)),
        out_shape=jax.ShapeDtypeStruct((m, npad), BF16),
        compiler_params=_params(("arbitrary", "arbitrary")),
        name="in_proj",
    )(hn, wp)


def _mm_res_kernel(x_ref, w_ref, r_ref, o_ref, wb_ref):
    @pl.when(pl.program_id(1) == 0)
    def _():
        _cast_weight(w_ref, wb_ref)

    o_ref[...] = r_ref[...] + _dot(x_ref[...], wb_ref[...])


def _mm_res(x, w, lead, kblk, kc, res, name):
    m = x.shape[0]
    n = w.shape[-1]
    tn = _pick(n, 512, 128)
    tm = _pick(m, 640, 128)
    nlead = len(lead)
    w_spec = pl.BlockSpec((None,) * nlead + (kc, tn), lambda j, i: lead + (kblk, j))
    return pl.pallas_call(
        _mm_res_kernel,
        grid=(n // tn, m // tm),
        in_specs=[pl.BlockSpec((tm, kc), lambda j, i: (i, kblk)),
                  w_spec,
                  pl.BlockSpec((tm, tn), lambda j, i: (i, j))],
        out_specs=pl.BlockSpec((tm, tn), lambda j, i: (i, j)),
        out_shape=jax.ShapeDtypeStruct((m, n), F32),
        scratch_shapes=[pltpu.VMEM((kc, tn), BF16)],
        compiler_params=_params(("arbitrary", "arbitrary"), VMEM_LIMIT_BIG),
        name=name,
    )(x, w, res)


def _rope(hi, cos, sin):
    lane = lax.broadcasted_iota(jnp.int32, hi.shape, 1)
    half = MLA_ROPE // 2
    rot = jnp.where(lane < half, pltpu.roll(hi, LANE - half, 1), pltpu.roll(hi, half, 1))
    return hi * cos + rot * sin


def _row_rms_to_scratch(x_ref, g_ref, xn_ref):
    x = x_ref[...].astype(F32)
    ms = jnp.mean(x * x, axis=-1, keepdims=True)
    xn_ref[...] = (x * lax.rsqrt(ms + EPS) * g_ref[...]).astype(BF16)


def _mla_q_kernel(x_ref, g_ref, w_ref, qg_ref, cos_ref, sin_ref, o_ref, xn_ref, *, scale):
    @pl.when(pl.program_id(1) == 0)
    def _():
        _row_rms_to_scratch(x_ref, g_ref, xn_ref)

    q = _dot(xn_ref[...], w_ref[...].astype(BF16))
    ss = jnp.sum(q * q, axis=-1, keepdims=True)
    qn = q * lax.rsqrt(ss * (1.0 / MLA_QK) + EPS) * qg_ref[...]
    hi = _rope(qn[:, LANE:], cos_ref[...], sin_ref[...])
    o_ref[...] = (jnp.concatenate([qn[:, :LANE], hi], axis=-1) * scale).astype(o_ref.dtype)


def _mla_kv_kernel(x_ref, g_ref, w_ref, pe_ref, kg_ref, cos_ref, sin_ref, k_ref, v_ref, xn_ref):
    @pl.when(pl.program_id(1) == 0)
    def _():
        _row_rms_to_scratch(x_ref, g_ref, xn_ref)

    kv = _dot(xn_ref[...], w_ref[...].astype(BF16))
    kn = kv[:, :MLA_NOPE]
    pe = pe_ref[...].astype(F32)
    ss = jnp.sum(kn * kn, axis=-1, keepdims=True) + jnp.sum(pe * pe, axis=-1, keepdims=True)
    r = lax.rsqrt(ss * (1.0 / MLA_QK) + EPS)
    kg = kg_ref[...]
    lo = kn * r * kg[:, :LANE]
    hi = _rope(pe * r * kg[:, LANE:], cos_ref[...], sin_ref[...])
    k_ref[...] = jnp.concatenate([lo, hi], axis=-1).astype(k_ref.dtype)
    v_ref[...] = kv[:, MLA_NOPE:].astype(v_ref.dtype)


def _softmax_step(s, m, l, acc, v):
    m_new = jnp.maximum(m, jnp.max(s, axis=-1, keepdims=True))
    alpha = jnp.exp(m - m_new)
    p = jnp.exp(s - m_new)
    l = alpha * l + jnp.sum(p, axis=-1, keepdims=True)
    acc = alpha * acc + _dot(p.astype(BF16), v)
    return m_new, l, acc


def _mla_attn_kernel(q_ref, k_ref, v_ref, km_ref, vm_ref, o_ref, *, tile, batch):
    seq = q_ref.shape[0]
    nq = seq // tile
    b = pl.program_id(1)

    @pl.when(b < batch)
    def _():
        km = km_ref[...]
        vm = vm_ref[...]
        mcol = lax.broadcasted_iota(jnp.int32, (tile, BLOCK), 1)
        row = lax.broadcasted_iota(jnp.int32, (tile, tile), 0)
        col = lax.broadcasted_iota(jnp.int32, (tile, tile), 1)
        for qi in range(nq):
            q = q_ref[qi * tile:(qi + 1) * tile, :]
            s0 = jnp.where(mcol >= META_LO, _dot_nt(q, km), NEG_INF)
            m = jnp.max(s0, axis=-1, keepdims=True)
            p = jnp.exp(s0 - m)
            l = jnp.sum(p, axis=-1, keepdims=True)
            acc = _dot(p.astype(BF16), vm)

            def kstep(kj, carry, q=q):
                mm, ll, aa = carry
                ks = pl.multiple_of(kj * tile, tile)
                s = _dot_nt(q, k_ref[pl.ds(ks, tile), :])
                return _softmax_step(s, mm, ll, aa, v_ref[pl.ds(ks, tile), :])

            if qi > 0:
                m, l, acc = lax.fori_loop(0, qi, kstep, (m, l, acc))
            s = _dot_nt(q, k_ref[qi * tile:(qi + 1) * tile, :])
            s = jnp.where(row >= col, s, NEG_INF)
            m, l, acc = _softmax_step(s, m, l, acc, v_ref[qi * tile:(qi + 1) * tile, :])
            o_ref[qi * tile:(qi + 1) * tile, :] = (acc / l).astype(o_ref.dtype)

    @pl.when(b == batch)
    def _():
        row = lax.broadcasted_iota(jnp.int32, (BLOCK, BLOCK), 0)
        col = lax.broadcasted_iota(jnp.int32, (BLOCK, BLOCK), 1)
        s = _dot_nt(q_ref[0:BLOCK, :], km_ref[...])
        s = jnp.where((col >= META_LO) & (row >= col), s, NEG_INF)
        m = jnp.max(s, axis=-1, keepdims=True)
        p = jnp.exp(s - m)
        l = jnp.sum(p, axis=-1, keepdims=True)
        o_ref[0:BLOCK, :] = (_dot(p.astype(BF16), vm_ref[...]) / l).astype(o_ref.dtype)


def _mla_branch(proj, lay, cq_g, ckv_g, w_uq, w_ukv, qn_g, kn_g, cos_t, sin_t, batch, seq):
    m = proj.shape[0]
    qr = cq_g.shape[0]
    kvr = ckv_g.shape[0]
    heads = w_uq.shape[1] // MLA_QK
    tm = _pick(m, 640, 128)
    scale = MLA_QK ** -0.5

    w_uq_h = jnp.pad(w_uq.reshape(qr, heads, MLA_QK), ((0, 0), (0, 0), (0, MLA_QK_PAD - MLA_QK)))
    w_uq_h = jnp.transpose(w_uq_h, (1, 0, 2))
    qg = jnp.pad(qn_g, (0, MLA_QK_PAD - MLA_QK)).reshape(1, MLA_QK_PAD)
    kg = jnp.pad(kn_g, (0, MLA_QK_PAD - MLA_QK)).reshape(1, MLA_QK_PAD)

    q = pl.pallas_call(
        functools.partial(_mla_q_kernel, scale=scale),
        grid=(m // tm, heads),
        in_specs=[pl.BlockSpec((tm, qr), lambda i, h: (i, lay["c_q"] // qr)),
                  pl.BlockSpec((1, qr), lambda i, h: (0, 0)),
                  pl.BlockSpec((None, qr, MLA_QK_PAD), lambda i, h: (h, 0, 0)),
                  pl.BlockSpec((1, MLA_QK_PAD), lambda i, h: (0, 0)),
                  pl.BlockSpec((tm, LANE), lambda i, h: (i, 0)),
                  pl.BlockSpec((tm, LANE), lambda i, h: (i, 0))],
        out_specs=pl.BlockSpec((None, tm, MLA_QK_PAD), lambda i, h: (h, i, 0)),
        out_shape=jax.ShapeDtypeStruct((heads, m, MLA_QK_PAD), BF16),
        scratch_shapes=[pltpu.VMEM((tm, qr), BF16)],
        compiler_params=_params(("arbitrary", "arbitrary")),
        name="mla_q",
    )(proj, cq_g.reshape(1, qr), w_uq_h, qg, cos_t, sin_t)

    k, v = pl.pallas_call(
        _mla_kv_kernel,
        grid=(m // tm, heads),
        in_specs=[pl.BlockSpec((tm, kvr), lambda i, h: (i, lay["c_kv"] // kvr)),
                  pl.BlockSpec((1, kvr), lambda i, h: (0, 0)),
                  pl.BlockSpec((kvr, MLA_NOPE + MLA_V), lambda i, h: (0, h)),
                  pl.BlockSpec((tm, LANE), lambda i, h: (i, lay["k_pe"] // LANE)),
                  pl.BlockSpec((1, MLA_QK_PAD), lambda i, h: (0, 0)),
                  pl.BlockSpec((tm, LANE), lambda i, h: (i, 0)),
                  pl.BlockSpec((tm, LANE), lambda i, h: (i, 0))],
        out_specs=[pl.BlockSpec((None, tm, MLA_QK_PAD), lambda i, h: (h, i, 0)),
                   pl.BlockSpec((None, tm, MLA_V), lambda i, h: (h, i, 0))],
        out_shape=[jax.ShapeDtypeStruct((heads, m, MLA_QK_PAD), BF16),
                   jax.ShapeDtypeStruct((heads, m, MLA_V), BF16)],
        scratch_shapes=[pltpu.VMEM((tm, kvr), BF16)],
        compiler_params=_params(("arbitrary", "arbitrary")),
        name="mla_kv",
    )(proj, ckv_g.reshape(1, kvr), w_ukv, proj, kg, cos_t, sin_t)

    meta_blk = m // BLOCK - 1
    tile = _pick(seq, 256, 128)
    a = pl.pallas_call(
        functools.partial(_mla_attn_kernel, tile=tile, batch=batch),
        grid=(heads, batch + 1),
        in_specs=[pl.BlockSpec((None, seq, MLA_QK_PAD), lambda h, b: (h, b, 0)),
                  pl.BlockSpec((None, seq, MLA_QK_PAD), lambda h, b: (h, b, 0)),
                  pl.BlockSpec((None, seq, MLA_V), lambda h, b: (h, b, 0)),
                  pl.BlockSpec((None, BLOCK, MLA_QK_PAD), lambda h, b: (h, meta_blk, 0)),
                  pl.BlockSpec((None, BLOCK, MLA_V), lambda h, b: (h, meta_blk, 0))],
        out_specs=pl.BlockSpec((seq, MLA_V), lambda h, b: (b, h)),
        out_shape=jax.ShapeDtypeStruct((m, heads * MLA_V), BF16),
        compiler_params=_params(("arbitrary", "arbitrary")),
        name="mla_attn",
    )(q, k, v, k, v)
    return a


def _t5_bucket(rel, buckets):
    n = jnp.maximum(rel, 0)
    max_exact = buckets // 2
    nf = jnp.maximum(n, 1).astype(F32)
    large = max_exact + (jnp.log(nf / max_exact) / math.log(REL_MAX_DIST / max_exact)
                         * (buckets - max_exact)).astype(jnp.int32)
    large = jnp.minimum(large, buckets - 1)
    return jnp.where(n < max_exact, n, large)


def _bias_kernel(rb_ref, bucket_ref, o_ref, *, buckets):
    h = pl.program_id(0)
    bucket = bucket_ref[...]
    acc = jnp.zeros(bucket.shape, F32)
    for b in range(buckets):
        acc = jnp.where(bucket == b, rb_ref[b, h], acc)
    o_ref[...] = acc


def _bias_table(rel_bias):
    buckets, hq = rel_bias.shape
    qi = jnp.arange(BLOCK)[:, None]
    sj = jnp.arange(2 * BLOCK)[None, :]
    bucket = _t5_bucket(qi + BLOCK - sj, buckets).astype(jnp.int32)
    return pl.pallas_call(
        functools.partial(_bias_kernel, buckets=buckets),
        grid=(hq,),
        in_specs=[pl.BlockSpec(memory_space=pltpu.SMEM),
                  pl.BlockSpec((BLOCK, 2 * BLOCK), lambda h: (0, 0))],
        out_specs=pl.BlockSpec((None, BLOCK, 2 * BLOCK), lambda h: (h, 0, 0)),
        out_shape=jax.ShapeDtypeStruct((hq, BLOCK, 2 * BLOCK), F32),
        compiler_params=_params(("arbitrary",)),
        name="rel_bias_table",
    )(rel_bias, bucket)


def _head_rms(x, gain):
    ms = jnp.mean(x * x, axis=-1, keepdims=True)
    return x * lax.rsqrt(ms + EPS) * gain


def _swa_kernel(prev_ref, plo_ref, clo_ref, q_ref, kc_ref, kp_ref, vc_ref, vp_ref, bias_ref,
                sink_ref, qg_ref, kg_ref, o_ref, *, hq, hkv):
    t = pl.program_id(0)
    plo = plo_ref[t]
    clo = clo_ref[t]
    hd = SWA_HEAD_DIM
    group = hq // hkv
    qi = lax.broadcasted_iota(jnp.int32, (BLOCK, 2 * BLOCK), 0)
    sj = lax.broadcasted_iota(jnp.int32, (BLOCK, 2 * BLOCK), 1)
    d = sj - qi
    in_prev = sj < BLOCK
    krow = jnp.where(in_prev, sj, sj - BLOCK)
    lo = jnp.where(in_prev, plo, clo)
    valid = (d >= 1) & (d <= WINDOW) & (krow >= lo)

    q_all = q_ref[...].astype(F32)
    k_all = jnp.concatenate([kp_ref[...], kc_ref[...]], axis=0).astype(F32)
    v_all = jnp.concatenate([vp_ref[...], vc_ref[...]], axis=0)
    qg = qg_ref[...] * (hd ** -0.5)
    kg = kg_ref[...]
    outs = []
    for g in range(hkv):
        k = _head_rms(k_all[:, g * hd:(g + 1) * hd], kg).astype(BF16)
        v = v_all[:, g * hd:(g + 1) * hd]
        for hh in range(group):
            h = g * group + hh
            qn = _head_rms(q_all[:, h * hd:(h + 1) * hd], qg).astype(BF16)
            s = _dot_nt(qn, k) + bias_ref[h]
            s = jnp.where(valid, s, NEG_INF)
            sk = sink_ref[h]
            m = jnp.maximum(jnp.max(s, axis=-1, keepdims=True), sk)
            e = jnp.exp(s - m)
            den = jnp.sum(e, axis=-1, keepdims=True) + jnp.exp(sk - m)
            outs.append(_dot((e / den).astype(BF16), v))
    o_ref[...] = jnp.concatenate(outs, axis=-1).astype(o_ref.dtype)


def _swa_branch(proj, lay, bias_tab, sinks, qg, kg, batch, seq):
    m = proj.shape[0]
    hq = sinks.shape[0]
    hd = SWA_HEAD_DIM
    qw = hq * hd
    kw = lay["v_s"] - lay["k_s"]
    hkv = kw // hd
    nblk = m // BLOCK
    per = seq // BLOCK
    t = jnp.arange(nblk, dtype=jnp.int32)
    is_meta = t == nblk - 1
    first = (t % per) == 0
    prev = jnp.where(is_meta | first, nblk - 1, t - 1).astype(jnp.int32)
    plo = jnp.where(is_meta, BLOCK, jnp.where(first, META_LO, 0)).astype(jnp.int32)
    clo = jnp.where(is_meta, META_LO, 0).astype(jnp.int32)
    grid_spec = pltpu.PrefetchScalarGridSpec(
        num_scalar_prefetch=3,
        grid=(nblk,),
        in_specs=[pl.BlockSpec((BLOCK, qw), lambda i, pr, pl_, cl: (i, lay["q_s"] // qw)),
                  pl.BlockSpec((BLOCK, kw), lambda i, pr, pl_, cl: (i, lay["k_s"] // kw)),
                  pl.BlockSpec((BLOCK, kw), lambda i, pr, pl_, cl: (pr[i], lay["k_s"] // kw)),
                  pl.BlockSpec((BLOCK, kw), lambda i, pr, pl_, cl: (i, lay["v_s"] // kw)),
                  pl.BlockSpec((BLOCK, kw), lambda i, pr, pl_, cl: (pr[i], lay["v_s"] // kw)),
                  pl.BlockSpec((hq, BLOCK, 2 * BLOCK), lambda i, pr, pl_, cl: (0, 0, 0)),
                  pl.BlockSpec(memory_space=pltpu.SMEM),
                  pl.BlockSpec((1, hd), lambda i, pr, pl_, cl: (0, 0)),
                  pl.BlockSpec((1, hd), lambda i, pr, pl_, cl: (0, 0))],
        out_specs=pl.BlockSpec((BLOCK, qw), lambda i, pr, pl_, cl: (i, 0)),
    )
    return pl.pallas_call(
        functools.partial(_swa_kernel, hq=hq, hkv=hkv),
        grid_spec=grid_spec,
        out_shape=jax.ShapeDtypeStruct((m, qw), BF16),
        compiler_params=_params(("arbitrary",)),
        name="swa_attn",
    )(prev, plo, clo, proj, proj, proj, proj, proj, bias_tab, sinks, qg.reshape(1, hd), kg.reshape(1, hd))


def _merge_kernel(a_ref, b_ref, wa_ref, wb_ref, ga_ref, gb_ref, o_ref, wab_ref, wbb_ref):
    @pl.when(pl.program_id(1) == 0)
    def _():
        _cast_weight(wa_ref, wab_ref)
        _cast_weight(wb_ref, wbb_ref)

    ya = _dot(a_ref[...], wab_ref[...])
    yb = _dot(b_ref[...], wbb_ref[...])
    ga = _sigmoid(ga_ref[...].astype(F32))
    gb = _sigmoid(gb_ref[...].astype(F32))
    o_ref[...] = (ga * ya + gb * yb).astype(o_ref.dtype)


def _merge(a, b, w_a, w_b, layer, proj, lay, tn):
    m, ka = a.shape
    kb = b.shape[1]
    d = w_a.shape[-1]
    tm = _pick(m, 640, 128)
    ga0 = lay["g_a"] // tn
    gb0 = lay["g_b"] // tn
    return pl.pallas_call(
        _merge_kernel,
        grid=(d // tn, m // tm),
        in_specs=[pl.BlockSpec((tm, ka), lambda j, i: (i, 0)),
                  pl.BlockSpec((tm, kb), lambda j, i: (i, 0)),
                  pl.BlockSpec((None, ka, tn), lambda j, i: (layer, 0, j)),
                  pl.BlockSpec((None, kb, tn), lambda j, i: (layer, 0, j)),
                  pl.BlockSpec((tm, tn), lambda j, i: (i, ga0 + j)),
                  pl.BlockSpec((tm, tn), lambda j, i: (i, gb0 + j))],
        out_specs=pl.BlockSpec((tm, tn), lambda j, i: (i, j)),
        out_shape=jax.ShapeDtypeStruct((m, d), BF16),
        scratch_shapes=[pltpu.VMEM((ka, tn), BF16), pltpu.VMEM((kb, tn), BF16)],
        compiler_params=_params(("arbitrary", "arbitrary")),
        name="branch_merge",
    )(a, b, w_a, w_b, proj, proj)


def _ffn_up_kernel(x_ref, w1_ref, w3_ref, o_ref, w1b_ref, w3b_ref):
    @pl.when(pl.program_id(1) == 0)
    def _():
        _cast_weight(w1_ref, w1b_ref)
        _cast_weight(w3_ref, w3b_ref)

    x = x_ref[...]
    u = _dot(x, w1b_ref[...])
    g = _dot(x, w3b_ref[...])
    o_ref[...] = (u * _sigmoid(u) * g).astype(o_ref.dtype)


def _ffn_up(hn, w1, w3, idx):
    m, d = hn.shape
    ff = w1.shape[-1]
    tn = _pick(ff, 256, 128)
    tm = _pick(m, 640, 128)
    return pl.pallas_call(
        _ffn_up_kernel,
        grid=(ff // tn, m // tm),
        in_specs=[pl.BlockSpec((tm, d), lambda j, i: (i, 0)),
                  pl.BlockSpec((None, d, tn), lambda j, i: (idx, 0, j)),
                  pl.BlockSpec((None, d, tn), lambda j, i: (idx, 0, j))],
        out_specs=pl.BlockSpec((tm, tn), lambda j, i: (i, j)),
        out_shape=jax.ShapeDtypeStruct((m, ff), BF16),
        scratch_shapes=[pltpu.VMEM((d, tn), BF16), pltpu.VMEM((d, tn), BF16)],
        compiler_params=_params(("arbitrary", "arbitrary")),
        name="ffn_up",
    )(hn, w1, w3)


def _dense_ffn(h_res, gain, w1, w3, w2, idx):
    hn = _rms_norm(h_res, gain)
    t = _ffn_up(hn, w1, w3, idx)
    ff = w1.shape[-1]
    nk = 1
    for cand in (1, 2, 4):
        if ff % cand == 0 and (ff // cand) % LANE == 0 and ff // cand <= 6144:
            nk = cand
            break
    kc = ff // nk
    out = h_res
    for kblk in range(nk):
        out = _mm_res(t, w2, (idx,), kblk, kc, out, "ffn_down")
    return out


def _router_kernel(x_ref, g_ref, w_ref, o_ref, *, n_exp):
    x = x_ref[...]
    ms = jnp.mean(x * x, axis=-1, keepdims=True)
    xn = (x * lax.rsqrt(ms + EPS) * g_ref[...]).astype(BF16)
    logits = _dot(xn, w_ref[...].astype(BF16))
    lane = lax.broadcasted_iota(jnp.int32, logits.shape, 1)
    logits = jnp.where(lane < n_exp, logits, -jnp.inf)
    lane_f = lane.astype(F32)
    m1 = jnp.max(logits, axis=-1, keepdims=True)
    i1 = jnp.min(jnp.where(logits == m1, lane_f, float(LANE)), axis=-1, keepdims=True)
    rest = jnp.where(lane_f == i1, -jnp.inf, logits)
    m2 = jnp.max(rest, axis=-1, keepdims=True)
    i2 = jnp.min(jnp.where(rest == m2, lane_f, float(LANE)), axis=-1, keepdims=True)
    e2 = jnp.exp(m2 - m1)
    w1 = 1.0 / (1.0 + e2)
    w2 = e2 / (1.0 + e2)
    out = jnp.where(lane == 0, i1, jnp.where(lane == 1, i2,
                                             jnp.where(lane == 2, w1, jnp.where(lane == 3, w2, 0.0))))
    o_ref[...] = out


def _router(h_res, gain, router_w):
    m, d = h_res.shape
    n_exp = router_w.shape[-1]
    tm = _pick(m, 640, 128)
    wpad = jnp.pad(router_w, ((0, 0), (0, LANE - n_exp)))
    return pl.pallas_call(
        functools.partial(_router_kernel, n_exp=n_exp),
        grid=(m // tm,),
        in_specs=[pl.BlockSpec((tm, d), lambda i: (i, 0)),
                  pl.BlockSpec((1, d), lambda i: (0, 0)),
                  pl.BlockSpec((d, LANE), lambda i: (0, 0))],
        out_specs=pl.BlockSpec((tm, LANE), lambda i: (i, 0)),
        out_shape=jax.ShapeDtypeStruct((m, LANE), F32),
        compiler_params=_params(("arbitrary",)),
        name="moe_router",
    )(h_res, gain.reshape(1, d), wpad)


def _row_copy(src_hbm, buf, sem, slot, src_row, dst_row):
    return pltpu.make_async_copy(src_hbm.at[pl.ds(src_row, 1)], buf.at[slot, pl.ds(dst_row, 1)],
                                 sem.at[slot])


def _gather_norm_kernel(tok_ref, h_hbm, g_ref, o_ref, buf, sem, *, rows):
    s = pl.program_id(0)
    n = pl.num_programs(0)

    def start(step, slot):
        def body(r, c):
            _row_copy(h_hbm, buf, sem, slot, tok_ref[step * rows + r], r).start()
            return c
        lax.fori_loop(0, rows, body, 0)

    @pl.when(s == 0)
    def _():
        start(0, 0)

    @pl.when(s + 1 < n)
    def _():
        start(s + 1, (s + 1) % 2)

    slot = s % 2

    def wbody(r, c):
        _row_copy(h_hbm, buf, sem, slot, 0, r).wait()
        return c
    lax.fori_loop(0, rows, wbody, 0)

    x = buf[slot]
    ms = jnp.mean(x * x, axis=-1, keepdims=True)
    o_ref[...] = (x * lax.rsqrt(ms + EPS) * g_ref[...]).astype(o_ref.dtype)


def _gather_norm(h_res, gain, row_tok, rows):
    d = h_res.shape[1]
    rp = row_tok.shape[0]
    grid_spec = pltpu.PrefetchScalarGridSpec(
        num_scalar_prefetch=1,
        grid=(rp // rows,),
        in_specs=[pl.BlockSpec(memory_space=pl.ANY),
                  pl.BlockSpec((1, d), lambda i, tok: (0, 0))],
        out_specs=pl.BlockSpec((rows, d), lambda i, tok: (i, 0)),
        scratch_shapes=[pltpu.VMEM((2, rows, d), F32), pltpu.SemaphoreType.DMA((2,))],
    )
    return pl.pallas_call(
        functools.partial(_gather_norm_kernel, rows=rows),
        grid_spec=grid_spec,
        out_shape=jax.ShapeDtypeStruct((rp, d), BF16),
        compiler_params=_params(("arbitrary",)),
        name="moe_gather_norm",
    )(row_tok, h_res, gain.reshape(1, d))


def _moe_up_kernel(be_ref, first_ref, valid_ref, xb_ref, x_ref, w1_ref, w3_ref, o_ref, w1b_ref, w3b_ref):
    i = pl.program_id(1)

    @pl.when(first_ref[i] == 1)
    def _():
        _cast_weight(w1_ref, w1b_ref)
        _cast_weight(w3_ref, w3b_ref)

    @pl.when(valid_ref[i] == 1)
    def _():
        x = x_ref[...]
        u = _dot(x, w1b_ref[...])
        g = _dot(x, w3b_ref[...])
        o_ref[...] = (u * _sigmoid(u) * g).astype(o_ref.dtype)

    @pl.when(valid_ref[i] == 0)
    def _():
        o_ref[...] = jnp.zeros(o_ref.shape, o_ref.dtype)


def _moe_down_kernel(be_ref, first_ref, valid_ref, xb_ref, x_ref, w_ref, o_ref, wb_ref):
    i = pl.program_id(1)

    @pl.when(first_ref[i] == 1)
    def _():
        _cast_weight(w_ref, wb_ref)

    @pl.when(valid_ref[i] == 1)
    def _():
        o_ref[...] = _dot(x_ref[...], wb_ref[...])

    @pl.when(valid_ref[i] == 0)
    def _():
        o_ref[...] = jnp.zeros(o_ref.shape, o_ref.dtype)


def _combine_kernel(pos_ref, h_ref, rt_ref, y_hbm, o_ref, buf, sem, *, rows):
    s = pl.program_id(0)
    n = pl.num_programs(0)

    def start(step, slot):
        def body(r, c):
            _row_copy(y_hbm, buf, sem, slot, pos_ref[step * 2 * rows + r], r).start()
            return c
        lax.fori_loop(0, 2 * rows, body, 0)

    @pl.when(s == 0)
    def _():
        start(0, 0)

    @pl.when(s + 1 < n)
    def _():
        start(s + 1, (s + 1) % 2)

    slot = s % 2

    def wbody(r, c):
        _row_copy(y_hbm, buf, sem, slot, 0, r).wait()
        return c
    lax.fori_loop(0, 2 * rows, wbody, 0)

    rt = rt_ref[...]
    y0 = buf[slot, 0:rows, :]
    y1 = buf[slot, rows:2 * rows, :]
    o_ref[...] = h_ref[...] + rt[:, 2:3] * y0 + rt[:, 3:4] * y1


def _moe_ffn(h_res, gain, router_w, w1, w3, w2, idx, n_out):
    m, d = h_res.shape
    n_exp = router_w.shape[-1]
    ff = w1.shape[-1]
    rt = _router(h_res, gain, router_w)

    tme = 256 if (TOP_K * m) % 256 == 0 else 128
    r_tot = TOP_K * m
    nb = r_tot // tme + n_exp
    rp = nb * tme
    e = rt[:, :TOP_K].astype(jnp.int32).reshape(-1)
    onehot = (e[:, None] == jnp.arange(n_exp, dtype=jnp.int32)[None, :]).astype(jnp.int32)
    csum = jnp.cumsum(onehot, axis=0)
    rank = jnp.sum(onehot * csum, axis=1) - 1
    counts = csum[-1]
    padded = (counts + tme - 1) // tme * tme
    gend = jnp.cumsum(padded)
    goff = gend - padded
    pos = (jnp.sum(onehot * goff[None, :], axis=1) + rank).astype(jnp.int32)
    row_tok = jnp.zeros((rp,), jnp.int32).at[pos].set(jnp.arange(r_tot, dtype=jnp.int32) // TOP_K)
    used = gend[-1] // tme
    blk = jnp.arange(nb, dtype=jnp.int32)
    be = jnp.minimum(jnp.sum((blk[:, None] * tme >= gend[None, :]).astype(jnp.int32), axis=1),
                     n_exp - 1).astype(jnp.int32)
    valid = (blk < used).astype(jnp.int32)
    first = jnp.concatenate([jnp.ones((1,), jnp.int32), (be[1:] != be[:-1]).astype(jnp.int32)])
    xb = jnp.minimum(blk, used - 1).astype(jnp.int32)

    xs = _gather_norm(h_res, gain, row_tok, 128)

    tn_up = _pick(ff, 512, 128)
    up_spec = pltpu.PrefetchScalarGridSpec(
        num_scalar_prefetch=4,
        grid=(ff // tn_up, nb),
        in_specs=[pl.BlockSpec((tme, d), lambda j, i, be_, f_, v_, xb_: (xb_[i], 0)),
                  pl.BlockSpec((None, None, d, tn_up), lambda j, i, be_, f_, v_, xb_: (idx, be_[i], 0, j)),
                  pl.BlockSpec((None, None, d, tn_up), lambda j, i, be_, f_, v_, xb_: (idx, be_[i], 0, j))],
        out_specs=pl.BlockSpec((tme, tn_up), lambda j, i, be_, f_, v_, xb_: (i, j)),
        scratch_shapes=[pltpu.VMEM((d, tn_up), BF16), pltpu.VMEM((d, tn_up), BF16)],
    )
    t = pl.pallas_call(
        _moe_up_kernel,
        grid_spec=up_spec,
        out_shape=jax.ShapeDtypeStruct((rp, ff), BF16),
        compiler_params=_params(("arbitrary", "arbitrary"), VMEM_LIMIT_BIG),
        name="moe_up",
    )(be, first, valid, xb, xs, w1, w3)

    tn_dn = _pick(d, 512, 128)
    dn_spec = pltpu.PrefetchScalarGridSpec(
        num_scalar_prefetch=4,
        grid=(d // tn_dn, nb),
        in_specs=[pl.BlockSpec((tme, ff), lambda j, i, be_, f_, v_, xb_: (xb_[i], 0)),
                  pl.BlockSpec((None, None, ff, tn_dn), lambda j, i, be_, f_, v_, xb_: (idx, be_[i], 0, j))],
        out_specs=pl.BlockSpec((tme, tn_dn), lambda j, i, be_, f_, v_, xb_: (i, j)),
        scratch_shapes=[pltpu.VMEM((ff, tn_dn), BF16)],
    )
    y = pl.pallas_call(
        _moe_down_kernel,
        grid_spec=dn_spec,
        out_shape=jax.ShapeDtypeStruct((rp, d), F32),
        compiler_params=_params(("arbitrary", "arbitrary"), VMEM_LIMIT_BIG),
        name="moe_down",
    )(be, first, valid, xb, t, w2)

    rows = 128
    cmb_spec = pltpu.PrefetchScalarGridSpec(
        num_scalar_prefetch=1,
        grid=(n_out // rows,),
        in_specs=[pl.BlockSpec((rows, d), lambda i, p: (i, 0)),
                  pl.BlockSpec((rows, LANE), lambda i, p: (i, 0)),
                  pl.BlockSpec(memory_space=pl.ANY)],
        out_specs=pl.BlockSpec((rows, d), lambda i, p: (i, 0)),
        scratch_shapes=[pltpu.VMEM((2, 2 * rows, d), F32), pltpu.SemaphoreType.DMA((2,))],
    )
    pos_blk = jnp.transpose(pos.reshape(m // rows, rows, TOP_K), (0, 2, 1)).reshape(-1)
    return pl.pallas_call(
        functools.partial(_combine_kernel, rows=rows),
        grid_spec=cmb_spec,
        out_shape=jax.ShapeDtypeStruct((n_out, d), F32),
        compiler_params=_params(("arbitrary",)),
        name="moe_combine",
    )(pos_blk, h_res, rt, y)


def _in_layout(d, qr, kvr, qw, kw, tn):
    segs = [("c_q", qr, qr), ("c_kv", kvr, kvr), ("k_pe", MLA_ROPE, LANE), ("q_s", qw, qw),
            ("k_s", kw, kw), ("v_s", kw, kw), ("g_a", d, tn), ("g_b", d, tn)]
    lay = {}
    cur = 0
    for name, width, align in segs:
        cur = _round_up(cur, align)
        lay[name] = cur
        cur += width
    lay["total"] = _round_up(cur, tn)
    return lay, segs


def _prep_w_in(w, lay, segs):
    d = w.shape[0]
    pieces = []
    cur = 0
    src = 0
    for name, width, _ in segs:
        if lay[name] > cur:
            pieces.append(jnp.zeros((d, lay[name] - cur), BF16))
        pieces.append(w[:, src:src + width].astype(BF16))
        cur = lay[name] + width
        src += width
    if lay["total"] > cur:
        pieces.append(jnp.zeros((d, lay["total"] - cur), BF16))
    return jnp.concatenate(pieces, axis=1)


def kernel(x, meta_tokens, rel_bias, attn_norm, w_in, mla_cq_norm, mla_ckv_norm, mla_w_uq, mla_w_ukv,
           mla_q_norm, mla_k_norm, swa_q_norm, swa_k_norm, swa_sinks, w_branch_mla, w_branch_swa, w_out,
           ffn_norm, dense_w1, dense_w3, dense_w2, moe_router, moe_w1, moe_w3, moe_w2):
    batch, seq, d = x.shape
    depth = w_in.shape[0]
    qr = mla_cq_norm.shape[1]
    kvr = mla_ckv_norm.shape[1]
    hq = swa_sinks.shape[1]
    qw = hq * SWA_HEAD_DIM
    kw = (w_in.shape[2] - qr - kvr - MLA_ROPE - qw - 2 * d) // 2
    assert seq % BLOCK == 0 and meta_tokens.shape[0] == N_META
    n_tok = batch * seq
    m = n_tok + BLOCK

    tn_in = _pick(d, 512, 128)
    lay, segs = _in_layout(d, qr, kvr, qw, kw, tn_in)

    h_res = jnp.concatenate([x.reshape(n_tok, d), jnp.zeros((META_LO, d), x.dtype),
                             meta_tokens.astype(x.dtype)], axis=0)

    pos = jnp.concatenate([jnp.tile(N_META + jnp.arange(seq), batch), jnp.arange(BLOCK) - META_LO])
    half = MLA_ROPE // 2
    inv_freq = ROPE_THETA ** (-jnp.arange(half, dtype=F32) / half)
    ang = pos.astype(F32)[:, None] * inv_freq[None, :]
    cos, sin = jnp.cos(ang), jnp.sin(ang)
    cos_t = jnp.concatenate([cos, cos, jnp.ones((m, LANE - MLA_ROPE), F32)], axis=1)
    sin_t = jnp.concatenate([-sin, sin, jnp.zeros((m, LANE - MLA_ROPE), F32)], axis=1)

    bias_tab = _bias_table(rel_bias)

    for i in range(depth):
        hn = _rms_norm(h_res, attn_norm[i])
        proj = _proj(hn, _prep_w_in(w_in[i], lay, segs), tn_in)
        a = _mla_branch(proj, lay, mla_cq_norm[i], mla_ckv_norm[i], mla_w_uq[i], mla_w_ukv[i],
                        mla_q_norm[i], mla_k_norm[i], cos_t, sin_t, batch, seq)
        b = _swa_branch(proj, lay, bias_tab, swa_sinks[i], swa_q_norm[i], swa_k_norm[i], batch, seq)
        merged = _merge(a, b, w_branch_mla, w_branch_swa, i, proj, lay, tn_in)
        h_res = _mm_res(merged, w_out, (i,), 0, d, h_res, "out_proj")
        last = i == depth - 1
        if i % 2 == 0:
            h_res = _dense_ffn(h_res, ffn_norm[i], dense_w1, dense_w3, dense_w2, i // 2)
            if last:
                h_res = h_res[:n_tok]
        else:
            h_res = _moe_ffn(h_res, ffn_norm[i], moe_router[i // 2], moe_w1, moe_w3, moe_w2, i // 2,
                             n_tok if last else m)
    return h_res.reshape(batch, seq, d)
```

```python
import functools
import math

import jax
import jax.numpy as jnp
from jax import lax
from jax.experimental import pallas as pl
from jax.experimental.pallas import tpu as pltpu

F32 = jnp.float32
BF16 = jnp.bfloat16

N_META = 16
BLOCK = 128
WINDOW = 128
MLA_NOPE = 128
MLA_ROPE = 64
MLA_V = 128
MLA_QK = MLA_NOPE + MLA_ROPE
MLA_QK_PAD = 256
ROPE_THETA = 10000.0
SWA_HEAD_DIM = 64
REL_MAX_DIST = 128
TOP_K = 2
EPS = 1e-6
NEG_INF = -1e30
LOG2E = 1.4426950408889634
LANE = 128
META_LO = BLOCK - N_META

VMEM_LIMIT_BIG = 56 * 1024 * 1024
VMEM_LIMIT_MID = 44 * 1024 * 1024


def _params(sem, vmem=VMEM_LIMIT_MID):
    return pltpu.CompilerParams(dimension_semantics=sem, vmem_limit_bytes=vmem)


def _pick(n, target, mult):
    best = None
    d = mult
    while d <= min(n, target):
        if n % d == 0:
            best = d
        d += mult
    return best if best is not None else n


def _round_up(a, b):
    return (a + b - 1) // b * b


def _cast_weight(w_ref, wb_ref):
    k = w_ref.shape[0]
    ch = 512 if k % 512 == 0 else (256 if k % 256 == 0 else 128)
    if k % ch != 0:
        wb_ref[...] = w_ref[...].astype(BF16)
        return

    def body(c, carry):
        r = pl.multiple_of(c * ch, ch)
        wb_ref[pl.ds(r, ch), :] = w_ref[pl.ds(r, ch), :].astype(BF16)
        return carry

    lax.fori_loop(0, k // ch, body, 0)


def _sigmoid(x):
    return 1.0 / (1.0 + jnp.exp(-x))


def _dot(a, b):
    return jnp.dot(a, b, preferred_element_type=F32)


def _dot_nt(a, b):
    return lax.dot_general(a, b, (((1,), (1,)), ((), ())), preferred_element_type=F32)


def _rms_kernel(x_ref, g_ref, o_ref):
    x = x_ref[...]
    ms = jnp.mean(x * x, axis=-1, keepdims=True)
    o_ref[...] = (x * lax.rsqrt(ms + EPS) * g_ref[...]).astype(o_ref.dtype)


def _rms_norm(h, gain):
    m, d = h.shape
    tm = _pick(m, 640, 128)
    return pl.pallas_call(
        _rms_kernel,
        grid=(m // tm,),
        in_specs=[pl.BlockSpec((tm, d), lambda i: (i, 0)),
                  pl.BlockSpec((1, d), lambda i: (0, 0))],
        out_specs=pl.BlockSpec((tm, d), lambda i: (i, 0)),
        out_shape=jax.ShapeDtypeStruct((m, d), BF16),
        compiler_params=_params(("arbitrary",)),
        name="rms_norm",
    )(h, gain.reshape(1, d))


def _in_proj_kernel(x_ref, a_ref, b_ref, o_ref, wb_ref, *, nh):
    j = pl.program_id(0)
    k, tn = a_ref.shape
    ch = 256 if k % 256 == 0 else 128
    half = LANE // 2

    def chunks(fn):
        def body(c, carry):
            r = pl.multiple_of(c * ch, ch)
            wb_ref[pl.ds(r, ch), :] = fn(r).astype(BF16)
            return carry
        lax.fori_loop(0, k // ch, body, 0)

    @pl.when(pl.program_id(1) == 0)
    def _():
        @pl.when(j < nh)
        def _():
            chunks(lambda r: a_ref[pl.ds(r, ch), :])

        @pl.when(j == nh)
        def _():
            lane = lax.broadcasted_iota(jnp.int32, (ch, tn), 1)
            chunks(lambda r: jnp.where(lane < MLA_ROPE, a_ref[pl.ds(r, ch), :], 0.0))

        @pl.when(j > nh)
        def _():
            chunks(lambda r: jnp.concatenate([a_ref[pl.ds(r, ch), half:], b_ref[pl.ds(r, ch), :half]],
                                             axis=1))

    o_ref[...] = _dot(x_ref[...], wb_ref[...]).astype(o_ref.dtype)


def _in_proj(hn, w_in, layer, lay, tn):
    m, k = hn.shape
    nh = lay["k_pe"] // tn
    nblk = lay["total"] // tn
    tm = _pick(m, 640, 128)
    sub = tn // LANE
    last_b = (w_in.shape[2] - 1) // LANE

    def a_map(j, i):
        return (layer, 0, jnp.where(j > nh, j - 1, j))

    def b_map(j, i):
        return (layer, 0, jnp.minimum(jnp.maximum(j, nh + 1) * sub, last_b))

    return pl.pallas_call(
        functools.partial(_in_proj_kernel, nh=nh),
        grid=(nblk, m // tm),
        in_specs=[pl.BlockSpec((tm, k), lambda j, i: (i, 0)),
                  pl.BlockSpec((None, k, tn), a_map),
                  pl.BlockSpec((None, k, LANE), b_map)],
        out_specs=pl.BlockSpec((tm, tn), lambda j, i: (i, j)),
        out_shape=jax.ShapeDtypeStruct((m, lay["total"]), BF16),
        scratch_shapes=[pltpu.VMEM((k, tn), BF16)],
        compiler_params=_params(("arbitrary", "arbitrary")),
        name="in_proj",
    )(hn, w_in, w_in)


def _mm_res_kernel(x_ref, w_ref, r_ref, o_ref, wb_ref):
    @pl.when(pl.program_id(1) == 0)
    def _():
        _cast_weight(w_ref, wb_ref)

    o_ref[...] = r_ref[...] + _dot(x_ref[...], wb_ref[...])


def _mm_res(x, w, lead, kblk, kc, res, name):
    m = x.shape[0]
    n = w.shape[-1]
    tn = _pick(n, 512, 128)
    tm = _pick(m, 640, 128)
    nlead = len(lead)
    w_spec = pl.BlockSpec((None,) * nlead + (kc, tn), lambda j, i: lead + (kblk, j))
    return pl.pallas_call(
        _mm_res_kernel,
        grid=(n // tn, m // tm),
        in_specs=[pl.BlockSpec((tm, kc), lambda j, i: (i, kblk)),
                  w_spec,
                  pl.BlockSpec((tm, tn), lambda j, i: (i, j))],
        out_specs=pl.BlockSpec((tm, tn), lambda j, i: (i, j)),
        out_shape=jax.ShapeDtypeStruct((m, n), F32),
        scratch_shapes=[pltpu.VMEM((kc, tn), BF16)],
        compiler_params=_params(("arbitrary", "arbitrary"), VMEM_LIMIT_BIG),
        name=name,
    )(x, w, res)


def _rope(hi, cos, sin):
    lane = lax.broadcasted_iota(jnp.int32, hi.shape, 1)
    half = MLA_ROPE // 2
    rot = jnp.where(lane < half, pltpu.roll(hi, LANE - half, 1), pltpu.roll(hi, half, 1))
    return hi * cos + rot * sin


def _row_rms(x_ref, g_ref):
    x = x_ref[...].astype(F32)
    ms = jnp.mean(x * x, axis=-1, keepdims=True)
    return (x * lax.rsqrt(ms + EPS) * g_ref[...]).astype(BF16)


def _mla_q_kernel(x_ref, g_ref, w_ref, qg_ref, cos_ref, sin_ref, o_ref, *, heads):
    xn = _row_rms(x_ref, g_ref)
    qg = qg_ref[...]
    cos = cos_ref[...]
    sin = sin_ref[...]
    for h in range(heads):
        q = _dot(xn, w_ref[h])
        ss = jnp.sum(q * q, axis=-1, keepdims=True)
        qn = q * lax.rsqrt(ss * (1.0 / MLA_QK) + EPS) * qg
        hi = _rope(qn[:, LANE:], cos, sin)
        o_ref[h] = jnp.concatenate([qn[:, :LANE], hi], axis=-1).astype(o_ref.dtype)


def _mla_kv_kernel(x_ref, g_ref, w_ref, pe_ref, kg_ref, cos_ref, sin_ref, k_ref, v_ref, *, heads):
    xn = _row_rms(x_ref, g_ref)
    kg = kg_ref[...]
    pe = pe_ref[...].astype(F32)
    pe_ss = jnp.sum(pe * pe, axis=-1, keepdims=True)
    pe_rot = _rope(pe * kg[:, LANE:], cos_ref[...], sin_ref[...])
    ones = jnp.ones((xn.shape[0], MLA_V), v_ref.dtype)
    width = MLA_NOPE + MLA_V
    for h in range(heads):
        kv = _dot(xn, w_ref[:, h * width:(h + 1) * width])
        kn = kv[:, :MLA_NOPE]
        r = lax.rsqrt((jnp.sum(kn * kn, axis=-1, keepdims=True) + pe_ss) * (1.0 / MLA_QK) + EPS)
        k_ref[h] = jnp.concatenate([kn * r * kg[:, :LANE], pe_rot * r], axis=-1).astype(k_ref.dtype)
        v_ref[h] = jnp.concatenate([kv[:, MLA_NOPE:].astype(v_ref.dtype), ones], axis=-1)


def _normalise(oe):
    return oe[:, :MLA_V] / oe[:, MLA_V:]


def _mla_attn_kernel(q_ref, k_ref, v_ref, km_ref, vm_ref, o_ref, *, tq, batch):
    seq = q_ref.shape[0]
    nq = seq // tq
    b = pl.program_id(1)

    @pl.when(b < batch)
    def _():
        km = km_ref[...]
        vm = vm_ref[...]
        mcol = lax.broadcasted_iota(jnp.int32, (tq, BLOCK), 1)
        meta_bias = jnp.where(mcol >= META_LO, 0.0, NEG_INF)
        row = lax.broadcasted_iota(jnp.int32, (tq, tq), 0)
        col = lax.broadcasted_iota(jnp.int32, (tq, tq), 1)
        causal_bias = jnp.where(row >= col, 0.0, NEG_INF)
        for qi in range(nq):
            lo, hi = qi * tq, (qi + 1) * tq
            q = q_ref[lo:hi, :]
            s0 = _dot_nt(q, km) + meta_bias
            sd = _dot_nt(q, k_ref[lo:hi, :]) + causal_bias
            m = jnp.maximum(jnp.max(s0, axis=-1, keepdims=True), jnp.max(sd, axis=-1, keepdims=True))
            if qi > 0:
                sf = _dot_nt(q, k_ref[0:lo, :])
                m = jnp.maximum(m, jnp.max(sf, axis=-1, keepdims=True))
            oe = _dot(jnp.exp2(s0 - m).astype(BF16), vm)
            oe = oe + _dot(jnp.exp2(sd - m).astype(BF16), v_ref[lo:hi, :])
            if qi > 0:
                oe = oe + _dot(jnp.exp2(sf - m).astype(BF16), v_ref[0:lo, :])
            o_ref[lo:hi, :] = _normalise(oe).astype(o_ref.dtype)

    @pl.when(b == batch)
    def _():
        row = lax.broadcasted_iota(jnp.int32, (BLOCK, BLOCK), 0)
        col = lax.broadcasted_iota(jnp.int32, (BLOCK, BLOCK), 1)
        s = _dot_nt(q_ref[0:BLOCK, :], km_ref[...])
        s = jnp.where((col >= META_LO) & (row >= col), s, NEG_INF)
        m = jnp.max(s, axis=-1, keepdims=True)
        oe = _dot(jnp.exp2(s - m).astype(BF16), vm_ref[...])
        o_ref[0:BLOCK, :] = _normalise(oe).astype(o_ref.dtype)


def _mla_branch(proj, lay, cq_g, ckv_g, w_uq, w_ukv, qn_g, kn_g, cos_t, sin_t, batch, seq):
    m = proj.shape[0]
    qr = cq_g.shape[0]
    kvr = ckv_g.shape[0]
    heads = w_uq.shape[1] // MLA_QK
    tm = _pick(m, 640, 128)
    scale = MLA_QK ** -0.5 * LOG2E

    w_uq_h = jnp.pad(w_uq.reshape(qr, heads, MLA_QK), ((0, 0), (0, 0), (0, MLA_QK_PAD - MLA_QK)))
    w_uq_h = jnp.transpose(w_uq_h, (1, 0, 2)).astype(BF16)
    w_ukv_b = w_ukv.astype(BF16)
    qg = (jnp.pad(qn_g, (0, MLA_QK_PAD - MLA_QK)) * scale).reshape(1, MLA_QK_PAD)
    kg = jnp.pad(kn_g, (0, MLA_QK_PAD - MLA_QK)).reshape(1, MLA_QK_PAD)
    v_ext = MLA_V + MLA_V

    q = pl.pallas_call(
        functools.partial(_mla_q_kernel, heads=heads),
        grid=(m // tm,),
        in_specs=[pl.BlockSpec((tm, qr), lambda i: (i, lay["c_q"] // qr)),
                  pl.BlockSpec((1, qr), lambda i: (0, 0)),
                  pl.BlockSpec((heads, qr, MLA_QK_PAD), lambda i: (0, 0, 0)),
                  pl.BlockSpec((1, MLA_QK_PAD), lambda i: (0, 0)),
                  pl.BlockSpec((tm, LANE), lambda i: (i, 0)),
                  pl.BlockSpec((tm, LANE), lambda i: (i, 0))],
        out_specs=pl.BlockSpec((heads, tm, MLA_QK_PAD), lambda i: (0, i, 0)),
        out_shape=jax.ShapeDtypeStruct((heads, m, MLA_QK_PAD), BF16),
        compiler_params=_params(("arbitrary",)),
        name="mla_q",
    )(proj, cq_g.reshape(1, qr), w_uq_h, qg, cos_t, sin_t)

    k, v = pl.pallas_call(
        functools.partial(_mla_kv_kernel, heads=heads),
        grid=(m // tm,),
        in_specs=[pl.BlockSpec((tm, kvr), lambda i: (i, lay["c_kv"] // kvr)),
                  pl.BlockSpec((1, kvr), lambda i: (0, 0)),
                  pl.BlockSpec((kvr, heads * (MLA_NOPE + MLA_V)), lambda i: (0, 0)),
                  pl.BlockSpec((tm, LANE), lambda i: (i, lay["k_pe"] // LANE)),
                  pl.BlockSpec((1, MLA_QK_PAD), lambda i: (0, 0)),
                  pl.BlockSpec((tm, LANE), lambda i: (i, 0)),
                  pl.BlockSpec((tm, LANE), lambda i: (i, 0))],
        out_specs=[pl.BlockSpec((heads, tm, MLA_QK_PAD), lambda i: (0, i, 0)),
                   pl.BlockSpec((heads, tm, v_ext), lambda i: (0, i, 0))],
        out_shape=[jax.ShapeDtypeStruct((heads, m, MLA_QK_PAD), BF16),
                   jax.ShapeDtypeStruct((heads, m, v_ext), BF16)],
        compiler_params=_params(("arbitrary",)),
        name="mla_kv",
    )(proj, ckv_g.reshape(1, kvr), w_ukv_b, proj, kg, cos_t, sin_t)

    meta_blk = m // BLOCK - 1
    tq = _pick(seq, 512, 128)
    a = pl.pallas_call(
        functools.partial(_mla_attn_kernel, tq=tq, batch=batch),
        grid=(heads, batch + 1),
        in_specs=[pl.BlockSpec((None, seq, MLA_QK_PAD), lambda h, b: (h, b, 0)),
                  pl.BlockSpec((None, seq, MLA_QK_PAD), lambda h, b: (h, b, 0)),
                  pl.BlockSpec((None, seq, v_ext), lambda h, b: (h, b, 0)),
                  pl.BlockSpec((None, BLOCK, MLA_QK_PAD), lambda h, b: (h, meta_blk, 0)),
                  pl.BlockSpec((None, BLOCK, v_ext), lambda h, b: (h, meta_blk, 0))],
        out_specs=pl.BlockSpec((seq, MLA_V), lambda h, b: (b, h)),
        out_shape=jax.ShapeDtypeStruct((m, heads * MLA_V), BF16),
        compiler_params=_params(("arbitrary", "arbitrary")),
        name="mla_attn",
    )(q, k, v, k, v)
    return a


def _t5_bucket(rel, buckets):
    n = jnp.maximum(rel, 0)
    max_exact = buckets // 2
    nf = jnp.maximum(n, 1).astype(F32)
    large = max_exact + (jnp.log(nf / max_exact) / math.log(REL_MAX_DIST / max_exact)
                         * (buckets - max_exact)).astype(jnp.int32)
    large = jnp.minimum(large, buckets - 1)
    return jnp.where(n < max_exact, n, large)


def _bias_kernel(rb_ref, bucket_ref, o_ref, *, buckets):
    pr = pl.program_id(0)
    bucket = bucket_ref[...]
    qi = lax.broadcasted_iota(jnp.int32, bucket.shape, 0)
    sj = lax.broadcasted_iota(jnp.int32, bucket.shape, 1)
    band = (sj - qi >= 1) & (sj - qi <= WINDOW)
    halves = []
    for t in range(2):
        acc = jnp.zeros(bucket.shape, F32)
        for b in range(buckets):
            acc = jnp.where(bucket == b, rb_ref[b, 2 * pr + t], acc)
        halves.append(jnp.where(band, acc * LOG2E, NEG_INF))
    o_ref[...] = jnp.concatenate(halves, axis=1)


def _bias_table(rel_bias):
    buckets, hq = rel_bias.shape
    qi = jnp.arange(BLOCK)[:, None]
    sj = jnp.arange(2 * BLOCK)[None, :]
    bucket = _t5_bucket(qi + BLOCK - sj, buckets).astype(jnp.int32)
    return pl.pallas_call(
        functools.partial(_bias_kernel, buckets=buckets),
        grid=(hq // 2,),
        in_specs=[pl.BlockSpec(memory_space=pltpu.SMEM),
                  pl.BlockSpec((BLOCK, 2 * BLOCK), lambda h: (0, 0))],
        out_specs=pl.BlockSpec((None, BLOCK, 4 * BLOCK), lambda h: (h, 0, 0)),
        out_shape=jax.ShapeDtypeStruct((hq // 2, BLOCK, 4 * BLOCK), F32),
        compiler_params=_params(("arbitrary",)),
        name="rel_bias_table",
    )(rel_bias, bucket)


def _seg_rms(x, seg_ref, segt_ref, gain):
    ss = _dot((x * x).astype(BF16), seg_ref[...])
    r = lax.rsqrt(ss * (1.0 / SWA_HEAD_DIM) + EPS)
    return x * _dot(r.astype(BF16), segt_ref[...]) * gain


def _swa_kernel(prev_ref, plo_ref, clo_ref, q_ref, kc_ref, kp_ref, vc_ref, vp_ref, bias_ref, sink_ref,
                qg_ref, kg_ref, segq_ref, segqt_ref, segk_ref, segkt_ref, ones_ref, o_ref, *, hq, hkv):
    del prev_ref
    t = pl.program_id(0)
    plo = plo_ref[t]
    clo = clo_ref[t]
    half = LANE // 2
    pairs_per_group = hq // hkv // 2

    sj = lax.broadcasted_iota(jnp.int32, (BLOCK, 4 * BLOCK), 1) % (2 * BLOCK)
    in_prev = sj < BLOCK
    krow = jnp.where(in_prev, sj, sj - BLOCK)
    key_bias = jnp.where(krow >= jnp.where(in_prev, plo, clo), 0.0, NEG_INF)

    qn = _seg_rms(q_ref[...].astype(F32), segq_ref, segqt_ref, qg_ref[...]).astype(BF16)
    kf = jnp.concatenate([kp_ref[...], kc_ref[...]], axis=0).astype(F32)
    kn = _seg_rms(kf, segk_ref, segkt_ref, kg_ref[...])
    vf = jnp.concatenate([vp_ref[...], vc_ref[...]], axis=0).astype(F32)

    lane2 = lax.broadcasted_iota(jnp.int32, (2 * BLOCK, LANE), 1)
    lane1 = lax.broadcasted_iota(jnp.int32, (BLOCK, LANE), 1)

    def block_diag(col, odd):
        other = pltpu.roll(col, half, 1)
        on_lo, on_hi = (other, col) if odd else (col, other)
        return jnp.concatenate([jnp.where(lane2 < half, on_lo, 0.0), jnp.where(lane2 < half, 0.0, on_hi)],
                               axis=0).astype(BF16)

    outs = []
    for g in range(hkv):
        c = g // 2
        kbd = block_diag(kn[:, c * LANE:(c + 1) * LANE], g % 2)
        vext = jnp.concatenate([block_diag(vf[:, c * LANE:(c + 1) * LANE], g % 2), ones_ref[...]], axis=1)
        for jj in range(pairs_per_group):
            pr = g * pairs_per_group + jj
            s = _dot_nt(qn[:, pr * LANE:(pr + 1) * LANE], kbd) + bias_ref[pr] + key_bias
            sa, sb = s[:, :2 * BLOCK], s[:, 2 * BLOCK:]
            ma = jnp.maximum(jnp.max(sa, axis=-1, keepdims=True), sink_ref[2 * pr])
            mb = jnp.maximum(jnp.max(sb, axis=-1, keepdims=True), sink_ref[2 * pr + 1])
            e = jnp.concatenate([jnp.exp2(sa - ma), jnp.exp2(sb - mb)], axis=1).astype(BF16)
            oe = _dot(e, vext)
            sink_term = jnp.where(lane1 < half, jnp.exp2(sink_ref[2 * pr] - ma),
                                  jnp.exp2(sink_ref[2 * pr + 1] - mb))
            outs.append(oe[:, :LANE] / (oe[:, LANE:] + sink_term))
    o_ref[...] = jnp.concatenate(outs, axis=-1).astype(o_ref.dtype)


def _swa_branch(proj, lay, bias_tab, sinks, qg, kg, batch, seq):
    m = proj.shape[0]
    hq = sinks.shape[0]
    hd = SWA_HEAD_DIM
    qw = hq * hd
    kw = lay["v_s"] - lay["k_s"]
    hkv = kw // hd
    nblk = m // BLOCK
    per = seq // BLOCK
    t = jnp.arange(nblk, dtype=jnp.int32)
    is_meta = t == nblk - 1
    first = (t % per) == 0
    prev = jnp.where(is_meta | first, nblk - 1, t - 1).astype(jnp.int32)
    plo = jnp.where(is_meta, BLOCK, jnp.where(first, META_LO, 0)).astype(jnp.int32)
    clo = jnp.where(is_meta, META_LO, 0).astype(jnp.int32)
    assert (hq // hkv) % 2 == 0 and kw % LANE == 0 and hq <= LANE

    def seg(width):
        return (jnp.arange(width)[:, None] // hd == jnp.arange(LANE)[None, :]).astype(BF16)

    segq, segk = seg(qw), seg(kw)
    qg_t = (jnp.tile(qg, hq) * (hd ** -0.5 * LOG2E)).reshape(1, qw)
    kg_t = jnp.tile(kg, hkv).reshape(1, kw)
    ones_bd = ((jnp.arange(4 * BLOCK)[:, None] < 2 * BLOCK)
               == (jnp.arange(LANE)[None, :] < LANE // 2)).astype(BF16)

    def const(shape):
        return pl.BlockSpec(shape, lambda i, pr, pl_, cl: (0,) * len(shape))

    grid_spec = pltpu.PrefetchScalarGridSpec(
        num_scalar_prefetch=3,
        grid=(nblk,),
        in_specs=[pl.BlockSpec((BLOCK, qw), lambda i, pr, pl_, cl: (i, lay["q_s"] // qw)),
                  pl.BlockSpec((BLOCK, kw), lambda i, pr, pl_, cl: (i, lay["k_s"] // kw)),
                  pl.BlockSpec((BLOCK, kw), lambda i, pr, pl_, cl: (pr[i], lay["k_s"] // kw)),
                  pl.BlockSpec((BLOCK, kw), lambda i, pr, pl_, cl: (i, lay["v_s"] // kw)),
                  pl.BlockSpec((BLOCK, kw), lambda i, pr, pl_, cl: (pr[i], lay["v_s"] // kw)),
                  const((hq // 2, BLOCK, 4 * BLOCK)),
                  pl.BlockSpec(memory_space=pltpu.SMEM),
                  const((1, qw)), const((1, kw)),
                  const((qw, LANE)), const((LANE, qw)), const((kw, LANE)), const((LANE, kw)),
                  const((4 * BLOCK, LANE))],
        out_specs=pl.BlockSpec((BLOCK, qw), lambda i, pr, pl_, cl: (i, 0)),
    )
    return pl.pallas_call(
        functools.partial(_swa_kernel, hq=hq, hkv=hkv),
        grid_spec=grid_spec,
        out_shape=jax.ShapeDtypeStruct((m, qw), BF16),
        compiler_params=_params(("arbitrary",)),
        name="swa_attn",
    )(prev, plo, clo, proj, proj, proj, proj, proj, bias_tab, sinks * LOG2E, qg_t, kg_t,
      segq, segq.T, segk, segk.T, ones_bd)


def _merge_kernel(a_ref, b_ref, wa_ref, wb_ref, ga_ref, gb_ref, o_ref, wab_ref, wbb_ref):
    @pl.when(pl.program_id(1) == 0)
    def _():
        _cast_weight(wa_ref, wab_ref)
        _cast_weight(wb_ref, wbb_ref)

    ya = _dot(a_ref[...], wab_ref[...])
    yb = _dot(b_ref[...], wbb_ref[...])
    ga = _sigmoid(ga_ref[...].astype(F32))
    gb = _sigmoid(gb_ref[...].astype(F32))
    o_ref[...] = (ga * ya + gb * yb).astype(o_ref.dtype)


def _merge(a, b, w_a, w_b, layer, proj, lay, tn):
    m, ka = a.shape
    kb = b.shape[1]
    d = w_a.shape[-1]
    tm = _pick(m, 640, 128)
    ga0 = lay["g_a"] // tn
    gb0 = lay["g_b"] // tn
    return pl.pallas_call(
        _merge_kernel,
        grid=(d // tn, m // tm),
        in_specs=[pl.BlockSpec((tm, ka), lambda j, i: (i, 0)),
                  pl.BlockSpec((tm, kb), lambda j, i: (i, 0)),
                  pl.BlockSpec((None, ka, tn), lambda j, i: (layer, 0, j)),
                  pl.BlockSpec((None, kb, tn), lambda j, i: (layer, 0, j)),
                  pl.BlockSpec((tm, tn), lambda j, i: (i, ga0 + j)),
                  pl.BlockSpec((tm, tn), lambda j, i: (i, gb0 + j))],
        out_specs=pl.BlockSpec((tm, tn), lambda j, i: (i, j)),
        out_shape=jax.ShapeDtypeStruct((m, d), BF16),
        scratch_shapes=[pltpu.VMEM((ka, tn), BF16), pltpu.VMEM((kb, tn), BF16)],
        compiler_params=_params(("arbitrary", "arbitrary")),
        name="branch_merge",
    )(a, b, w_a, w_b, proj, proj)


def _ffn_up_kernel(x_ref, w1_ref, w3_ref, o_ref, w1b_ref, w3b_ref):
    @pl.when(pl.program_id(1) == 0)
    def _():
        _cast_weight(w1_ref, w1b_ref)
        _cast_weight(w3_ref, w3b_ref)

    x = x_ref[...]
    u = _dot(x, w1b_ref[...])
    g = _dot(x, w3b_ref[...])
    o_ref[...] = (u * _sigmoid(u) * g).astype(o_ref.dtype)


def _ffn_up(hn, w1, w3, idx):
    m, d = hn.shape
    ff = w1.shape[-1]
    tn = _pick(ff, 256, 128)
    tm = _pick(m, 640, 128)
    return pl.pallas_call(
        _ffn_up_kernel,
        grid=(ff // tn, m // tm),
        in_specs=[pl.BlockSpec((tm, d), lambda j, i: (i, 0)),
                  pl.BlockSpec((None, d, tn), lambda j, i: (idx, 0, j)),
                  pl.BlockSpec((None, d, tn), lambda j, i: (idx, 0, j))],
        out_specs=pl.BlockSpec((tm, tn), lambda j, i: (i, j)),
        out_shape=jax.ShapeDtypeStruct((m, ff), BF16),
        scratch_shapes=[pltpu.VMEM((d, tn), BF16), pltpu.VMEM((d, tn), BF16)],
        compiler_params=_params(("arbitrary", "arbitrary")),
        name="ffn_up",
    )(hn, w1, w3)


def _dense_ffn(h_res, gain, w1, w3, w2, idx):
    hn = _rms_norm(h_res, gain)
    t = _ffn_up(hn, w1, w3, idx)
    ff = w1.shape[-1]
    nk = 1
    for cand in (1, 2, 4):
        if ff % cand == 0 and (ff // cand) % LANE == 0 and ff // cand <= 6144:
            nk = cand
            break
    kc = ff // nk
    out = h_res
    for kblk in range(nk):
        out = _mm_res(t, w2, (idx,), kblk, kc, out, "ffn_down")
    return out


def _router_kernel(x_ref, g_ref, w_ref, o_ref, *, n_exp):
    x = x_ref[...]
    ms = jnp.mean(x * x, axis=-1, keepdims=True)
    xn = (x * lax.rsqrt(ms + EPS) * g_ref[...]).astype(BF16)
    logits = _dot(xn, w_ref[...].astype(BF16))
    lane = lax.broadcasted_iota(jnp.int32, logits.shape, 1)
    logits = jnp.where(lane < n_exp, logits, -jnp.inf)
    lane_f = lane.astype(F32)
    m1 = jnp.max(logits, axis=-1, keepdims=True)
    i1 = jnp.min(jnp.where(logits == m1, lane_f, float(LANE)), axis=-1, keepdims=True)
    rest = jnp.where(lane_f == i1, -jnp.inf, logits)
    m2 = jnp.max(rest, axis=-1, keepdims=True)
    i2 = jnp.min(jnp.where(rest == m2, lane_f, float(LANE)), axis=-1, keepdims=True)
    e2 = jnp.exp(m2 - m1)
    w1 = 1.0 / (1.0 + e2)
    w2 = e2 / (1.0 + e2)
    out = jnp.where(lane == 0, i1, jnp.where(lane == 1, i2,
                                             jnp.where(lane == 2, w1, jnp.where(lane == 3, w2, 0.0))))
    o_ref[...] = out


def _router(h_res, gain, router_w):
    m, d = h_res.shape
    n_exp = router_w.shape[-1]
    tm = _pick(m, 640, 128)
    wpad = jnp.pad(router_w, ((0, 0), (0, LANE - n_exp)))
    return pl.pallas_call(
        functools.partial(_router_kernel, n_exp=n_exp),
        grid=(m // tm,),
        in_specs=[pl.BlockSpec((tm, d), lambda i: (i, 0)),
                  pl.BlockSpec((1, d), lambda i: (0, 0)),
                  pl.BlockSpec((d, LANE), lambda i: (0, 0))],
        out_specs=pl.BlockSpec((tm, LANE), lambda i: (i, 0)),
        out_shape=jax.ShapeDtypeStruct((m, LANE), F32),
        compiler_params=_params(("arbitrary",)),
        name="moe_router",
    )(h_res, gain.reshape(1, d), wpad)


def _row_copy(src_hbm, buf, sem, slot, src_row, dst_row):
    return pltpu.make_async_copy(src_hbm.at[pl.ds(src_row, 1)], buf.at[slot, pl.ds(dst_row, 1)],
                                 sem.at[slot])


def _gather_norm_kernel(tok_ref, h_hbm, g_ref, o_ref, buf, sem, *, rows):
    s = pl.program_id(0)
    n = pl.num_programs(0)

    def start(step, slot):
        def body(r, c):
            _row_copy(h_hbm, buf, sem, slot, tok_ref[step * rows + r], r).start()
            return c
        lax.fori_loop(0, rows, body, 0)

    @pl.when(s == 0)
    def _():
        start(0, 0)

    @pl.when(s + 1 < n)
    def _():
        start(s + 1, (s + 1) % 2)

    slot = s % 2

    def wbody(r, c):
        _row_copy(h_hbm, buf, sem, slot, 0, r).wait()
        return c
    lax.fori_loop(0, rows, wbody, 0)

    x = buf[slot]
    ms = jnp.mean(x * x, axis=-1, keepdims=True)
    o_ref[...] = (x * lax.rsqrt(ms + EPS) * g_ref[...]).astype(o_ref.dtype)


def _gather_norm(h_res, gain, row_tok, rows):
    d = h_res.shape[1]
    rp = row_tok.shape[0]
    grid_spec = pltpu.PrefetchScalarGridSpec(
        num_scalar_prefetch=1,
        grid=(rp // rows,),
        in_specs=[pl.BlockSpec(memory_space=pl.ANY),
                  pl.BlockSpec((1, d), lambda i, tok: (0, 0))],
        out_specs=pl.BlockSpec((rows, d), lambda i, tok: (i, 0)),
        scratch_shapes=[pltpu.VMEM((2, rows, d), F32), pltpu.SemaphoreType.DMA((2,))],
    )
    return pl.pallas_call(
        functools.partial(_gather_norm_kernel, rows=rows),
        grid_spec=grid_spec,
        out_shape=jax.ShapeDtypeStruct((rp, d), BF16),
        compiler_params=_params(("arbitrary",)),
        name="moe_gather_norm",
    )(row_tok, h_res, gain.reshape(1, d))


def _moe_up_kernel(be_ref, first_ref, valid_ref, xb_ref, x_ref, w1_ref, w3_ref, o_ref, w1b_ref, w3b_ref):
    i = pl.program_id(1)

    @pl.when(first_ref[i] == 1)
    def _():
        _cast_weight(w1_ref, w1b_ref)
        _cast_weight(w3_ref, w3b_ref)

    @pl.when(valid_ref[i] == 1)
    def _():
        x = x_ref[...]
        u = _dot(x, w1b_ref[...])
        g = _dot(x, w3b_ref[...])
        o_ref[...] = (u * _sigmoid(u) * g).astype(o_ref.dtype)

    @pl.when(valid_ref[i] == 0)
    def _():
        o_ref[...] = jnp.zeros(o_ref.shape, o_ref.dtype)


def _moe_down_kernel(be_ref, first_ref, valid_ref, xb_ref, x_ref, w_ref, o_ref, wb_ref):
    i = pl.program_id(1)

    @pl.when(first_ref[i] == 1)
    def _():
        _cast_weight(w_ref, wb_ref)

    @pl.when(valid_ref[i] == 1)
    def _():
        o_ref[...] = _dot(x_ref[...], wb_ref[...])

    @pl.when(valid_ref[i] == 0)
    def _():
        o_ref[...] = jnp.zeros(o_ref.shape, o_ref.dtype)


def _combine_kernel(pos_ref, h_ref, rt_ref, y_hbm, o_ref, buf, sem, *, rows):
    s = pl.program_id(0)
    n = pl.num_programs(0)

    def start(step, slot):
        def body(r, c):
            _row_copy(y_hbm, buf, sem, slot, pos_ref[step * 2 * rows + r], r).start()
            return c
        lax.fori_loop(0, 2 * rows, body, 0)

    @pl.when(s == 0)
    def _():
        start(0, 0)

    @pl.when(s + 1 < n)
    def _():
        start(s + 1, (s + 1) % 2)

    slot = s % 2

    def wbody(r, c):
        _row_copy(y_hbm, buf, sem, slot, 0, r).wait()
        return c
    lax.fori_loop(0, 2 * rows, wbody, 0)

    rt = rt_ref[...]
    y0 = buf[slot, 0:rows, :]
    y1 = buf[slot, rows:2 * rows, :]
    o_ref[...] = h_ref[...] + rt[:, 2:3] * y0 + rt[:, 3:4] * y1


def _moe_ffn(h_res, gain, router_w, w1, w3, w2, idx, n_out):
    m, d = h_res.shape
    n_exp = router_w.shape[-1]
    ff = w1.shape[-1]
    rt = _router(h_res, gain, router_w)

    tme = 256 if (TOP_K * m) % 256 == 0 else 128
    r_tot = TOP_K * m
    nb = r_tot // tme + n_exp
    rp = nb * tme
    e = rt[:, :TOP_K].astype(jnp.int32).reshape(-1)
    onehot = (e[:, None] == jnp.arange(n_exp, dtype=jnp.int32)[None, :]).astype(jnp.int32)
    csum = jnp.cumsum(onehot, axis=0)
    rank = jnp.sum(onehot * csum, axis=1) - 1
    counts = csum[-1]
    padded = (counts + tme - 1) // tme * tme
    gend = jnp.cumsum(padded)
    goff = gend - padded
    pos = (jnp.sum(onehot * goff[None, :], axis=1) + rank).astype(jnp.int32)
    row_tok = jnp.zeros((rp,), jnp.int32).at[pos].set(jnp.arange(r_tot, dtype=jnp.int32) // TOP_K)
    used = gend[-1] // tme
    blk = jnp.arange(nb, dtype=jnp.int32)
    be = jnp.minimum(jnp.sum((blk[:, None] * tme >= gend[None, :]).astype(jnp.int32), axis=1),
                     n_exp - 1).astype(jnp.int32)
    valid = (blk < used).astype(jnp.int32)
    first = jnp.concatenate([jnp.ones((1,), jnp.int32), (be[1:] != be[:-1]).astype(jnp.int32)])
    xb = jnp.minimum(blk, used - 1).astype(jnp.int32)

    xs = _gather_norm(h_res, gain, row_tok, 128)

    tn_up = _pick(ff, 512, 128)
    up_spec = pltpu.PrefetchScalarGridSpec(
        num_scalar_prefetch=4,
        grid=(ff // tn_up, nb),
        in_specs=[pl.BlockSpec((tme, d), lambda j, i, be_, f_, v_, xb_: (xb_[i], 0)),
                  pl.BlockSpec((None, None, d, tn_up), lambda j, i, be_, f_, v_, xb_: (idx, be_[i], 0, j)),
                  pl.BlockSpec((None, None, d, tn_up), lambda j, i, be_, f_, v_, xb_: (idx, be_[i], 0, j))],
        out_specs=pl.BlockSpec((tme, tn_up), lambda j, i, be_, f_, v_, xb_: (i, j)),
        scratch_shapes=[pltpu.VMEM((d, tn_up), BF16), pltpu.VMEM((d, tn_up), BF16)],
    )
    t = pl.pallas_call(
        _moe_up_kernel,
        grid_spec=up_spec,
        out_shape=jax.ShapeDtypeStruct((rp, ff), BF16),
        compiler_params=_params(("arbitrary", "arbitrary"), VMEM_LIMIT_BIG),
        name="moe_up",
    )(be, first, valid, xb, xs, w1, w3)

    tn_dn = _pick(d, 512, 128)
    dn_spec = pltpu.PrefetchScalarGridSpec(
        num_scalar_prefetch=4,
        grid=(d // tn_dn, nb),
        in_specs=[pl.BlockSpec((tme, ff), lambda j, i, be_, f_, v_, xb_: (xb_[i], 0)),
                  pl.BlockSpec((None, None, ff, tn_dn), lambda j, i, be_, f_, v_, xb_: (idx, be_[i], 0, j))],
        out_specs=pl.BlockSpec((tme, tn_dn), lambda j, i, be_, f_, v_, xb_: (i, j)),
        scratch_shapes=[pltpu.VMEM((ff, tn_dn), BF16)],
    )
    y = pl.pallas_call(
        _moe_down_kernel,
        grid_spec=dn_spec,
        out_shape=jax.ShapeDtypeStruct((rp, d), F32),
        compiler_params=_params(("arbitrary", "arbitrary"), VMEM_LIMIT_BIG),
        name="moe_down",
    )(be, first, valid, xb, t, w2)

    rows = 128
    cmb_spec = pltpu.PrefetchScalarGridSpec(
        num_scalar_prefetch=1,
        grid=(n_out // rows,),
        in_specs=[pl.BlockSpec((rows, d), lambda i, p: (i, 0)),
                  pl.BlockSpec((rows, LANE), lambda i, p: (i, 0)),
                  pl.BlockSpec(memory_space=pl.ANY)],
        out_specs=pl.BlockSpec((rows, d), lambda i, p: (i, 0)),
        scratch_shapes=[pltpu.VMEM((2, 2 * rows, d), F32), pltpu.SemaphoreType.DMA((2,))],
    )
    pos_blk = jnp.transpose(pos.reshape(m // rows, rows, TOP_K), (0, 2, 1)).reshape(-1)
    return pl.pallas_call(
        functools.partial(_combine_kernel, rows=rows),
        grid_spec=cmb_spec,
        out_shape=jax.ShapeDtypeStruct((n_out, d), F32),
        compiler_params=_params(("arbitrary",)),
        name="moe_combine",
    )(pos_blk, h_res, rt, y)


def _in_layout(d, qr, kvr, qw, kw):
    head = qr + kvr
    rest = qw + 2 * kw + 2 * d
    tn = max(t for t in (512, 256, 128) if head % t == 0 and rest % t == 0)
    lay = {"c_q": 0, "c_kv": qr, "k_pe": head}
    cur = head + tn
    for name, width in (("q_s", qw), ("k_s", kw), ("v_s", kw), ("g_a", d), ("g_b", d)):
        lay[name] = cur
        cur += width
    lay["total"] = cur
    assert lay["c_kv"] % kvr == 0 and lay["q_s"] % qw == 0 and lay["k_s"] % kw == 0 and lay["v_s"] % kw == 0
    assert lay["g_a"] % tn == 0 and lay["g_b"] % tn == 0 and d % tn == 0
    return lay, tn


def kernel(x, meta_tokens, rel_bias, attn_norm, w_in, mla_cq_norm, mla_ckv_norm, mla_w_uq, mla_w_ukv,
           mla_q_norm, mla_k_norm, swa_q_norm, swa_k_norm, swa_sinks, w_branch_mla, w_branch_swa, w_out,
           ffn_norm, dense_w1, dense_w3, dense_w2, moe_router, moe_w1, moe_w3, moe_w2):
    batch, seq, d = x.shape
    depth = w_in.shape[0]
    qr = mla_cq_norm.shape[1]
    kvr = mla_ckv_norm.shape[1]
    hq = swa_sinks.shape[1]
    qw = hq * SWA_HEAD_DIM
    kw = (w_in.shape[2] - qr - kvr - MLA_ROPE - qw - 2 * d) // 2
    assert seq % BLOCK == 0 and meta_tokens.shape[0] == N_META
    n_tok = batch * seq
    m = n_tok + BLOCK

    lay, tn_in = _in_layout(d, qr, kvr, qw, kw)

    h_res = jnp.concatenate([x.reshape(n_tok, d), jnp.zeros((META_LO, d), x.dtype),
                             meta_tokens.astype(x.dtype)], axis=0)

    pos = jnp.concatenate([jnp.tile(N_META + jnp.arange(seq), batch), jnp.arange(BLOCK) - META_LO])
    half = MLA_ROPE // 2
    inv_freq = ROPE_THETA ** (-jnp.arange(half, dtype=F32) / half)
    ang = pos.astype(F32)[:, None] * inv_freq[None, :]
    cos, sin = jnp.cos(ang), jnp.sin(ang)
    cos_t = jnp.concatenate([cos, cos, jnp.ones((m, LANE - MLA_ROPE), F32)], axis=1)
    sin_t = jnp.concatenate([-sin, sin, jnp.zeros((m, LANE - MLA_ROPE), F32)], axis=1)

    bias_tab = _bias_table(rel_bias)

    for i in range(depth):
        hn = _rms_norm(h_res, attn_norm[i])
        proj = _in_proj(hn, w_in, i, lay, tn_in)
        a = _mla_branch(proj, lay, mla_cq_norm[i], mla_ckv_norm[i], mla_w_uq[i], mla_w_ukv[i],
                        mla_q_norm[i], mla_k_norm[i], cos_t, sin_t, batch, seq)
        b = _swa_branch(proj, lay, bias_tab, swa_sinks[i], swa_q_norm[i], swa_k_norm[i], batch, seq)
        merged = _merge(a, b, w_branch_mla, w_branch_swa, i, proj, lay, tn_in)
        h_res = _mm_res(merged, w_out, (i,), 0, d, h_res, "out_proj")
        last = i == depth - 1
        if i % 2 == 0:
            h_res = _dense_ffn(h_res, ffn_norm[i], dense_w1, dense_w3, dense_w2, i // 2)
            if last:
                h_res = h_res[:n_tok]
        else:
            h_res = _moe_ffn(h_res, ffn_norm[i], moe_router[i // 2], moe_w1, moe_w3, moe_w2, i // 2,
                             n_tok if last else m)
    return h_res.reshape(batch, seq, d)
```

```python
import functools
import math

import jax
import jax.numpy as jnp
from jax import lax
from jax.experimental import pallas as pl
from jax.experimental.pallas import tpu as pltpu

F32 = jnp.float32
BF16 = jnp.bfloat16

N_META = 16
BLOCK = 128
WINDOW = 128
MLA_NOPE = 128
MLA_ROPE = 64
MLA_V = 128
MLA_QK = MLA_NOPE + MLA_ROPE
MLA_QK_PAD = 256
ROPE_THETA = 10000.0
SWA_HEAD_DIM = 64
REL_MAX_DIST = 128
TOP_K = 2
EPS = 1e-6
NEG_INF = -1e30
LOG2E = 1.4426950408889634
LANE = 128
META_LO = BLOCK - N_META

VMEM_LIMIT_BIG = 56 * 1024 * 1024
VMEM_LIMIT_MID = 44 * 1024 * 1024


def _params(sem, vmem=VMEM_LIMIT_MID):
    return pltpu.CompilerParams(dimension_semantics=sem, vmem_limit_bytes=vmem)


def _pick(n, target, mult):
    best = None
    d = mult
    while d <= min(n, target):
        if n % d == 0:
            best = d
        d += mult
    return best if best is not None else n


def _round_up(a, b):
    return (a + b - 1) // b * b


def _cast_weight(w_ref, wb_ref):
    k = w_ref.shape[0]
    ch = 512 if k % 512 == 0 else (256 if k % 256 == 0 else 128)
    if k % ch != 0:
        wb_ref[...] = w_ref[...].astype(BF16)
        return

    def body(c, carry):
        r = pl.multiple_of(c * ch, ch)
        wb_ref[pl.ds(r, ch), :] = w_ref[pl.ds(r, ch), :].astype(BF16)
        return carry

    lax.fori_loop(0, k // ch, body, 0)


def _sigmoid(x):
    return 1.0 / (1.0 + jnp.exp(-x))


def _dot(a, b):
    return jnp.dot(a, b, preferred_element_type=F32)


def _dot_nt(a, b):
    return lax.dot_general(a, b, (((1,), (1,)), ((), ())), preferred_element_type=F32)


def _rms_kernel(x_ref, g_ref, o_ref):
    x = x_ref[...]
    ms = jnp.mean(x * x, axis=-1, keepdims=True)
    o_ref[...] = (x * lax.rsqrt(ms + EPS) * g_ref[...]).astype(o_ref.dtype)


def _rms_norm(h, gain):
    m, d = h.shape
    tm = _pick(m, 640, 128)
    return pl.pallas_call(
        _rms_kernel,
        grid=(m // tm,),
        in_specs=[pl.BlockSpec((tm, d), lambda i: (i, 0)),
                  pl.BlockSpec((1, d), lambda i: (0, 0))],
        out_specs=pl.BlockSpec((tm, d), lambda i: (i, 0)),
        out_shape=jax.ShapeDtypeStruct((m, d), BF16),
        compiler_params=_params(("arbitrary",)),
        name="rms_norm",
    )(h, gain.reshape(1, d))


def _in_proj_kernel(x_ref, a_ref, b_ref, o_ref, wb_ref, *, nh):
    j = pl.program_id(0)
    tn, k = a_ref.shape
    half = LANE // 2

    @pl.when(pl.program_id(1) == 0)
    def _():
        @pl.when(j < nh)
        def _():
            wb_ref[...] = a_ref[...].astype(BF16)

        @pl.when(j == nh)
        def _():
            wb_ref[0:half, :] = a_ref[0:half, :].astype(BF16)
            wb_ref[half:tn, :] = jnp.zeros((tn - half, k), BF16)

        @pl.when(j > nh)
        def _():
            wb_ref[0:tn - half, :] = a_ref[half:tn, :].astype(BF16)
            wb_ref[tn - half:tn, :] = b_ref[0:half, :].astype(BF16)

    o_ref[...] = _dot_nt(x_ref[...], wb_ref[...]).astype(o_ref.dtype)


def _in_proj(hn, w_in_t, layer, lay, tn):
    m, k = hn.shape
    nh = lay["k_pe"] // tn
    nblk = lay["total"] // tn
    tm = _pick(m, 640, 128)
    sub = tn // LANE
    last_b = (w_in_t.shape[1] - 1) // LANE

    def a_map(j, i):
        return (layer, jnp.where(j > nh, j - 1, j), 0)

    def b_map(j, i):
        return (layer, jnp.minimum(jnp.maximum(j, nh + 1) * sub, last_b), 0)

    return pl.pallas_call(
        functools.partial(_in_proj_kernel, nh=nh),
        grid=(nblk, m // tm),
        in_specs=[pl.BlockSpec((tm, k), lambda j, i: (i, 0)),
                  pl.BlockSpec((None, tn, k), a_map),
                  pl.BlockSpec((None, LANE, k), b_map)],
        out_specs=pl.BlockSpec((tm, tn), lambda j, i: (i, j)),
        out_shape=jax.ShapeDtypeStruct((m, lay["total"]), BF16),
        scratch_shapes=[pltpu.VMEM((tn, k), BF16)],
        compiler_params=_params(("arbitrary", "arbitrary")),
        name="in_proj",
    )(hn, w_in_t, w_in_t)


def _mm_res_kernel(x_ref, w_ref, r_ref, o_ref, wb_ref):
    @pl.when(pl.program_id(1) == 0)
    def _():
        _cast_weight(w_ref, wb_ref)

    o_ref[...] = r_ref[...] + _dot(x_ref[...], wb_ref[...])


def _mm_res(x, w, lead, kblk, kc, res, name):
    m = x.shape[0]
    n = w.shape[-1]
    tn = _pick(n, 512, 128)
    tm = _pick(m, 640, 128)
    nlead = len(lead)
    w_spec = pl.BlockSpec((None,) * nlead + (kc, tn), lambda j, i: lead + (kblk, j))
    return pl.pallas_call(
        _mm_res_kernel,
        grid=(n // tn, m // tm),
        in_specs=[pl.BlockSpec((tm, kc), lambda j, i: (i, kblk)),
                  w_spec,
                  pl.BlockSpec((tm, tn), lambda j, i: (i, j))],
        out_specs=pl.BlockSpec((tm, tn), lambda j, i: (i, j)),
        out_shape=jax.ShapeDtypeStruct((m, n), F32),
        scratch_shapes=[pltpu.VMEM((kc, tn), BF16)],
        compiler_params=_params(("arbitrary", "arbitrary"), VMEM_LIMIT_BIG),
        name=name,
    )(x, w, res)


def _rope(hi, cos, sin):
    lane = lax.broadcasted_iota(jnp.int32, hi.shape, 1)
    half = MLA_ROPE // 2
    rot = jnp.where(lane < half, pltpu.roll(hi, LANE - half, 1), pltpu.roll(hi, half, 1))
    return hi * cos + rot * sin


def _row_rms(x_ref, g_ref):
    x = x_ref[...].astype(F32)
    ms = jnp.mean(x * x, axis=-1, keepdims=True)
    return (x * lax.rsqrt(ms + EPS) * g_ref[...]).astype(BF16)


def _mla_q_kernel(x_ref, g_ref, w_ref, qg_ref, cos_ref, sin_ref, o_ref, *, heads):
    xn = _row_rms(x_ref, g_ref)
    qg = qg_ref[...]
    cos = cos_ref[...]
    sin = sin_ref[...]
    for h in range(heads):
        q = _dot(xn, w_ref[h])
        ss = jnp.sum(q * q, axis=-1, keepdims=True)
        qn = q * lax.rsqrt(ss * (1.0 / MLA_QK) + EPS) * qg
        hi = _rope(qn[:, LANE:], cos, sin)
        o_ref[h] = jnp.concatenate([qn[:, :LANE], hi], axis=-1).astype(o_ref.dtype)


def _mla_kv_kernel(x_ref, g_ref, w_ref, pe_ref, kg_ref, cos_ref, sin_ref, k_ref, v_ref, *, heads):
    xn = _row_rms(x_ref, g_ref)
    kg = kg_ref[...]
    pe = pe_ref[...].astype(F32)
    pe_ss = jnp.sum(pe * pe, axis=-1, keepdims=True)
    pe_rot = _rope(pe * kg[:, LANE:], cos_ref[...], sin_ref[...])
    ones = jnp.ones((xn.shape[0], MLA_V), v_ref.dtype)
    width = MLA_NOPE + MLA_V
    for h in range(heads):
        kv = _dot(xn, w_ref[:, h * width:(h + 1) * width])
        kn = kv[:, :MLA_NOPE]
        r = lax.rsqrt((jnp.sum(kn * kn, axis=-1, keepdims=True) + pe_ss) * (1.0 / MLA_QK) + EPS)
        k_ref[h] = jnp.concatenate([kn * r * kg[:, :LANE], pe_rot * r], axis=-1).astype(k_ref.dtype)
        v_ref[h] = jnp.concatenate([kv[:, MLA_NOPE:].astype(v_ref.dtype), ones], axis=-1)


def _normalise(oe):
    return oe[:, :MLA_V] / oe[:, MLA_V:]


def _mla_attn_kernel(q_ref, k_ref, v_ref, km_ref, vm_ref, o_ref, *, tq, batch):
    seq = q_ref.shape[0]
    nq = seq // tq
    b = pl.program_id(1)

    @pl.when(b < batch)
    def _():
        km = km_ref[...]
        vm = vm_ref[...]
        mcol = lax.broadcasted_iota(jnp.int32, (tq, BLOCK), 1)
        meta_bias = jnp.where(mcol >= META_LO, 0.0, NEG_INF)
        row = lax.broadcasted_iota(jnp.int32, (tq, tq), 0)
        col = lax.broadcasted_iota(jnp.int32, (tq, tq), 1)
        causal_bias = jnp.where(row >= col, 0.0, NEG_INF)
        for qi in range(nq):
            lo, hi = qi * tq, (qi + 1) * tq
            q = q_ref[lo:hi, :]
            s0 = _dot_nt(q, km) + meta_bias
            sd = _dot_nt(q, k_ref[lo:hi, :]) + causal_bias
            m = jnp.maximum(jnp.max(s0, axis=-1, keepdims=True), jnp.max(sd, axis=-1, keepdims=True))
            if qi > 0:
                sf = _dot_nt(q, k_ref[0:lo, :])
                m = jnp.maximum(m, jnp.max(sf, axis=-1, keepdims=True))
            oe = _dot(jnp.exp2(s0 - m).astype(BF16), vm)
            oe = oe + _dot(jnp.exp2(sd - m).astype(BF16), v_ref[lo:hi, :])
            if qi > 0:
                oe = oe + _dot(jnp.exp2(sf - m).astype(BF16), v_ref[0:lo, :])
            o_ref[lo:hi, :] = _normalise(oe).astype(o_ref.dtype)

    @pl.when(b == batch)
    def _():
        row = lax.broadcasted_iota(jnp.int32, (BLOCK, BLOCK), 0)
        col = lax.broadcasted_iota(jnp.int32, (BLOCK, BLOCK), 1)
        s = _dot_nt(q_ref[0:BLOCK, :], km_ref[...])
        s = jnp.where((col >= META_LO) & (row >= col), s, NEG_INF)
        m = jnp.max(s, axis=-1, keepdims=True)
        oe = _dot(jnp.exp2(s - m).astype(BF16), vm_ref[...])
        o_ref[0:BLOCK, :] = _normalise(oe).astype(o_ref.dtype)


def _mla_branch(proj, lay, cq_g, ckv_g, w_uq, w_ukv, qn_g, kn_g, cos_t, sin_t, batch, seq):
    m = proj.shape[0]
    qr = cq_g.shape[0]
    kvr = ckv_g.shape[0]
    heads = w_uq.shape[1] // MLA_QK
    tm = _pick(m, 640, 128)
    scale = MLA_QK ** -0.5 * LOG2E

    w_uq_h = jnp.pad(w_uq.reshape(qr, heads, MLA_QK), ((0, 0), (0, 0), (0, MLA_QK_PAD - MLA_QK)))
    w_uq_h = jnp.transpose(w_uq_h, (1, 0, 2)).astype(BF16)
    w_ukv_b = w_ukv.astype(BF16)
    qg = (jnp.pad(qn_g, (0, MLA_QK_PAD - MLA_QK)) * scale).reshape(1, MLA_QK_PAD)
    kg = jnp.pad(kn_g, (0, MLA_QK_PAD - MLA_QK)).reshape(1, MLA_QK_PAD)
    v_ext = MLA_V + MLA_V

    q = pl.pallas_call(
        functools.partial(_mla_q_kernel, heads=heads),
        grid=(m // tm,),
        in_specs=[pl.BlockSpec((tm, qr), lambda i: (i, lay["c_q"] // qr)),
                  pl.BlockSpec((1, qr), lambda i: (0, 0)),
                  pl.BlockSpec((heads, qr, MLA_QK_PAD), lambda i: (0, 0, 0)),
                  pl.BlockSpec((1, MLA_QK_PAD), lambda i: (0, 0)),
                  pl.BlockSpec((tm, LANE), lambda i: (i, 0)),
                  pl.BlockSpec((tm, LANE), lambda i: (i, 0))],
        out_specs=pl.BlockSpec((heads, tm, MLA_QK_PAD), lambda i: (0, i, 0)),
        out_shape=jax.ShapeDtypeStruct((heads, m, MLA_QK_PAD), BF16),
        compiler_params=_params(("arbitrary",)),
        name="mla_q",
    )(proj, cq_g.reshape(1, qr), w_uq_h, qg, cos_t, sin_t)

    k, v = pl.pallas_call(
        functools.partial(_mla_kv_kernel, heads=heads),
        grid=(m // tm,),
        in_specs=[pl.BlockSpec((tm, kvr), lambda i: (i, lay["c_kv"] // kvr)),
                  pl.BlockSpec((1, kvr), lambda i: (0, 0)),
                  pl.BlockSpec((kvr, heads * (MLA_NOPE + MLA_V)), lambda i: (0, 0)),
                  pl.BlockSpec((tm, LANE), lambda i: (i, lay["k_pe"] // LANE)),
                  pl.BlockSpec((1, MLA_QK_PAD), lambda i: (0, 0)),
                  pl.BlockSpec((tm, LANE), lambda i: (i, 0)),
                  pl.BlockSpec((tm, LANE), lambda i: (i, 0))],
        out_specs=[pl.BlockSpec((heads, tm, MLA_QK_PAD), lambda i: (0, i, 0)),
                   pl.BlockSpec((heads, tm, v_ext), lambda i: (0, i, 0))],
        out_shape=[jax.ShapeDtypeStruct((heads, m, MLA_QK_PAD), BF16),
                   jax.ShapeDtypeStruct((heads, m, v_ext), BF16)],
        compiler_params=_params(("arbitrary",)),
        name="mla_kv",
    )(proj, ckv_g.reshape(1, kvr), w_ukv_b, proj, kg, cos_t, sin_t)

    meta_blk = m // BLOCK - 1
    tq = _pick(seq, 512, 128)
    a = pl.pallas_call(
        functools.partial(_mla_attn_kernel, tq=tq, batch=batch),
        grid=(heads, batch + 1),
        in_specs=[pl.BlockSpec((None, seq, MLA_QK_PAD), lambda h, b: (h, b, 0)),
                  pl.BlockSpec((None, seq, MLA_QK_PAD), lambda h, b: (h, b, 0)),
                  pl.BlockSpec((None, seq, v_ext), lambda h, b: (h, b, 0)),
                  pl.BlockSpec((None, BLOCK, MLA_QK_PAD), lambda h, b: (h, meta_blk, 0)),
                  pl.BlockSpec((None, BLOCK, v_ext), lambda h, b: (h, meta_blk, 0))],
        out_specs=pl.BlockSpec((seq, MLA_V), lambda h, b: (b, h)),
        out_shape=jax.ShapeDtypeStruct((m, heads * MLA_V), BF16),
        compiler_params=_params(("arbitrary", "arbitrary")),
        name="mla_attn",
    )(q, k, v, k, v)
    return a


def _t5_bucket(rel, buckets):
    n = jnp.maximum(rel, 0)
    max_exact = buckets // 2
    nf = jnp.maximum(n, 1).astype(F32)
    large = max_exact + (jnp.log(nf / max_exact) / math.log(REL_MAX_DIST / max_exact)
                         * (buckets - max_exact)).astype(jnp.int32)
    large = jnp.minimum(large, buckets - 1)
    return jnp.where(n < max_exact, n, large)


def _bias_kernel(rb_ref, bucket_ref, o_ref, *, buckets):
    pr = pl.program_id(0)
    bucket = bucket_ref[...]
    qi = lax.broadcasted_iota(jnp.int32, bucket.shape, 0)
    sj = lax.broadcasted_iota(jnp.int32, bucket.shape, 1)
    band = (sj - qi >= 1) & (sj - qi <= WINDOW)
    halves = []
    for t in range(2):
        acc = jnp.zeros(bucket.shape, F32)
        for b in range(buckets):
            acc = jnp.where(bucket == b, rb_ref[b, 2 * pr + t], acc)
        halves.append(jnp.where(band, acc * LOG2E, NEG_INF))
    o_ref[...] = jnp.concatenate(halves, axis=1)


def _bias_table(rel_bias):
    buckets, hq = rel_bias.shape
    qi = jnp.arange(BLOCK)[:, None]
    sj = jnp.arange(2 * BLOCK)[None, :]
    bucket = _t5_bucket(qi + BLOCK - sj, buckets).astype(jnp.int32)
    return pl.pallas_call(
        functools.partial(_bias_kernel, buckets=buckets),
        grid=(hq // 2,),
        in_specs=[pl.BlockSpec(memory_space=pltpu.SMEM),
                  pl.BlockSpec((BLOCK, 2 * BLOCK), lambda h: (0, 0))],
        out_specs=pl.BlockSpec((None, BLOCK, 4 * BLOCK), lambda h: (h, 0, 0)),
        out_shape=jax.ShapeDtypeStruct((hq // 2, BLOCK, 4 * BLOCK), F32),
        compiler_params=_params(("arbitrary",)),
        name="rel_bias_table",
    )(rel_bias, bucket)


def _seg_rms(x, seg_ref, segt_ref, gain):
    ss = _dot((x * x).astype(BF16), seg_ref[...])
    r = lax.rsqrt(ss * (1.0 / SWA_HEAD_DIM) + EPS)
    return x * _dot(r.astype(BF16), segt_ref[...]) * gain


def _swa_kernel(prev_ref, plo_ref, clo_ref, q_ref, kc_ref, kp_ref, vc_ref, vp_ref, bias_ref, sink_ref,
                qg_ref, kg_ref, segq_ref, segqt_ref, segk_ref, segkt_ref, ones_ref, o_ref, *, hq, hkv):
    del prev_ref
    t = pl.program_id(0)
    plo = plo_ref[t]
    clo = clo_ref[t]
    half = LANE // 2
    pairs_per_group = hq // hkv // 2

    sj = lax.broadcasted_iota(jnp.int32, (BLOCK, 4 * BLOCK), 1) % (2 * BLOCK)
    in_prev = sj < BLOCK
    krow = jnp.where(in_prev, sj, sj - BLOCK)
    key_bias = jnp.where(krow >= jnp.where(in_prev, plo, clo), 0.0, NEG_INF)

    qn = _seg_rms(q_ref[...].astype(F32), segq_ref, segqt_ref, qg_ref[...]).astype(BF16)
    kf = jnp.concatenate([kp_ref[...], kc_ref[...]], axis=0).astype(F32)
    kn = _seg_rms(kf, segk_ref, segkt_ref, kg_ref[...])
    vf = jnp.concatenate([vp_ref[...], vc_ref[...]], axis=0).astype(F32)

    lane2 = lax.broadcasted_iota(jnp.int32, (2 * BLOCK, LANE), 1)
    lane1 = lax.broadcasted_iota(jnp.int32, (BLOCK, LANE), 1)

    def block_diag(col, odd):
        other = pltpu.roll(col, half, 1)
        on_lo, on_hi = (other, col) if odd else (col, other)
        return jnp.concatenate([jnp.where(lane2 < half, on_lo, 0.0), jnp.where(lane2 < half, 0.0, on_hi)],
                               axis=0).astype(BF16)

    outs = []
    for g in range(hkv):
        c = g // 2
        kbd = block_diag(kn[:, c * LANE:(c + 1) * LANE], g % 2)
        vext = jnp.concatenate([block_diag(vf[:, c * LANE:(c + 1) * LANE], g % 2), ones_ref[...]], axis=1)
        for jj in range(pairs_per_group):
            pr = g * pairs_per_group + jj
            s = _dot_nt(qn[:, pr * LANE:(pr + 1) * LANE], kbd) + bias_ref[pr] + key_bias
            sa, sb = s[:, :2 * BLOCK], s[:, 2 * BLOCK:]
            ma = jnp.maximum(jnp.max(sa, axis=-1, keepdims=True), sink_ref[2 * pr])
            mb = jnp.maximum(jnp.max(sb, axis=-1, keepdims=True), sink_ref[2 * pr + 1])
            e = jnp.concatenate([jnp.exp2(sa - ma), jnp.exp2(sb - mb)], axis=1).astype(BF16)
            oe = _dot(e, vext)
            sink_term = jnp.where(lane1 < half, jnp.exp2(sink_ref[2 * pr] - ma),
                                  jnp.exp2(sink_ref[2 * pr + 1] - mb))
            outs.append(oe[:, :LANE] / (oe[:, LANE:] + sink_term))
    o_ref[...] = jnp.concatenate(outs, axis=-1).astype(o_ref.dtype)


def _swa_branch(proj, lay, bias_tab, sinks, qg, kg, batch, seq):
    m = proj.shape[0]
    hq = sinks.shape[0]
    hd = SWA_HEAD_DIM
    qw = hq * hd
    kw = lay["v_s"] - lay["k_s"]
    hkv = kw // hd
    nblk = m // BLOCK
    per = seq // BLOCK
    t = jnp.arange(nblk, dtype=jnp.int32)
    is_meta = t == nblk - 1
    first = (t % per) == 0
    prev = jnp.where(is_meta | first, nblk - 1, t - 1).astype(jnp.int32)
    plo = jnp.where(is_meta, BLOCK, jnp.where(first, META_LO, 0)).astype(jnp.int32)
    clo = jnp.where(is_meta, META_LO, 0).astype(jnp.int32)
    assert (hq // hkv) % 2 == 0 and kw % LANE == 0 and hq <= LANE

    def seg(width):
        return (jnp.arange(width)[:, None] // hd == jnp.arange(LANE)[None, :]).astype(BF16)

    segq, segk = seg(qw), seg(kw)
    qg_t = (jnp.tile(qg, hq) * (hd ** -0.5 * LOG2E)).reshape(1, qw)
    kg_t = jnp.tile(kg, hkv).reshape(1, kw)
    ones_bd = ((jnp.arange(4 * BLOCK)[:, None] < 2 * BLOCK)
               == (jnp.arange(LANE)[None, :] < LANE // 2)).astype(BF16)

    def const(shape):
        return pl.BlockSpec(shape, lambda i, pr, pl_, cl: (0,) * len(shape))

    grid_spec = pltpu.PrefetchScalarGridSpec(
        num_scalar_prefetch=3,
        grid=(nblk,),
        in_specs=[pl.BlockSpec((BLOCK, qw), lambda i, pr, pl_, cl: (i, lay["q_s"] // qw)),
                  pl.BlockSpec((BLOCK, kw), lambda i, pr, pl_, cl: (i, lay["k_s"] // kw)),
                  pl.BlockSpec((BLOCK, kw), lambda i, pr, pl_, cl: (pr[i], lay["k_s"] // kw)),
                  pl.BlockSpec((BLOCK, kw), lambda i, pr, pl_, cl: (i, lay["v_s"] // kw)),
                  pl.BlockSpec((BLOCK, kw), lambda i, pr, pl_, cl: (pr[i], lay["v_s"] // kw)),
                  const((hq // 2, BLOCK, 4 * BLOCK)),
                  pl.BlockSpec(memory_space=pltpu.SMEM),
                  const((1, qw)), const((1, kw)),
                  const((qw, LANE)), const((LANE, qw)), const((kw, LANE)), const((LANE, kw)),
                  const((4 * BLOCK, LANE))],
        out_specs=pl.BlockSpec((BLOCK, qw), lambda i, pr, pl_, cl: (i, 0)),
    )
    return pl.pallas_call(
        functools.partial(_swa_kernel, hq=hq, hkv=hkv),
        grid_spec=grid_spec,
        out_shape=jax.ShapeDtypeStruct((m, qw), BF16),
        compiler_params=_params(("arbitrary",)),
        name="swa_attn",
    )(prev, plo, clo, proj, proj, proj, proj, proj, bias_tab, sinks * LOG2E, qg_t, kg_t,
      segq, segq.T, segk, segk.T, ones_bd)


def _merge_kernel(a_ref, b_ref, wa_ref, wb_ref, ga_ref, gb_ref, o_ref, wab_ref, wbb_ref):
    @pl.when(pl.program_id(1) == 0)
    def _():
        _cast_weight(wa_ref, wab_ref)
        _cast_weight(wb_ref, wbb_ref)

    ya = _dot(a_ref[...], wab_ref[...])
    yb = _dot(b_ref[...], wbb_ref[...])
    ga = _sigmoid(ga_ref[...].astype(F32))
    gb = _sigmoid(gb_ref[...].astype(F32))
    o_ref[...] = (ga * ya + gb * yb).astype(o_ref.dtype)


def _merge(a, b, w_a, w_b, layer, proj, lay, tn):
    m, ka = a.shape
    kb = b.shape[1]
    d = w_a.shape[-1]
    tm = _pick(m, 640, 128)
    ga0 = lay["g_a"] // tn
    gb0 = lay["g_b"] // tn
    return pl.pallas_call(
        _merge_kernel,
        grid=(d // tn, m // tm),
        in_specs=[pl.BlockSpec((tm, ka), lambda j, i: (i, 0)),
                  pl.BlockSpec((tm, kb), lambda j, i: (i, 0)),
                  pl.BlockSpec((None, ka, tn), lambda j, i: (layer, 0, j)),
                  pl.BlockSpec((None, kb, tn), lambda j, i: (layer, 0, j)),
                  pl.BlockSpec((tm, tn), lambda j, i: (i, ga0 + j)),
                  pl.BlockSpec((tm, tn), lambda j, i: (i, gb0 + j))],
        out_specs=pl.BlockSpec((tm, tn), lambda j, i: (i, j)),
        out_shape=jax.ShapeDtypeStruct((m, d), BF16),
        scratch_shapes=[pltpu.VMEM((ka, tn), BF16), pltpu.VMEM((kb, tn), BF16)],
        compiler_params=_params(("arbitrary", "arbitrary")),
        name="branch_merge",
    )(a, b, w_a, w_b, proj, proj)


def _ffn_up_kernel(x_ref, w1_ref, w3_ref, o_ref, w1b_ref, w3b_ref):
    @pl.when(pl.program_id(1) == 0)
    def _():
        _cast_weight(w1_ref, w1b_ref)
        _cast_weight(w3_ref, w3b_ref)

    x = x_ref[...]
    u = _dot(x, w1b_ref[...])
    g = _dot(x, w3b_ref[...])
    o_ref[...] = (u * _sigmoid(u) * g).astype(o_ref.dtype)


def _ffn_up(hn, w1, w3, idx):
    m, d = hn.shape
    ff = w1.shape[-1]
    tn = _pick(ff, 256, 128)
    tm = _pick(m, 640, 128)
    return pl.pallas_call(
        _ffn_up_kernel,
        grid=(ff // tn, m // tm),
        in_specs=[pl.BlockSpec((tm, d), lambda j, i: (i, 0)),
                  pl.BlockSpec((None, d, tn), lambda j, i: (idx, 0, j)),
                  pl.BlockSpec((None, d, tn), lambda j, i: (idx, 0, j))],
        out_specs=pl.BlockSpec((tm, tn), lambda j, i: (i, j)),
        out_shape=jax.ShapeDtypeStruct((m, ff), BF16),
        scratch_shapes=[pltpu.VMEM((d, tn), BF16), pltpu.VMEM((d, tn), BF16)],
        compiler_params=_params(("arbitrary", "arbitrary")),
        name="ffn_up",
    )(hn, w1, w3)


def _dense_ffn(h_res, gain, w1, w3, w2, idx):
    hn = _rms_norm(h_res, gain)
    t = _ffn_up(hn, w1, w3, idx)
    ff = w1.shape[-1]
    nk = 1
    for cand in (1, 2, 4):
        if ff % cand == 0 and (ff // cand) % LANE == 0 and ff // cand <= 6144:
            nk = cand
            break
    kc = ff // nk
    out = h_res
    for kblk in range(nk):
        out = _mm_res(t, w2, (idx,), kblk, kc, out, "ffn_down")
    return out


def _router_kernel(x_ref, g_ref, w_ref, o_ref, *, n_exp):
    x = x_ref[...]
    ms = jnp.mean(x * x, axis=-1, keepdims=True)
    xn = (x * lax.rsqrt(ms + EPS) * g_ref[...]).astype(BF16)
    logits = _dot(xn, w_ref[...].astype(BF16))
    lane = lax.broadcasted_iota(jnp.int32, logits.shape, 1)
    logits = jnp.where(lane < n_exp, logits, -jnp.inf)
    lane_f = lane.astype(F32)
    m1 = jnp.max(logits, axis=-1, keepdims=True)
    i1 = jnp.min(jnp.where(logits == m1, lane_f, float(LANE)), axis=-1, keepdims=True)
    rest = jnp.where(lane_f == i1, -jnp.inf, logits)
    m2 = jnp.max(rest, axis=-1, keepdims=True)
    i2 = jnp.min(jnp.where(rest == m2, lane_f, float(LANE)), axis=-1, keepdims=True)
    e2 = jnp.exp(m2 - m1)
    w1 = 1.0 / (1.0 + e2)
    w2 = e2 / (1.0 + e2)
    out = jnp.where(lane == 0, i1, jnp.where(lane == 1, i2,
                                             jnp.where(lane == 2, w1, jnp.where(lane == 3, w2, 0.0))))
    o_ref[...] = out


def _router(h_res, gain, router_w):
    m, d = h_res.shape
    n_exp = router_w.shape[-1]
    tm = _pick(m, 640, 128)
    wpad = jnp.pad(router_w, ((0, 0), (0, LANE - n_exp)))
    return pl.pallas_call(
        functools.partial(_router_kernel, n_exp=n_exp),
        grid=(m // tm,),
        in_specs=[pl.BlockSpec((tm, d), lambda i: (i, 0)),
                  pl.BlockSpec((1, d), lambda i: (0, 0)),
                  pl.BlockSpec((d, LANE), lambda i: (0, 0))],
        out_specs=pl.BlockSpec((tm, LANE), lambda i: (i, 0)),
        out_shape=jax.ShapeDtypeStruct((m, LANE), F32),
        compiler_params=_params(("arbitrary",)),
        name="moe_router",
    )(h_res, gain.reshape(1, d), wpad)


def _row_copy(src_hbm, buf, sem, slot, src_row, dst_row):
    return pltpu.make_async_copy(src_hbm.at[pl.ds(src_row, 1)], buf.at[slot, pl.ds(dst_row, 1)],
                                 sem.at[slot])


def _start_row_gather(src_hbm, buf, sem, slot, idx_ref, base, count):
    def body(r2, c):
        for p in range(2):
            r = 2 * r2 + p
            _row_copy(src_hbm, buf, sem, slot, idx_ref[base + r], r).start(priority=p)
        return c
    lax.fori_loop(0, count // 2, body, 0)


def _wait_row_gather(src_hbm, buf, sem, slot, count):
    def body(r, c):
        _row_copy(src_hbm, buf, sem, slot, 0, r).wait()
        return c
    lax.fori_loop(0, count, body, 0)


def _gather_norm_kernel(tok_ref, live_ref, h_hbm, g_ref, o_ref, buf, sem, *, rows):
    s = pl.program_id(0)
    n = pl.num_programs(0)

    @pl.when((s == 0) & (live_ref[0] == 1))
    def _():
        _start_row_gather(h_hbm, buf, sem, 0, tok_ref, 0, rows)

    nxt = jnp.minimum(s + 1, n - 1)

    @pl.when((s + 1 < n) & (live_ref[nxt] == 1))
    def _():
        _start_row_gather(h_hbm, buf, sem, (s + 1) % 2, tok_ref, (s + 1) * rows, rows)

    slot = s % 2

    @pl.when(live_ref[s] == 1)
    def _():
        _wait_row_gather(h_hbm, buf, sem, slot, rows)
        x = buf[slot]
        ms = jnp.mean(x * x, axis=-1, keepdims=True)
        o_ref[...] = (x * lax.rsqrt(ms + EPS) * g_ref[...]).astype(o_ref.dtype)

    @pl.when(live_ref[s] == 0)
    def _():
        o_ref[...] = jnp.zeros(o_ref.shape, o_ref.dtype)


def _gather_norm(h_res, gain, row_tok, live, rows):
    d = h_res.shape[1]
    rp = row_tok.shape[0]
    grid_spec = pltpu.PrefetchScalarGridSpec(
        num_scalar_prefetch=2,
        grid=(rp // rows,),
        in_specs=[pl.BlockSpec(memory_space=pl.ANY),
                  pl.BlockSpec((1, d), lambda i, tok, lv: (0, 0))],
        out_specs=pl.BlockSpec((rows, d), lambda i, tok, lv: (i, 0)),
        scratch_shapes=[pltpu.VMEM((2, rows, d), F32), pltpu.SemaphoreType.DMA((2,))],
    )
    return pl.pallas_call(
        functools.partial(_gather_norm_kernel, rows=rows),
        grid_spec=grid_spec,
        out_shape=jax.ShapeDtypeStruct((rp, d), BF16),
        compiler_params=_params(("arbitrary",)),
        name="moe_gather_norm",
    )(row_tok, live, h_res, gain.reshape(1, d))


def _for_live_subtiles(valid_ref, o_ref, sub, compute):
    i = pl.program_id(1)
    for s in range(o_ref.shape[0] // sub):
        rows = slice(s * sub, (s + 1) * sub)

        @pl.when(valid_ref[i] > s)
        def _(rows=rows):
            o_ref[rows, :] = compute(rows)

        @pl.when(valid_ref[i] <= s)
        def _(rows=rows):
            o_ref[rows, :] = jnp.zeros((sub, o_ref.shape[1]), o_ref.dtype)


def _moe_up_kernel(be_ref, first_ref, valid_ref, xb_ref, x_ref, w1_ref, w3_ref, o_ref, w1b_ref, w3b_ref,
                   *, sub):
    @pl.when(first_ref[pl.program_id(1)] == 1)
    def _():
        _cast_weight(w1_ref, w1b_ref)
        _cast_weight(w3_ref, w3b_ref)

    def compute(rows):
        x = x_ref[rows, :]
        u = _dot(x, w1b_ref[...])
        g = _dot(x, w3b_ref[...])
        return (u * _sigmoid(u) * g).astype(o_ref.dtype)

    _for_live_subtiles(valid_ref, o_ref, sub, compute)


def _moe_down_kernel(be_ref, first_ref, valid_ref, xb_ref, x_ref, w_ref, o_ref, wb_ref, *, sub):
    @pl.when(first_ref[pl.program_id(1)] == 1)
    def _():
        _cast_weight(w_ref, wb_ref)

    _for_live_subtiles(valid_ref, o_ref, sub, lambda rows: _dot(x_ref[rows, :], wb_ref[...]))


def _combine_kernel(pos_ref, h_ref, rt_ref, y_hbm, o_ref, buf, sem, *, rows):
    s = pl.program_id(0)
    n = pl.num_programs(0)

    @pl.when(s == 0)
    def _():
        _start_row_gather(y_hbm, buf, sem, 0, pos_ref, 0, 2 * rows)

    @pl.when(s + 1 < n)
    def _():
        _start_row_gather(y_hbm, buf, sem, (s + 1) % 2, pos_ref, (s + 1) * 2 * rows, 2 * rows)

    slot = s % 2
    _wait_row_gather(y_hbm, buf, sem, slot, 2 * rows)

    rt = rt_ref[...]
    y0 = buf[slot, 0:rows, :]
    y1 = buf[slot, rows:2 * rows, :]
    o_ref[...] = h_ref[...] + rt[:, 2:3] * y0 + rt[:, 3:4] * y1


def _moe_ffn(h_res, gain, router_w, w1, w3, w2, idx, n_out):
    m, d = h_res.shape
    n_exp = router_w.shape[-1]
    ff = w1.shape[-1]
    rt = _router(h_res, gain, router_w)

    sub = 256
    tme = 2 * sub
    grows = 128
    r_tot = TOP_K * m
    nb = -(-(r_tot + n_exp * (tme - 1)) // tme)
    rp = nb * tme
    e = rt[:, :TOP_K].astype(jnp.int32).reshape(-1)
    onehot = (e[:, None] == jnp.arange(n_exp, dtype=jnp.int32)[None, :]).astype(jnp.int32)
    csum = jnp.cumsum(onehot, axis=0)
    rank = jnp.sum(onehot * csum, axis=1) - 1
    counts = csum[-1]
    padded = (counts + tme - 1) // tme * tme
    gend = jnp.cumsum(padded)
    goff = gend - padded
    pos = (jnp.sum(onehot * goff[None, :], axis=1) + rank).astype(jnp.int32)
    row_tok = jnp.zeros((rp,), jnp.int32).at[pos].set(jnp.arange(r_tot, dtype=jnp.int32) // TOP_K)
    used = gend[-1] // tme
    blk = jnp.arange(nb, dtype=jnp.int32)
    be = jnp.minimum(jnp.sum((blk[:, None] * tme >= gend[None, :]).astype(jnp.int32), axis=1),
                     n_exp - 1).astype(jnp.int32)
    first = jnp.concatenate([jnp.ones((1,), jnp.int32), (be[1:] != be[:-1]).astype(jnp.int32)])
    xb = jnp.minimum(blk, used - 1).astype(jnp.int32)

    def live_tiles(tile, per_block):
        start = jnp.arange(rp // tile, dtype=jnp.int32) * tile
        e_of = jnp.repeat(be, tme // tile)
        real_end = jnp.where(jnp.repeat(blk < used, tme // tile), (goff + counts)[e_of], 0)
        live = (start < real_end).astype(jnp.int32)
        return live if per_block is None else jnp.sum(live.reshape(nb, per_block), axis=1).astype(jnp.int32)

    valid = live_tiles(sub, tme // sub)
    glive = live_tiles(grows, None)

    xs = _gather_norm(h_res, gain, row_tok, glive, grows)

    tn_up = _pick(ff, 512, 128)
    up_spec = pltpu.PrefetchScalarGridSpec(
        num_scalar_prefetch=4,
        grid=(ff // tn_up, nb),
        in_specs=[pl.BlockSpec((tme, d), lambda j, i, be_, f_, v_, xb_: (xb_[i], 0)),
                  pl.BlockSpec((None, None, d, tn_up), lambda j, i, be_, f_, v_, xb_: (idx, be_[i], 0, j)),
                  pl.BlockSpec((None, None, d, tn_up), lambda j, i, be_, f_, v_, xb_: (idx, be_[i], 0, j))],
        out_specs=pl.BlockSpec((tme, tn_up), lambda j, i, be_, f_, v_, xb_: (i, j)),
        scratch_shapes=[pltpu.VMEM((d, tn_up), BF16), pltpu.VMEM((d, tn_up), BF16)],
    )
    t = pl.pallas_call(
        functools.partial(_moe_up_kernel, sub=sub),
        grid_spec=up_spec,
        out_shape=jax.ShapeDtypeStruct((rp, ff), BF16),
        compiler_params=_params(("arbitrary", "arbitrary"), VMEM_LIMIT_BIG),
        name="moe_up",
    )(be, first, valid, xb, xs, w1, w3)

    tn_dn = _pick(d, 512, 128)
    dn_spec = pltpu.PrefetchScalarGridSpec(
        num_scalar_prefetch=4,
        grid=(d // tn_dn, nb),
        in_specs=[pl.BlockSpec((tme, ff), lambda j, i, be_, f_, v_, xb_: (xb_[i], 0)),
                  pl.BlockSpec((None, None, ff, tn_dn), lambda j, i, be_, f_, v_, xb_: (idx, be_[i], 0, j))],
        out_specs=pl.BlockSpec((tme, tn_dn), lambda j, i, be_, f_, v_, xb_: (i, j)),
        scratch_shapes=[pltpu.VMEM((ff, tn_dn), BF16)],
    )
    y = pl.pallas_call(
        functools.partial(_moe_down_kernel, sub=sub),
        grid_spec=dn_spec,
        out_shape=jax.ShapeDtypeStruct((rp, d), F32),
        compiler_params=_params(("arbitrary", "arbitrary"), VMEM_LIMIT_BIG),
        name="moe_down",
    )(be, first, valid, xb, t, w2)

    rows = 128
    cmb_spec = pltpu.PrefetchScalarGridSpec(
        num_scalar_prefetch=1,
        grid=(n_out // rows,),
        in_specs=[pl.BlockSpec((rows, d), lambda i, p: (i, 0)),
                  pl.BlockSpec((rows, LANE), lambda i, p: (i, 0)),
                  pl.BlockSpec(memory_space=pl.ANY)],
        out_specs=pl.BlockSpec((rows, d), lambda i, p: (i, 0)),
        scratch_shapes=[pltpu.VMEM((2, 2 * rows, d), F32), pltpu.SemaphoreType.DMA((2,))],
    )
    pos_blk = jnp.transpose(pos.reshape(m // rows, rows, TOP_K), (0, 2, 1)).reshape(-1)
    return pl.pallas_call(
        functools.partial(_combine_kernel, rows=rows),
        grid_spec=cmb_spec,
        out_shape=jax.ShapeDtypeStruct((n_out, d), F32),
        compiler_params=_params(("arbitrary",)),
        name="moe_combine",
    )(pos_blk, h_res, rt, y)


def _in_layout(d, qr, kvr, qw, kw):
    head = qr + kvr
    rest = qw + 2 * kw + 2 * d
    tn = max(t for t in (512, 256, 128) if head % t == 0 and rest % t == 0)
    lay = {"c_q": 0, "c_kv": qr, "k_pe": head}
    cur = head + tn
    for name, width in (("q_s", qw), ("k_s", kw), ("v_s", kw), ("g_a", d), ("g_b", d)):
        lay[name] = cur
        cur += width
    lay["total"] = cur
    assert lay["c_kv"] % kvr == 0 and lay["q_s"] % qw == 0 and lay["k_s"] % kw == 0 and lay["v_s"] % kw == 0
    assert lay["g_a"] % tn == 0 and lay["g_b"] % tn == 0 and d % tn == 0
    return lay, tn


def kernel(x, meta_tokens, rel_bias, attn_norm, w_in, mla_cq_norm, mla_ckv_norm, mla_w_uq, mla_w_ukv,
           mla_q_norm, mla_k_norm, swa_q_norm, swa_k_norm, swa_sinks, w_branch_mla, w_branch_swa, w_out,
           ffn_norm, dense_w1, dense_w3, dense_w2, moe_router, moe_w1, moe_w3, moe_w2):
    batch, seq, d = x.shape
    depth = w_in.shape[0]
    qr = mla_cq_norm.shape[1]
    kvr = mla_ckv_norm.shape[1]
    hq = swa_sinks.shape[1]
    qw = hq * SWA_HEAD_DIM
    kw = (w_in.shape[2] - qr - kvr - MLA_ROPE - qw - 2 * d) // 2
    assert seq % BLOCK == 0 and meta_tokens.shape[0] == N_META
    n_tok = batch * seq
    m = n_tok + BLOCK

    lay, tn_in = _in_layout(d, qr, kvr, qw, kw)

    h_res = jnp.concatenate([x.reshape(n_tok, d), jnp.zeros((META_LO, d), x.dtype),
                             meta_tokens.astype(x.dtype)], axis=0)

    pos = jnp.concatenate([jnp.tile(N_META + jnp.arange(seq), batch), jnp.arange(BLOCK) - META_LO])
    half = MLA_ROPE // 2
    inv_freq = ROPE_THETA ** (-jnp.arange(half, dtype=F32) / half)
    ang = pos.astype(F32)[:, None] * inv_freq[None, :]
    cos, sin = jnp.cos(ang), jnp.sin(ang)
    cos_t = jnp.concatenate([cos, cos, jnp.ones((m, LANE - MLA_ROPE), F32)], axis=1)
    sin_t = jnp.concatenate([-sin, sin, jnp.zeros((m, LANE - MLA_ROPE), F32)], axis=1)

    bias_tab = _bias_table(rel_bias)
    w_in_t = jnp.swapaxes(w_in, 1, 2)

    for i in range(depth):
        hn = _rms_norm(h_res, attn_norm[i])
        proj = _in_proj(hn, w_in_t, i, lay, tn_in)
        a = _mla_branch(proj, lay, mla_cq_norm[i], mla_ckv_norm[i], mla_w_uq[i], mla_w_ukv[i],
                        mla_q_norm[i], mla_k_norm[i], cos_t, sin_t, batch, seq)
        b = _swa_branch(proj, lay, bias_tab, swa_sinks[i], swa_q_norm[i], swa_k_norm[i], batch, seq)
        merged = _merge(a, b, w_branch_mla, w_branch_swa, i, proj, lay, tn_in)
        h_res = _mm_res(merged, w_out, (i,), 0, d, h_res, "out_proj")
        last = i == depth - 1
        if i % 2 == 0:
            h_res = _dense_ffn(h_res, ffn_norm[i], dense_w1, dense_w3, dense_w2, i // 2)
            if last:
                h_res = h_res[:n_tok]
        else:
            h_res = _moe_ffn(h_res, ffn_norm[i], moe_router[i // 2], moe_w1, moe_w3, moe_w2, i // 2,
                             n_tok if last else m)
    return h_res.reshape(batch, seq, d)
```

```python
import functools
import math

import jax
import jax.numpy as jnp
from jax import lax
from jax.experimental import pallas as pl
from jax.experimental.pallas import tpu as pltpu

F32 = jnp.float32
BF16 = jnp.bfloat16

N_META = 16
BLOCK = 128
WINDOW = 128
MLA_NOPE = 128
MLA_ROPE = 64
MLA_V = 128
MLA_QK = MLA_NOPE + MLA_ROPE
MLA_QK_PAD = 256
ROPE_THETA = 10000.0
SWA_HEAD_DIM = 64
REL_MAX_DIST = 128
TOP_K = 2
EPS = 1e-6
NEG_INF = -1e30
LOG2E = 1.4426950408889634
LANE = 128
META_LO = BLOCK - N_META

VMEM_LIMIT_BIG = 58 * 1024 * 1024
VMEM_LIMIT_MID = 44 * 1024 * 1024


def _params(sem, vmem=VMEM_LIMIT_MID):
    return pltpu.CompilerParams(dimension_semantics=sem, vmem_limit_bytes=vmem)


def _pick(n, target, mult):
    best = None
    d = mult
    while d <= min(n, target):
        if n % d == 0:
            best = d
        d += mult
    return best if best is not None else n


def _round_up(a, b):
    return (a + b - 1) // b * b


def _cast_weight(w_ref, wb_ref):
    k = w_ref.shape[0]
    ch = 512 if k % 512 == 0 else (256 if k % 256 == 0 else 128)
    if k % ch != 0:
        wb_ref[...] = w_ref[...].astype(BF16)
        return

    def body(c, carry):
        r = pl.multiple_of(c * ch, ch)
        wb_ref[pl.ds(r, ch), :] = w_ref[pl.ds(r, ch), :].astype(BF16)
        return carry

    lax.fori_loop(0, k // ch, body, 0)


def _sigmoid(x):
    return 1.0 / (1.0 + jnp.exp(-x))


def _dot(a, b):
    return jnp.dot(a, b, preferred_element_type=F32)


def _dot_nt(a, b):
    return lax.dot_general(a, b, (((1,), (1,)), ((), ())), preferred_element_type=F32)


def _rms_kernel(x_ref, g_ref, o_ref):
    x = x_ref[...]
    ms = jnp.mean(x * x, axis=-1, keepdims=True)
    o_ref[...] = (x * lax.rsqrt(ms + EPS) * g_ref[...]).astype(o_ref.dtype)


def _rms_norm(h, gain):
    m, d = h.shape
    tm = _pick(m, 640, 128)
    return pl.pallas_call(
        _rms_kernel,
        grid=(m // tm,),
        in_specs=[pl.BlockSpec((tm, d), lambda i: (i, 0)),
                  pl.BlockSpec((1, d), lambda i: (0, 0))],
        out_specs=pl.BlockSpec((tm, d), lambda i: (i, 0)),
        out_shape=jax.ShapeDtypeStruct((m, d), BF16),
        compiler_params=_params(("arbitrary",)),
        name="rms_norm",
    )(h, gain.reshape(1, d))


def _in_proj_kernel(x_ref, a_ref, b_ref, o_ref, wb_ref, *, nh):
    j = pl.program_id(0)
    tn, k = a_ref.shape
    half = LANE // 2

    @pl.when(pl.program_id(1) == 0)
    def _():
        @pl.when(j < nh)
        def _():
            wb_ref[...] = a_ref[...].astype(BF16)

        @pl.when(j == nh)
        def _():
            wb_ref[0:half, :] = a_ref[0:half, :].astype(BF16)
            wb_ref[half:tn, :] = jnp.zeros((tn - half, k), BF16)

        @pl.when(j > nh)
        def _():
            wb_ref[0:tn - half, :] = a_ref[half:tn, :].astype(BF16)
            wb_ref[tn - half:tn, :] = b_ref[0:half, :].astype(BF16)

    o_ref[...] = _dot_nt(x_ref[...], wb_ref[...]).astype(o_ref.dtype)


def _in_proj(hn, w_in_t, layer, lay, tn):
    m, k = hn.shape
    nh = lay["k_pe"] // tn
    nblk = lay["total"] // tn
    tm = _pick(m, 1664, 128)
    sub = tn // LANE
    last_b = (w_in_t.shape[1] - 1) // LANE

    def a_map(j, i):
        return (layer, jnp.where(j > nh, j - 1, j), 0)

    def b_map(j, i):
        return (layer, jnp.minimum(jnp.maximum(j, nh + 1) * sub, last_b), 0)

    return pl.pallas_call(
        functools.partial(_in_proj_kernel, nh=nh),
        grid=(nblk, m // tm),
        in_specs=[pl.BlockSpec((tm, k), lambda j, i: (i, 0)),
                  pl.BlockSpec((None, tn, k), a_map),
                  pl.BlockSpec((None, LANE, k), b_map)],
        out_specs=pl.BlockSpec((tm, tn), lambda j, i: (i, j)),
        out_shape=jax.ShapeDtypeStruct((m, lay["total"]), BF16),
        scratch_shapes=[pltpu.VMEM((tn, k), BF16)],
        compiler_params=_params(("arbitrary", "arbitrary"), VMEM_LIMIT_BIG),
        name="in_proj",
    )(hn, w_in_t, w_in_t)


def _mm_res_kernel(x_ref, w_ref, r_ref, o_ref, wb_ref):
    @pl.when(pl.program_id(1) == 0)
    def _():
        _cast_weight(w_ref, wb_ref)

    o_ref[...] = r_ref[...] + _dot(x_ref[...], wb_ref[...])


def _mm_res(x, w, lead, kblk, kc, res, name):
    m = x.shape[0]
    n = w.shape[-1]
    tn = _pick(n, 512, 128)
    tm = _pick(m, 640, 128)
    nlead = len(lead)
    w_spec = pl.BlockSpec((None,) * nlead + (kc, tn), lambda j, i: lead + (kblk, j))
    return pl.pallas_call(
        _mm_res_kernel,
        grid=(n // tn, m // tm),
        in_specs=[pl.BlockSpec((tm, kc), lambda j, i: (i, kblk)),
                  w_spec,
                  pl.BlockSpec((tm, tn), lambda j, i: (i, j))],
        out_specs=pl.BlockSpec((tm, tn), lambda j, i: (i, j)),
        out_shape=jax.ShapeDtypeStruct((m, n), F32),
        scratch_shapes=[pltpu.VMEM((kc, tn), BF16)],
        compiler_params=_params(("arbitrary", "arbitrary"), VMEM_LIMIT_BIG),
        name=name,
    )(x, w, res)


def _rope(hi, cos, sin):
    lane = lax.broadcasted_iota(jnp.int32, hi.shape, 1)
    half = MLA_ROPE // 2
    rot = jnp.where(lane < half, pltpu.roll(hi, LANE - half, 1), pltpu.roll(hi, half, 1))
    return hi * cos + rot * sin


def _row_rms(x_ref, g_ref):
    x = x_ref[...].astype(F32)
    ms = jnp.mean(x * x, axis=-1, keepdims=True)
    return (x * lax.rsqrt(ms + EPS) * g_ref[...]).astype(BF16)


def _mla_q_kernel(x_ref, g_ref, w_ref, qg_ref, cos_ref, sin_ref, o_ref, *, heads):
    xn = _row_rms(x_ref, g_ref)
    qg = qg_ref[...]
    cos = cos_ref[...]
    sin = sin_ref[...]
    for h in range(heads):
        q = _dot(xn, w_ref[h])
        ss = jnp.sum(q * q, axis=-1, keepdims=True)
        qn = q * lax.rsqrt(ss * (1.0 / MLA_QK) + EPS) * qg
        hi = _rope(qn[:, LANE:], cos, sin)
        o_ref[h] = jnp.concatenate([qn[:, :LANE], hi], axis=-1).astype(o_ref.dtype)


def _mla_kv_kernel(x_ref, g_ref, w_ref, pe_ref, kg_ref, cos_ref, sin_ref, k_ref, v_ref, *, heads):
    xn = _row_rms(x_ref, g_ref)
    kg = kg_ref[...]
    pe = pe_ref[...].astype(F32)
    pe_ss = jnp.sum(pe * pe, axis=-1, keepdims=True)
    pe_rot = _rope(pe * kg[:, LANE:], cos_ref[...], sin_ref[...])
    ones = jnp.ones((xn.shape[0], MLA_V), v_ref.dtype)
    width = MLA_NOPE + MLA_V
    for h in range(heads):
        kv = _dot(xn, w_ref[:, h * width:(h + 1) * width])
        kn = kv[:, :MLA_NOPE]
        r = lax.rsqrt((jnp.sum(kn * kn, axis=-1, keepdims=True) + pe_ss) * (1.0 / MLA_QK) + EPS)
        k_ref[h] = jnp.concatenate([kn * r * kg[:, :LANE], pe_rot * r], axis=-1).astype(k_ref.dtype)
        v_ref[h] = jnp.concatenate([kv[:, MLA_NOPE:].astype(v_ref.dtype), ones], axis=-1)


def _normalise(oe):
    return oe[:, :MLA_V] / oe[:, MLA_V:]


def _mla_attn_kernel(q_ref, k_ref, v_ref, km_ref, vm_ref, o_ref, *, tq, batch):
    seq = q_ref.shape[0]
    nq = seq // tq
    b = pl.program_id(1)

    @pl.when(b < batch)
    def _():
        km = km_ref[...]
        vm = vm_ref[...]
        mcol = lax.broadcasted_iota(jnp.int32, (tq, BLOCK), 1)
        meta_bias = jnp.where(mcol >= META_LO, 0.0, NEG_INF)
        row = lax.broadcasted_iota(jnp.int32, (tq, tq), 0)
        col = lax.broadcasted_iota(jnp.int32, (tq, tq), 1)
        causal_bias = jnp.where(row >= col, 0.0, NEG_INF)
        for qi in range(nq):
            lo, hi = qi * tq, (qi + 1) * tq
            q = q_ref[lo:hi, :]
            s0 = _dot_nt(q, km) + meta_bias
            sd = _dot_nt(q, k_ref[lo:hi, :]) + causal_bias
            m = jnp.maximum(jnp.max(s0, axis=-1, keepdims=True), jnp.max(sd, axis=-1, keepdims=True))
            if qi > 0:
                sf = _dot_nt(q, k_ref[0:lo, :])
                m = jnp.maximum(m, jnp.max(sf, axis=-1, keepdims=True))
            oe = _dot(jnp.exp2(s0 - m).astype(BF16), vm)
            oe = oe + _dot(jnp.exp2(sd - m).astype(BF16), v_ref[lo:hi, :])
            if qi > 0:
                oe = oe + _dot(jnp.exp2(sf - m).astype(BF16), v_ref[0:lo, :])
            o_ref[lo:hi, :] = _normalise(oe).astype(o_ref.dtype)

    @pl.when(b == batch)
    def _():
        row = lax.broadcasted_iota(jnp.int32, (BLOCK, BLOCK), 0)
        col = lax.broadcasted_iota(jnp.int32, (BLOCK, BLOCK), 1)
        s = _dot_nt(q_ref[0:BLOCK, :], km_ref[...])
        s = jnp.where((col >= META_LO) & (row >= col), s, NEG_INF)
        m = jnp.max(s, axis=-1, keepdims=True)
        oe = _dot(jnp.exp2(s - m).astype(BF16), vm_ref[...])
        o_ref[0:BLOCK, :] = _normalise(oe).astype(o_ref.dtype)


def _mla_branch(proj, lay, cq_g, ckv_g, w_uq, w_ukv, qn_g, kn_g, cos_t, sin_t, batch, seq):
    m = proj.shape[0]
    qr = cq_g.shape[0]
    kvr = ckv_g.shape[0]
    heads = w_uq.shape[1] // MLA_QK
    tm = _pick(m, 640, 128)
    scale = MLA_QK ** -0.5 * LOG2E

    w_uq_h = jnp.pad(w_uq.reshape(qr, heads, MLA_QK), ((0, 0), (0, 0), (0, MLA_QK_PAD - MLA_QK)))
    w_uq_h = jnp.transpose(w_uq_h, (1, 0, 2)).astype(BF16)
    w_ukv_b = w_ukv.astype(BF16)
    qg = (jnp.pad(qn_g, (0, MLA_QK_PAD - MLA_QK)) * scale).reshape(1, MLA_QK_PAD)
    kg = jnp.pad(kn_g, (0, MLA_QK_PAD - MLA_QK)).reshape(1, MLA_QK_PAD)
    v_ext = MLA_V + MLA_V

    q = pl.pallas_call(
        functools.partial(_mla_q_kernel, heads=heads),
        grid=(m // tm,),
        in_specs=[pl.BlockSpec((tm, qr), lambda i: (i, lay["c_q"] // qr)),
                  pl.BlockSpec((1, qr), lambda i: (0, 0)),
                  pl.BlockSpec((heads, qr, MLA_QK_PAD), lambda i: (0, 0, 0)),
                  pl.BlockSpec((1, MLA_QK_PAD), lambda i: (0, 0)),
                  pl.BlockSpec((tm, LANE), lambda i: (i, 0)),
                  pl.BlockSpec((tm, LANE), lambda i: (i, 0))],
        out_specs=pl.BlockSpec((heads, tm, MLA_QK_PAD), lambda i: (0, i, 0)),
        out_shape=jax.ShapeDtypeStruct((heads, m, MLA_QK_PAD), BF16),
        compiler_params=_params(("arbitrary",)),
        name="mla_q",
    )(proj, cq_g.reshape(1, qr), w_uq_h, qg, cos_t, sin_t)

    k, v = pl.pallas_call(
        functools.partial(_mla_kv_kernel, heads=heads),
        grid=(m // tm,),
        in_specs=[pl.BlockSpec((tm, kvr), lambda i: (i, lay["c_kv"] // kvr)),
                  pl.BlockSpec((1, kvr), lambda i: (0, 0)),
                  pl.BlockSpec((kvr, heads * (MLA_NOPE + MLA_V)), lambda i: (0, 0)),
                  pl.BlockSpec((tm, LANE), lambda i: (i, lay["k_pe"] // LANE)),
                  pl.BlockSpec((1, MLA_QK_PAD), lambda i: (0, 0)),
                  pl.BlockSpec((tm, LANE), lambda i: (i, 0)),
                  pl.BlockSpec((tm, LANE), lambda i: (i, 0))],
        out_specs=[pl.BlockSpec((heads, tm, MLA_QK_PAD), lambda i: (0, i, 0)),
                   pl.BlockSpec((heads, tm, v_ext), lambda i: (0, i, 0))],
        out_shape=[jax.ShapeDtypeStruct((heads, m, MLA_QK_PAD), BF16),
                   jax.ShapeDtypeStruct((heads, m, v_ext), BF16)],
        compiler_params=_params(("arbitrary",)),
        name="mla_kv",
    )(proj, ckv_g.reshape(1, kvr), w_ukv_b, proj, kg, cos_t, sin_t)

    meta_blk = m // BLOCK - 1
    tq = _pick(seq, 512, 128)
    a = pl.pallas_call(
        functools.partial(_mla_attn_kernel, tq=tq, batch=batch),
        grid=(heads, batch + 1),
        in_specs=[pl.BlockSpec((None, seq, MLA_QK_PAD), lambda h, b: (h, b, 0)),
                  pl.BlockSpec((None, seq, MLA_QK_PAD), lambda h, b: (h, b, 0)),
                  pl.BlockSpec((None, seq, v_ext), lambda h, b: (h, b, 0)),
                  pl.BlockSpec((None, BLOCK, MLA_QK_PAD), lambda h, b: (h, meta_blk, 0)),
                  pl.BlockSpec((None, BLOCK, v_ext), lambda h, b: (h, meta_blk, 0))],
        out_specs=pl.BlockSpec((seq, MLA_V), lambda h, b: (b, h)),
        out_shape=jax.ShapeDtypeStruct((m, heads * MLA_V), BF16),
        compiler_params=_params(("arbitrary", "arbitrary")),
        name="mla_attn",
    )(q, k, v, k, v)
    return a


def _t5_bucket(rel, buckets):
    n = jnp.maximum(rel, 0)
    max_exact = buckets // 2
    nf = jnp.maximum(n, 1).astype(F32)
    large = max_exact + (jnp.log(nf / max_exact) / math.log(REL_MAX_DIST / max_exact)
                         * (buckets - max_exact)).astype(jnp.int32)
    large = jnp.minimum(large, buckets - 1)
    return jnp.where(n < max_exact, n, large)


def _bias_kernel(rb_ref, bucket_ref, o_ref, *, buckets):
    pr = pl.program_id(0)
    bucket = bucket_ref[...]
    qi = lax.broadcasted_iota(jnp.int32, bucket.shape, 0)
    sj = lax.broadcasted_iota(jnp.int32, bucket.shape, 1)
    band = (sj - qi >= 1) & (sj - qi <= WINDOW)
    halves = []
    for t in range(2):
        acc = jnp.zeros(bucket.shape, F32)
        for b in range(buckets):
            acc = jnp.where(bucket == b, rb_ref[b, 2 * pr + t], acc)
        halves.append(jnp.where(band, acc * LOG2E, NEG_INF))
    o_ref[...] = jnp.concatenate(halves, axis=1)


def _bias_table(rel_bias):
    buckets, hq = rel_bias.shape
    qi = jnp.arange(BLOCK)[:, None]
    sj = jnp.arange(2 * BLOCK)[None, :]
    bucket = _t5_bucket(qi + BLOCK - sj, buckets).astype(jnp.int32)
    return pl.pallas_call(
        functools.partial(_bias_kernel, buckets=buckets),
        grid=(hq // 2,),
        in_specs=[pl.BlockSpec(memory_space=pltpu.SMEM),
                  pl.BlockSpec((BLOCK, 2 * BLOCK), lambda h: (0, 0))],
        out_specs=pl.BlockSpec((None, BLOCK, 4 * BLOCK), lambda h: (h, 0, 0)),
        out_shape=jax.ShapeDtypeStruct((hq // 2, BLOCK, 4 * BLOCK), F32),
        compiler_params=_params(("arbitrary",)),
        name="rel_bias_table",
    )(rel_bias, bucket)


def _seg_rms(x, seg_ref, segt_ref, gain):
    ss = _dot((x * x).astype(BF16), seg_ref[...])
    r = lax.rsqrt(ss * (1.0 / SWA_HEAD_DIM) + EPS)
    return x * _dot(r.astype(BF16), segt_ref[...]) * gain


def _swa_kernel(prev_ref, plo_ref, clo_ref, q_ref, kc_ref, kp_ref, vc_ref, vp_ref, bias_ref, sink_ref,
                qg_ref, kg_ref, segq_ref, segqt_ref, segk_ref, segkt_ref, ones_ref, o_ref, *, hq, hkv):
    del prev_ref
    t = pl.program_id(0)
    plo = plo_ref[t]
    clo = clo_ref[t]
    half = LANE // 2
    pairs_per_group = hq // hkv // 2

    sj = lax.broadcasted_iota(jnp.int32, (BLOCK, 4 * BLOCK), 1) % (2 * BLOCK)
    in_prev = sj < BLOCK
    krow = jnp.where(in_prev, sj, sj - BLOCK)
    key_bias = jnp.where(krow >= jnp.where(in_prev, plo, clo), 0.0, NEG_INF)

    qn = _seg_rms(q_ref[...].astype(F32), segq_ref, segqt_ref, qg_ref[...]).astype(BF16)
    kf = jnp.concatenate([kp_ref[...], kc_ref[...]], axis=0).astype(F32)
    kn = _seg_rms(kf, segk_ref, segkt_ref, kg_ref[...])
    vf = jnp.concatenate([vp_ref[...], vc_ref[...]], axis=0).astype(F32)

    lane2 = lax.broadcasted_iota(jnp.int32, (2 * BLOCK, LANE), 1)
    lane1 = lax.broadcasted_iota(jnp.int32, (BLOCK, LANE), 1)

    def block_diag(col, odd):
        other = pltpu.roll(col, half, 1)
        on_lo, on_hi = (other, col) if odd else (col, other)
        return jnp.concatenate([jnp.where(lane2 < half, on_lo, 0.0), jnp.where(lane2 < half, 0.0, on_hi)],
                               axis=0).astype(BF16)

    outs = []
    for g in range(hkv):
        c = g // 2
        kbd = block_diag(kn[:, c * LANE:(c + 1) * LANE], g % 2)
        vext = jnp.concatenate([block_diag(vf[:, c * LANE:(c + 1) * LANE], g % 2), ones_ref[...]], axis=1)
        for jj in range(pairs_per_group):
            pr = g * pairs_per_group + jj
            s = _dot_nt(qn[:, pr * LANE:(pr + 1) * LANE], kbd) + bias_ref[pr] + key_bias
            sa, sb = s[:, :2 * BLOCK], s[:, 2 * BLOCK:]
            ma = jnp.maximum(jnp.max(sa, axis=-1, keepdims=True), sink_ref[2 * pr])
            mb = jnp.maximum(jnp.max(sb, axis=-1, keepdims=True), sink_ref[2 * pr + 1])
            e = jnp.concatenate([jnp.exp2(sa - ma), jnp.exp2(sb - mb)], axis=1).astype(BF16)
            oe = _dot(e, vext)
            sink_term = jnp.where(lane1 < half, jnp.exp2(sink_ref[2 * pr] - ma),
                                  jnp.exp2(sink_ref[2 * pr + 1] - mb))
            outs.append(oe[:, :LANE] / (oe[:, LANE:] + sink_term))
    o_ref[...] = jnp.concatenate(outs, axis=-1).astype(o_ref.dtype)


def _swa_branch(proj, lay, bias_tab, sinks, qg, kg, batch, seq):
    m = proj.shape[0]
    hq = sinks.shape[0]
    hd = SWA_HEAD_DIM
    qw = hq * hd
    kw = lay["v_s"] - lay["k_s"]
    hkv = kw // hd
    nblk = m // BLOCK
    per = seq // BLOCK
    t = jnp.arange(nblk, dtype=jnp.int32)
    is_meta = t == nblk - 1
    first = (t % per) == 0
    prev = jnp.where(is_meta | first, nblk - 1, t - 1).astype(jnp.int32)
    plo = jnp.where(is_meta, BLOCK, jnp.where(first, META_LO, 0)).astype(jnp.int32)
    clo = jnp.where(is_meta, META_LO, 0).astype(jnp.int32)
    assert (hq // hkv) % 2 == 0 and kw % LANE == 0 and hq <= LANE

    def seg(width):
        return (jnp.arange(width)[:, None] // hd == jnp.arange(LANE)[None, :]).astype(BF16)

    segq, segk = seg(qw), seg(kw)
    qg_t = (jnp.tile(qg, hq) * (hd ** -0.5 * LOG2E)).reshape(1, qw)
    kg_t = jnp.tile(kg, hkv).reshape(1, kw)
    ones_bd = ((jnp.arange(4 * BLOCK)[:, None] < 2 * BLOCK)
               == (jnp.arange(LANE)[None, :] < LANE // 2)).astype(BF16)

    def const(shape):
        return pl.BlockSpec(shape, lambda i, pr, pl_, cl: (0,) * len(shape))

    grid_spec = pltpu.PrefetchScalarGridSpec(
        num_scalar_prefetch=3,
        grid=(nblk,),
        in_specs=[pl.BlockSpec((BLOCK, qw), lambda i, pr, pl_, cl: (i, lay["q_s"] // qw)),
                  pl.BlockSpec((BLOCK, kw), lambda i, pr, pl_, cl: (i, lay["k_s"] // kw)),
                  pl.BlockSpec((BLOCK, kw), lambda i, pr, pl_, cl: (pr[i], lay["k_s"] // kw)),
                  pl.BlockSpec((BLOCK, kw), lambda i, pr, pl_, cl: (i, lay["v_s"] // kw)),
                  pl.BlockSpec((BLOCK, kw), lambda i, pr, pl_, cl: (pr[i], lay["v_s"] // kw)),
                  const((hq // 2, BLOCK, 4 * BLOCK)),
                  pl.BlockSpec(memory_space=pltpu.SMEM),
                  const((1, qw)), const((1, kw)),
                  const((qw, LANE)), const((LANE, qw)), const((kw, LANE)), const((LANE, kw)),
                  const((4 * BLOCK, LANE))],
        out_specs=pl.BlockSpec((BLOCK, qw), lambda i, pr, pl_, cl: (i, 0)),
    )
    return pl.pallas_call(
        functools.partial(_swa_kernel, hq=hq, hkv=hkv),
        grid_spec=grid_spec,
        out_shape=jax.ShapeDtypeStruct((m, qw), BF16),
        compiler_params=_params(("arbitrary",)),
        name="swa_attn",
    )(prev, plo, clo, proj, proj, proj, proj, proj, bias_tab, sinks * LOG2E, qg_t, kg_t,
      segq, segq.T, segk, segk.T, ones_bd)


def _merge_kernel(a_ref, b_ref, wa_ref, wb_ref, ga_ref, gb_ref, o_ref, wab_ref, wbb_ref):
    @pl.when(pl.program_id(1) == 0)
    def _():
        _cast_weight(wa_ref, wab_ref)
        _cast_weight(wb_ref, wbb_ref)

    ya = _dot(a_ref[...], wab_ref[...])
    yb = _dot(b_ref[...], wbb_ref[...])
    ga = _sigmoid(ga_ref[...].astype(F32))
    gb = _sigmoid(gb_ref[...].astype(F32))
    o_ref[...] = (ga * ya + gb * yb).astype(o_ref.dtype)


def _merge(a, b, w_a, w_b, layer, proj, lay, tn):
    m, ka = a.shape
    kb = b.shape[1]
    d = w_a.shape[-1]
    tm = _pick(m, 640, 128)
    ga0 = lay["g_a"] // tn
    gb0 = lay["g_b"] // tn
    return pl.pallas_call(
        _merge_kernel,
        grid=(d // tn, m // tm),
        in_specs=[pl.BlockSpec((tm, ka), lambda j, i: (i, 0)),
                  pl.BlockSpec((tm, kb), lambda j, i: (i, 0)),
                  pl.BlockSpec((None, ka, tn), lambda j, i: (layer, 0, j)),
                  pl.BlockSpec((None, kb, tn), lambda j, i: (layer, 0, j)),
                  pl.BlockSpec((tm, tn), lambda j, i: (i, ga0 + j)),
                  pl.BlockSpec((tm, tn), lambda j, i: (i, gb0 + j))],
        out_specs=pl.BlockSpec((tm, tn), lambda j, i: (i, j)),
        out_shape=jax.ShapeDtypeStruct((m, d), BF16),
        scratch_shapes=[pltpu.VMEM((ka, tn), BF16), pltpu.VMEM((kb, tn), BF16)],
        compiler_params=_params(("arbitrary", "arbitrary")),
        name="branch_merge",
    )(a, b, w_a, w_b, proj, proj)


def _ffn_up_kernel(x_ref, w1_ref, w3_ref, o_ref, w1b_ref, w3b_ref):
    @pl.when(pl.program_id(1) == 0)
    def _():
        _cast_weight(w1_ref, w1b_ref)
        _cast_weight(w3_ref, w3b_ref)

    x = x_ref[...]
    u = _dot(x, w1b_ref[...])
    g = _dot(x, w3b_ref[...])
    o_ref[...] = (u * _sigmoid(u) * g).astype(o_ref.dtype)


def _ffn_up(hn, w1, w3, idx):
    m, d = hn.shape
    ff = w1.shape[-1]
    tn = _pick(ff, 256, 128)
    tm = _pick(m, 1664, 128)
    return pl.pallas_call(
        _ffn_up_kernel,
        grid=(ff // tn, m // tm),
        in_specs=[pl.BlockSpec((tm, d), lambda j, i: (i, 0)),
                  pl.BlockSpec((None, d, tn), lambda j, i: (idx, 0, j)),
                  pl.BlockSpec((None, d, tn), lambda j, i: (idx, 0, j))],
        out_specs=pl.BlockSpec((tm, tn), lambda j, i: (i, j)),
        out_shape=jax.ShapeDtypeStruct((m, ff), BF16),
        scratch_shapes=[pltpu.VMEM((d, tn), BF16), pltpu.VMEM((d, tn), BF16)],
        compiler_params=_params(("arbitrary", "arbitrary"), VMEM_LIMIT_BIG),
        name="ffn_up",
    )(hn, w1, w3)


def _dense_ffn(h_res, gain, w1, w3, w2, idx):
    hn = _rms_norm(h_res, gain)
    t = _ffn_up(hn, w1, w3, idx)
    ff = w1.shape[-1]
    nk = 1
    for cand in (1, 2, 4):
        if ff % cand == 0 and (ff // cand) % LANE == 0 and ff // cand <= 6144:
            nk = cand
            break
    kc = ff // nk
    out = h_res
    for kblk in range(nk):
        out = _mm_res(t, w2, (idx,), kblk, kc, out, "ffn_down")
    return out


def _router_kernel(x_ref, g_ref, w_ref, o_ref, *, n_exp):
    x = x_ref[...]
    ms = jnp.mean(x * x, axis=-1, keepdims=True)
    xn = (x * lax.rsqrt(ms + EPS) * g_ref[...]).astype(BF16)
    logits = _dot(xn, w_ref[...].astype(BF16))
    lane = lax.broadcasted_iota(jnp.int32, logits.shape, 1)
    logits = jnp.where(lane < n_exp, logits, -jnp.inf)
    lane_f = lane.astype(F32)
    m1 = jnp.max(logits, axis=-1, keepdims=True)
    i1 = jnp.min(jnp.where(logits == m1, lane_f, float(LANE)), axis=-1, keepdims=True)
    rest = jnp.where(lane_f == i1, -jnp.inf, logits)
    m2 = jnp.max(rest, axis=-1, keepdims=True)
    i2 = jnp.min(jnp.where(rest == m2, lane_f, float(LANE)), axis=-1, keepdims=True)
    e2 = jnp.exp(m2 - m1)
    w1 = 1.0 / (1.0 + e2)
    w2 = e2 / (1.0 + e2)
    out = jnp.where(lane == 0, i1, jnp.where(lane == 1, i2,
                                             jnp.where(lane == 2, w1, jnp.where(lane == 3, w2, 0.0))))
    o_ref[...] = out


def _router(h_res, gain, router_w):
    m, d = h_res.shape
    n_exp = router_w.shape[-1]
    tm = _pick(m, 640, 128)
    wpad = jnp.pad(router_w, ((0, 0), (0, LANE - n_exp)))
    return pl.pallas_call(
        functools.partial(_router_kernel, n_exp=n_exp),
        grid=(m // tm,),
        in_specs=[pl.BlockSpec((tm, d), lambda i: (i, 0)),
                  pl.BlockSpec((1, d), lambda i: (0, 0)),
                  pl.BlockSpec((d, LANE), lambda i: (0, 0))],
        out_specs=pl.BlockSpec((tm, LANE), lambda i: (i, 0)),
        out_shape=jax.ShapeDtypeStruct((m, LANE), F32),
        compiler_params=_params(("arbitrary",)),
        name="moe_router",
    )(h_res, gain.reshape(1, d), wpad)


def _row_copy(src_hbm, buf, sem, slot, src_row, dst_row):
    return pltpu.make_async_copy(src_hbm.at[pl.ds(src_row, 1)], buf.at[slot, pl.ds(dst_row, 1)],
                                 sem.at[slot])


def _start_row_gather(src_hbm, buf, sem, slot, idx_ref, base, count):
    def body(r2, c):
        for p in range(2):
            r = 2 * r2 + p
            _row_copy(src_hbm, buf, sem, slot, idx_ref[base + r], r).start(priority=p)
        return c
    lax.fori_loop(0, count // 2, body, 0)


def _wait_row_gather(src_hbm, buf, sem, slot, count):
    def body(r, c):
        _row_copy(src_hbm, buf, sem, slot, 0, r).wait()
        return c
    lax.fori_loop(0, count, body, 0)


def _gather_norm_kernel(tok_ref, live_ref, h_hbm, g_ref, o_ref, buf, sem, *, rows):
    s = pl.program_id(0)
    n = pl.num_programs(0)

    @pl.when((s == 0) & (live_ref[0] == 1))
    def _():
        _start_row_gather(h_hbm, buf, sem, 0, tok_ref, 0, rows)

    nxt = jnp.minimum(s + 1, n - 1)

    @pl.when((s + 1 < n) & (live_ref[nxt] == 1))
    def _():
        _start_row_gather(h_hbm, buf, sem, (s + 1) % 2, tok_ref, (s + 1) * rows, rows)

    slot = s % 2

    @pl.when(live_ref[s] == 1)
    def _():
        _wait_row_gather(h_hbm, buf, sem, slot, rows)
        x = buf[slot]
        ms = jnp.mean(x * x, axis=-1, keepdims=True)
        o_ref[...] = (x * lax.rsqrt(ms + EPS) * g_ref[...]).astype(o_ref.dtype)

    @pl.when(live_ref[s] == 0)
    def _():
        o_ref[...] = jnp.zeros(o_ref.shape, o_ref.dtype)


def _gather_norm(h_res, gain, row_tok, live, rows):
    d = h_res.shape[1]
    rp = row_tok.shape[0]
    grid_spec = pltpu.PrefetchScalarGridSpec(
        num_scalar_prefetch=2,
        grid=(rp // rows,),
        in_specs=[pl.BlockSpec(memory_space=pl.ANY),
                  pl.BlockSpec((1, d), lambda i, tok, lv: (0, 0))],
        out_specs=pl.BlockSpec((rows, d), lambda i, tok, lv: (i, 0)),
        scratch_shapes=[pltpu.VMEM((2, rows, d), F32), pltpu.SemaphoreType.DMA((2,))],
    )
    return pl.pallas_call(
        functools.partial(_gather_norm_kernel, rows=rows),
        grid_spec=grid_spec,
        out_shape=jax.ShapeDtypeStruct((rp, d), BF16),
        compiler_params=_params(("arbitrary",)),
        name="moe_gather_norm",
    )(row_tok, live, h_res, gain.reshape(1, d))


def _for_live_subtiles(valid_ref, o_ref, sub, compute):
    i = pl.program_id(1)
    for s in range(o_ref.shape[0] // sub):
        rows = slice(s * sub, (s + 1) * sub)

        @pl.when(valid_ref[i] > s)
        def _(rows=rows):
            o_ref[rows, :] = compute(rows)

        @pl.when(valid_ref[i] <= s)
        def _(rows=rows):
            o_ref[rows, :] = jnp.zeros((sub, o_ref.shape[1]), o_ref.dtype)


def _expert_weights(sched, w_hbms, wf_ref, wb_ref, sem, *, layer, tn):
    be_ref, first_ref, nf_ref, ep_ref, nc_ref = sched
    j = pl.program_id(0)
    i = pl.program_id(1)
    nj = pl.num_programs(0)

    def copy(which, e, jj, slot):
        src = w_hbms[which].at[layer, e, :, pl.ds(pl.multiple_of(jj * tn, tn), tn)]
        return pltpu.make_async_copy(src, wf_ref.at[slot, which], sem.at[slot, which])

    @pl.when(first_ref[i] == 1)
    def _():
        epoch = j * nc_ref[0] + ep_ref[i]
        slot = epoch % 2

        @pl.when(epoch == 0)
        def _():
            for which in range(len(w_hbms)):
                copy(which, be_ref[0], 0, 0).start()

        for which in range(len(w_hbms)):
            copy(which, 0, 0, slot).wait()

        more_here = nf_ref[i] >= 0
        nxt_e = jnp.where(more_here, be_ref[jnp.maximum(nf_ref[i], 0)], be_ref[0])
        nxt_j = jnp.where(more_here, j, j + 1)

        @pl.when(more_here | (j + 1 < nj))
        def _():
            for which in range(len(w_hbms)):
                copy(which, nxt_e, nxt_j, 1 - slot).start()

        for which in range(len(w_hbms)):
            _cast_weight(wf_ref.at[slot, which], wb_ref.at[which])


def _moe_up_kernel(be_ref, first_ref, nf_ref, ep_ref, nc_ref, valid_ref, xb_ref, x_ref, w1_hbm, w3_hbm,
                   o_ref, wf_ref, wb_ref, sem, *, sub, layer, tn):
    _expert_weights((be_ref, first_ref, nf_ref, ep_ref, nc_ref), (w1_hbm, w3_hbm), wf_ref, wb_ref, sem,
                    layer=layer, tn=tn)

    def compute(rows):
        x = x_ref[rows, :]
        u = _dot(x, wb_ref[0])
        g = _dot(x, wb_ref[1])
        return (u * _sigmoid(u) * g).astype(o_ref.dtype)

    _for_live_subtiles(valid_ref, o_ref, sub, compute)


def _moe_down_kernel(be_ref, first_ref, nf_ref, ep_ref, nc_ref, valid_ref, xb_ref, x_ref, w_hbm,
                     o_ref, wf_ref, wb_ref, sem, *, sub, layer, tn):
    _expert_weights((be_ref, first_ref, nf_ref, ep_ref, nc_ref), (w_hbm,), wf_ref, wb_ref, sem,
                    layer=layer, tn=tn)
    _for_live_subtiles(valid_ref, o_ref, sub, lambda rows: _dot(x_ref[rows, :], wb_ref[0]))


def _combine_kernel(pos_ref, h_ref, rt_ref, y_hbm, o_ref, buf, sem, *, rows):
    s = pl.program_id(0)
    n = pl.num_programs(0)

    @pl.when(s == 0)
    def _():
        _start_row_gather(y_hbm, buf, sem, 0, pos_ref, 0, 2 * rows)

    @pl.when(s + 1 < n)
    def _():
        _start_row_gather(y_hbm, buf, sem, (s + 1) % 2, pos_ref, (s + 1) * 2 * rows, 2 * rows)

    slot = s % 2
    _wait_row_gather(y_hbm, buf, sem, slot, 2 * rows)

    rt = rt_ref[...]
    y0 = buf[slot, 0:rows, :]
    y1 = buf[slot, rows:2 * rows, :]
    o_ref[...] = h_ref[...] + rt[:, 2:3] * y0 + rt[:, 3:4] * y1


def _moe_ffn(h_res, gain, router_w, w1, w3, w2, idx, n_out):
    m, d = h_res.shape
    n_exp = router_w.shape[-1]
    ff = w1.shape[-1]
    rt = _router(h_res, gain, router_w)

    sub = 256
    tme = 2 * sub
    grows = 128
    r_tot = TOP_K * m
    nb = -(-(r_tot + n_exp * (tme - 1)) // tme)
    rp = nb * tme
    e = rt[:, :TOP_K].astype(jnp.int32).reshape(-1)
    onehot = (e[:, None] == jnp.arange(n_exp, dtype=jnp.int32)[None, :]).astype(jnp.int32)
    csum = jnp.cumsum(onehot, axis=0)
    rank = jnp.sum(onehot * csum, axis=1) - 1
    counts = csum[-1]
    padded = (counts + tme - 1) // tme * tme
    gend = jnp.cumsum(padded)
    goff = gend - padded
    pos = (jnp.sum(onehot * goff[None, :], axis=1) + rank).astype(jnp.int32)
    row_tok = jnp.zeros((rp,), jnp.int32).at[pos].set(jnp.arange(r_tot, dtype=jnp.int32) // TOP_K)
    used = gend[-1] // tme
    blk = jnp.arange(nb, dtype=jnp.int32)
    xb = jnp.minimum(blk, used - 1).astype(jnp.int32)
    be = jnp.minimum(jnp.sum((xb[:, None] * tme >= gend[None, :]).astype(jnp.int32), axis=1),
                     n_exp - 1).astype(jnp.int32)
    first = jnp.concatenate([jnp.ones((1,), jnp.int32), (be[1:] != be[:-1]).astype(jnp.int32)])
    nxt_change = lax.cummin(jnp.where(first == 1, blk, nb)[::-1])[::-1]
    nf = jnp.concatenate([nxt_change[1:], jnp.full((1,), nb, jnp.int32)])
    nf = jnp.where(nf >= nb, -1, nf).astype(jnp.int32)
    ep = (jnp.cumsum(first) - 1).astype(jnp.int32)
    nc = jnp.sum(first).astype(jnp.int32).reshape(1)

    def live_tiles(tile, per_block):
        start = jnp.arange(rp // tile, dtype=jnp.int32) * tile
        e_of = jnp.repeat(be, tme // tile)
        real_end = jnp.where(jnp.repeat(blk < used, tme // tile), (goff + counts)[e_of], 0)
        live = (start < real_end).astype(jnp.int32)
        return live if per_block is None else jnp.sum(live.reshape(nb, per_block), axis=1).astype(jnp.int32)

    valid = live_tiles(sub, tme // sub)
    glive = live_tiles(grows, None)

    xs = _gather_norm(h_res, gain, row_tok, glive, grows)

    sched = (be, first, nf, ep, nc, valid, xb)

    def x_map(j, i, *s):
        return (s[6][i], 0)

    def o_map(j, i, *s):
        return (i, j)

    def staged(n_w, k, tn):
        return [pltpu.VMEM((2, n_w, k, tn), F32), pltpu.VMEM((n_w, k, tn), BF16),
                pltpu.SemaphoreType.DMA((2, n_w))]

    hbm = pl.BlockSpec(memory_space=pl.ANY)
    tn_up = _pick(ff, 512, 128)
    up_spec = pltpu.PrefetchScalarGridSpec(
        num_scalar_prefetch=len(sched),
        grid=(ff // tn_up, nb),
        in_specs=[pl.BlockSpec((tme, d), x_map), hbm, hbm],
        out_specs=pl.BlockSpec((tme, tn_up), o_map),
        scratch_shapes=staged(2, d, tn_up),
    )
    t = pl.pallas_call(
        functools.partial(_moe_up_kernel, sub=sub, layer=idx, tn=tn_up),
        grid_spec=up_spec,
        out_shape=jax.ShapeDtypeStruct((rp, ff), BF16),
        compiler_params=_params(("arbitrary", "arbitrary"), VMEM_LIMIT_BIG),
        name="moe_up",
    )(*sched, xs, w1, w3)

    tn_dn = _pick(d, 512, 128)
    dn_spec = pltpu.PrefetchScalarGridSpec(
        num_scalar_prefetch=len(sched),
        grid=(d // tn_dn, nb),
        in_specs=[pl.BlockSpec((tme, ff), x_map), hbm],
        out_specs=pl.BlockSpec((tme, tn_dn), o_map),
        scratch_shapes=staged(1, ff, tn_dn),
    )
    y = pl.pallas_call(
        functools.partial(_moe_down_kernel, sub=sub, layer=idx, tn=tn_dn),
        grid_spec=dn_spec,
        out_shape=jax.ShapeDtypeStruct((rp, d), F32),
        compiler_params=_params(("arbitrary", "arbitrary"), VMEM_LIMIT_BIG),
        name="moe_down",
    )(*sched, t, w2)

    rows = 128
    cmb_spec = pltpu.PrefetchScalarGridSpec(
        num_scalar_prefetch=1,
        grid=(n_out // rows,),
        in_specs=[pl.BlockSpec((rows, d), lambda i, p: (i, 0)),
                  pl.BlockSpec((rows, LANE), lambda i, p: (i, 0)),
                  pl.BlockSpec(memory_space=pl.ANY)],
        out_specs=pl.BlockSpec((rows, d), lambda i, p: (i, 0)),
        scratch_shapes=[pltpu.VMEM((2, 2 * rows, d), F32), pltpu.SemaphoreType.DMA((2,))],
    )
    pos_blk = jnp.transpose(pos.reshape(m // rows, rows, TOP_K), (0, 2, 1)).reshape(-1)
    return pl.pallas_call(
        functools.partial(_combine_kernel, rows=rows),
        grid_spec=cmb_spec,
        out_shape=jax.ShapeDtypeStruct((n_out, d), F32),
        compiler_params=_params(("arbitrary",)),
        name="moe_combine",
    )(pos_blk, h_res, rt, y)


def _in_layout(d, qr, kvr, qw, kw):
    head = qr + kvr
    rest = qw + 2 * kw + 2 * d
    tn = max(t for t in (512, 256, 128) if head % t == 0 and rest % t == 0)
    lay = {"c_q": 0, "c_kv": qr, "k_pe": head}
    cur = head + tn
    for name, width in (("q_s", qw), ("k_s", kw), ("v_s", kw), ("g_a", d), ("g_b", d)):
        lay[name] = cur
        cur += width
    lay["total"] = cur
    assert lay["c_kv"] % kvr == 0 and lay["q_s"] % qw == 0 and lay["k_s"] % kw == 0 and lay["v_s"] % kw == 0
    assert lay["g_a"] % tn == 0 and lay["g_b"] % tn == 0 and d % tn == 0
    return lay, tn


def kernel(x, meta_tokens, rel_bias, attn_norm, w_in, mla_cq_norm, mla_ckv_norm, mla_w_uq, mla_w_ukv,
           mla_q_norm, mla_k_norm, swa_q_norm, swa_k_norm, swa_sinks, w_branch_mla, w_branch_swa, w_out,
           ffn_norm, dense_w1, dense_w3, dense_w2, moe_router, moe_w1, moe_w3, moe_w2):
    batch, seq, d = x.shape
    depth = w_in.shape[0]
    qr = mla_cq_norm.shape[1]
    kvr = mla_ckv_norm.shape[1]
    hq = swa_sinks.shape[1]
    qw = hq * SWA_HEAD_DIM
    kw = (w_in.shape[2] - qr - kvr - MLA_ROPE - qw - 2 * d) // 2
    assert seq % BLOCK == 0 and meta_tokens.shape[0] == N_META
    n_tok = batch * seq
    m = n_tok + BLOCK

    lay, tn_in = _in_layout(d, qr, kvr, qw, kw)

    h_res = jnp.concatenate([x.reshape(n_tok, d), jnp.zeros((META_LO, d), x.dtype),
                             meta_tokens.astype(x.dtype)], axis=0)

    pos = jnp.concatenate([jnp.tile(N_META + jnp.arange(seq), batch), jnp.arange(BLOCK) - META_LO])
    half = MLA_ROPE // 2
    inv_freq = ROPE_THETA ** (-jnp.arange(half, dtype=F32) / half)
    ang = pos.astype(F32)[:, None] * inv_freq[None, :]
    cos, sin = jnp.cos(ang), jnp.sin(ang)
    cos_t = jnp.concatenate([cos, cos, jnp.ones((m, LANE - MLA_ROPE), F32)], axis=1)
    sin_t = jnp.concatenate([-sin, sin, jnp.zeros((m, LANE - MLA_ROPE), F32)], axis=1)

    bias_tab = _bias_table(rel_bias)
    w_in_t = jnp.swapaxes(w_in, 1, 2)

    for i in range(depth):
        hn = _rms_norm(h_res, attn_norm[i])
        proj = _in_proj(hn, w_in_t, i, lay, tn_in)
        a = _mla_branch(proj, lay, mla_cq_norm[i], mla_ckv_norm[i], mla_w_uq[i], mla_w_ukv[i],
                        mla_q_norm[i], mla_k_norm[i], cos_t, sin_t, batch, seq)
        b = _swa_branch(proj, lay, bias_tab, swa_sinks[i], swa_q_norm[i], swa_k_norm[i], batch, seq)
        merged = _merge(a, b, w_branch_mla, w_branch_swa, i, proj, lay, tn_in)
        h_res = _mm_res(merged, w_out, (i,), 0, d, h_res, "out_proj")
        last = i == depth - 1
        if i % 2 == 0:
            h_res = _dense_ffn(h_res, ffn_norm[i], dense_w1, dense_w3, dense_w2, i // 2)
            if last:
                h_res = h_res[:n_tok]
        else:
            h_res = _moe_ffn(h_res, ffn_norm[i], moe_router[i // 2], moe_w1, moe_w3, moe_w2, i // 2,
                             n_tok if last else m)
    return h_res.reshape(batch, seq, d)
```

```python
import functools
import math

import jax
import jax.numpy as jnp
from jax import lax
from jax.experimental import pallas as pl
from jax.experimental.pallas import tpu as pltpu

F32 = jnp.float32
BF16 = jnp.bfloat16

N_META = 16
BLOCK = 128
WINDOW = 128
MLA_NOPE = 128
MLA_ROPE = 64
MLA_V = 128
MLA_QK = MLA_NOPE + MLA_ROPE
MLA_QK_PAD = 256
ROPE_THETA = 10000.0
SWA_HEAD_DIM = 64
REL_MAX_DIST = 128
TOP_K = 2
EPS = 1e-6
NEG_INF = -1e30
LOG2E = 1.4426950408889634
LANE = 128
META_LO = BLOCK - N_META

VMEM_LIMIT_BIG = 58 * 1024 * 1024
VMEM_LIMIT_MID = 44 * 1024 * 1024


def _params(sem, vmem=VMEM_LIMIT_MID):
    return pltpu.CompilerParams(dimension_semantics=sem, vmem_limit_bytes=vmem)


def _pick(n, target, mult):
    best = None
    d = mult
    while d <= min(n, target):
        if n % d == 0:
            best = d
        d += mult
    return best if best is not None else n


def _round_up(a, b):
    return (a + b - 1) // b * b


def _cast_weight(w_ref, wb_ref):
    k = w_ref.shape[0]
    ch = 512 if k % 512 == 0 else (256 if k % 256 == 0 else 128)
    if k % ch != 0:
        wb_ref[...] = w_ref[...].astype(BF16)
        return

    def body(c, carry):
        r = pl.multiple_of(c * ch, ch)
        wb_ref[pl.ds(r, ch), :] = w_ref[pl.ds(r, ch), :].astype(BF16)
        return carry

    lax.fori_loop(0, k // ch, body, 0)


def _sigmoid(x):
    return 1.0 / (1.0 + jnp.exp(-x))


def _dot(a, b):
    return jnp.dot(a, b, preferred_element_type=F32)


def _dot_nt(a, b):
    return lax.dot_general(a, b, (((1,), (1,)), ((), ())), preferred_element_type=F32)


def _rms_kernel(x_ref, g_ref, o_ref):
    x = x_ref[...]
    ms = jnp.mean(x * x, axis=-1, keepdims=True)
    o_ref[...] = (x * lax.rsqrt(ms + EPS) * g_ref[...]).astype(o_ref.dtype)


def _rms_norm(h, gain):
    m, d = h.shape
    tm = _pick(m, 640, 128)
    return pl.pallas_call(
        _rms_kernel,
        grid=(m // tm,),
        in_specs=[pl.BlockSpec((tm, d), lambda i: (i, 0)),
                  pl.BlockSpec((1, d), lambda i: (0, 0))],
        out_specs=pl.BlockSpec((tm, d), lambda i: (i, 0)),
        out_shape=jax.ShapeDtypeStruct((m, d), BF16),
        compiler_params=_params(("arbitrary",)),
        name="rms_norm",
    )(h, gain.reshape(1, d))


def _in_proj_kernel(x_ref, a_ref, b_ref, o_ref, wb_ref, *, nh):
    j = pl.program_id(0)
    tn, k = a_ref.shape
    half = LANE // 2

    @pl.when(pl.program_id(1) == 0)
    def _():
        @pl.when(j < nh)
        def _():
            wb_ref[...] = a_ref[...].astype(BF16)

        @pl.when(j == nh)
        def _():
            wb_ref[0:half, :] = a_ref[0:half, :].astype(BF16)
            wb_ref[half:tn, :] = jnp.zeros((tn - half, k), BF16)

        @pl.when(j > nh)
        def _():
            wb_ref[0:tn - half, :] = a_ref[half:tn, :].astype(BF16)
            wb_ref[tn - half:tn, :] = b_ref[0:half, :].astype(BF16)

    o_ref[...] = _dot_nt(x_ref[...], wb_ref[...]).astype(o_ref.dtype)


def _in_proj(hn, w_in_t, layer, lay, tn):
    m, k = hn.shape
    nh = lay["k_pe"] // tn
    nblk = lay["total"] // tn
    tm = _pick(m, 1664, 128)
    sub = tn // LANE
    last_b = (w_in_t.shape[1] - 1) // LANE

    def a_map(j, i):
        return (layer, jnp.where(j > nh, j - 1, j), 0)

    def b_map(j, i):
        return (layer, jnp.minimum(jnp.maximum(j, nh + 1) * sub, last_b), 0)

    return pl.pallas_call(
        functools.partial(_in_proj_kernel, nh=nh),
        grid=(nblk, m // tm),
        in_specs=[pl.BlockSpec((tm, k), lambda j, i: (i, 0)),
                  pl.BlockSpec((None, tn, k), a_map),
                  pl.BlockSpec((None, LANE, k), b_map)],
        out_specs=pl.BlockSpec((tm, tn), lambda j, i: (i, j)),
        out_shape=jax.ShapeDtypeStruct((m, lay["total"]), BF16),
        scratch_shapes=[pltpu.VMEM((tn, k), BF16)],
        compiler_params=_params(("arbitrary", "arbitrary"), VMEM_LIMIT_BIG),
        name="in_proj",
    )(hn, w_in_t, w_in_t)


def _resident_weights(windows, wf_ref, wb_ref, sem):
    j = pl.program_id(0)
    nj = pl.num_programs(0)
    n_w = len(windows)

    def copy(which, jj):
        return pltpu.make_async_copy(windows[which](jj), wf_ref.at[which], sem.at[which])

    @pl.when(pl.program_id(1) == 0)
    def _():
        @pl.when(j == 0)
        def _():
            for which in range(n_w):
                copy(which, 0).start()

        for which in range(n_w):
            copy(which, 0).wait()
            _cast_weight(wf_ref.at[which], wb_ref.at[which])

        @pl.when(j + 1 < nj)
        def _():
            for which in range(n_w):
                copy(which, j + 1).start()


def _staged(n_w, k, tn):
    return [pltpu.VMEM((n_w, k, tn), F32), pltpu.VMEM((n_w, k, tn), BF16), pltpu.SemaphoreType.DMA((n_w,))]


def _col_window(w_hbm, lead, k0, kc, tn):
    return lambda jj: w_hbm.at[(*lead, pl.ds(k0, kc), pl.ds(pl.multiple_of(jj * tn, tn), tn))]


def _mm_res_kernel(x_ref, w_hbm, r_ref, o_ref, wf_ref, wb_ref, sem, *, lead, k0, kc, tn):
    _resident_weights([_col_window(w_hbm, lead, k0, kc, tn)], wf_ref, wb_ref, sem)
    o_ref[...] = r_ref[...] + _dot(x_ref[...], wb_ref[0])


def _mm_res(x, w, lead, kblk, kc, res, tm_target, name):
    m = x.shape[0]
    n = w.shape[-1]
    tn = _pick(n, 512, 128)
    tm = _pick(m, tm_target, 128)
    return pl.pallas_call(
        functools.partial(_mm_res_kernel, lead=lead, k0=kblk * kc, kc=kc, tn=tn),
        grid=(n // tn, m // tm),
        in_specs=[pl.BlockSpec((tm, kc), lambda j, i: (i, kblk)),
                  pl.BlockSpec(memory_space=pl.ANY),
                  pl.BlockSpec((tm, tn), lambda j, i: (i, j))],
        out_specs=pl.BlockSpec((tm, tn), lambda j, i: (i, j)),
        out_shape=jax.ShapeDtypeStruct((m, n), F32),
        scratch_shapes=_staged(1, kc, tn),
        compiler_params=_params(("arbitrary", "arbitrary"), VMEM_LIMIT_BIG),
        name=name,
    )(x, w, res)


def _rope(hi, cos, sin):
    lane = lax.broadcasted_iota(jnp.int32, hi.shape, 1)
    half = MLA_ROPE // 2
    rot = jnp.where(lane < half, pltpu.roll(hi, LANE - half, 1), pltpu.roll(hi, half, 1))
    return hi * cos + rot * sin


def _row_rms(x_ref, g_ref):
    x = x_ref[...].astype(F32)
    ms = jnp.mean(x * x, axis=-1, keepdims=True)
    return (x * lax.rsqrt(ms + EPS) * g_ref[...]).astype(BF16)


def _mla_q_kernel(x_ref, g_ref, w_ref, qg_ref, cos_ref, sin_ref, o_ref, *, heads):
    xn = _row_rms(x_ref, g_ref)
    qg = qg_ref[...]
    cos = cos_ref[...]
    sin = sin_ref[...]
    for h in range(heads):
        q = _dot(xn, w_ref[h])
        ss = jnp.sum(q * q, axis=-1, keepdims=True)
        qn = q * lax.rsqrt(ss * (1.0 / MLA_QK) + EPS) * qg
        hi = _rope(qn[:, LANE:], cos, sin)
        o_ref[h] = jnp.concatenate([qn[:, :LANE], hi], axis=-1).astype(o_ref.dtype)


def _mla_kv_kernel(x_ref, g_ref, w_ref, pe_ref, kg_ref, cos_ref, sin_ref, k_ref, v_ref, *, heads):
    xn = _row_rms(x_ref, g_ref)
    kg = kg_ref[...]
    pe = pe_ref[...].astype(F32)
    pe_ss = jnp.sum(pe * pe, axis=-1, keepdims=True)
    pe_rot = _rope(pe * kg[:, LANE:], cos_ref[...], sin_ref[...])
    ones = jnp.ones((xn.shape[0], MLA_V), v_ref.dtype)
    width = MLA_NOPE + MLA_V
    for h in range(heads):
        kv = _dot(xn, w_ref[:, h * width:(h + 1) * width])
        kn = kv[:, :MLA_NOPE]
        r = lax.rsqrt((jnp.sum(kn * kn, axis=-1, keepdims=True) + pe_ss) * (1.0 / MLA_QK) + EPS)
        k_ref[h] = jnp.concatenate([kn * r * kg[:, :LANE], pe_rot * r], axis=-1).astype(k_ref.dtype)
        v_ref[h] = jnp.concatenate([kv[:, MLA_NOPE:].astype(v_ref.dtype), ones], axis=-1)


def _normalise(oe):
    return oe[:, :MLA_V] / oe[:, MLA_V:]


def _mla_attn_kernel(q_ref, k_ref, v_ref, km_ref, vm_ref, o_ref, *, tq, batch):
    seq = q_ref.shape[0]
    nq = seq // tq
    b = pl.program_id(1)

    @pl.when(b < batch)
    def _():
        km = km_ref[...]
        vm = vm_ref[...]
        mcol = lax.broadcasted_iota(jnp.int32, (tq, BLOCK), 1)
        meta_bias = jnp.where(mcol >= META_LO, 0.0, NEG_INF)
        row = lax.broadcasted_iota(jnp.int32, (tq, tq), 0)
        col = lax.broadcasted_iota(jnp.int32, (tq, tq), 1)
        causal_bias = jnp.where(row >= col, 0.0, NEG_INF)
        for qi in range(nq):
            lo, hi = qi * tq, (qi + 1) * tq
            q = q_ref[lo:hi, :]
            s0 = _dot_nt(q, km) + meta_bias
            sd = _dot_nt(q, k_ref[lo:hi, :]) + causal_bias
            m = jnp.maximum(jnp.max(s0, axis=-1, keepdims=True), jnp.max(sd, axis=-1, keepdims=True))
            if qi > 0:
                sf = _dot_nt(q, k_ref[0:lo, :])
                m = jnp.maximum(m, jnp.max(sf, axis=-1, keepdims=True))
            oe = _dot(jnp.exp2(s0 - m).astype(BF16), vm)
            oe = oe + _dot(jnp.exp2(sd - m).astype(BF16), v_ref[lo:hi, :])
            if qi > 0:
                oe = oe + _dot(jnp.exp2(sf - m).astype(BF16), v_ref[0:lo, :])
            o_ref[lo:hi, :] = _normalise(oe).astype(o_ref.dtype)

    @pl.when(b == batch)
    def _():
        row = lax.broadcasted_iota(jnp.int32, (BLOCK, BLOCK), 0)
        col = lax.broadcasted_iota(jnp.int32, (BLOCK, BLOCK), 1)
        s = _dot_nt(q_ref[0:BLOCK, :], km_ref[...])
        s = jnp.where((col >= META_LO) & (row >= col), s, NEG_INF)
        m = jnp.max(s, axis=-1, keepdims=True)
        oe = _dot(jnp.exp2(s - m).astype(BF16), vm_ref[...])
        o_ref[0:BLOCK, :] = _normalise(oe).astype(o_ref.dtype)


def _mla_branch(proj, lay, cq_g, ckv_g, w_uq, w_ukv, qn_g, kn_g, cos_t, sin_t, batch, seq):
    m = proj.shape[0]
    qr = cq_g.shape[0]
    kvr = ckv_g.shape[0]
    heads = w_uq.shape[1] // MLA_QK
    tm = _pick(m, 640, 128)
    scale = MLA_QK ** -0.5 * LOG2E

    w_uq_h = jnp.pad(w_uq.reshape(qr, heads, MLA_QK), ((0, 0), (0, 0), (0, MLA_QK_PAD - MLA_QK)))
    w_uq_h = jnp.transpose(w_uq_h, (1, 0, 2)).astype(BF16)
    w_ukv_b = w_ukv.astype(BF16)
    qg = (jnp.pad(qn_g, (0, MLA_QK_PAD - MLA_QK)) * scale).reshape(1, MLA_QK_PAD)
    kg = jnp.pad(kn_g, (0, MLA_QK_PAD - MLA_QK)).reshape(1, MLA_QK_PAD)
    v_ext = MLA_V + MLA_V

    q = pl.pallas_call(
        functools.partial(_mla_q_kernel, heads=heads),
        grid=(m // tm,),
        in_specs=[pl.BlockSpec((tm, qr), lambda i: (i, lay["c_q"] // qr)),
                  pl.BlockSpec((1, qr), lambda i: (0, 0)),
                  pl.BlockSpec((heads, qr, MLA_QK_PAD), lambda i: (0, 0, 0)),
                  pl.BlockSpec((1, MLA_QK_PAD), lambda i: (0, 0)),
                  pl.BlockSpec((tm, LANE), lambda i: (i, 0)),
                  pl.BlockSpec((tm, LANE), lambda i: (i, 0))],
        out_specs=pl.BlockSpec((heads, tm, MLA_QK_PAD), lambda i: (0, i, 0)),
        out_shape=jax.ShapeDtypeStruct((heads, m, MLA_QK_PAD), BF16),
        compiler_params=_params(("arbitrary",)),
        name="mla_q",
    )(proj, cq_g.reshape(1, qr), w_uq_h, qg, cos_t, sin_t)

    k, v = pl.pallas_call(
        functools.partial(_mla_kv_kernel, heads=heads),
        grid=(m // tm,),
        in_specs=[pl.BlockSpec((tm, kvr), lambda i: (i, lay["c_kv"] // kvr)),
                  pl.BlockSpec((1, kvr), lambda i: (0, 0)),
                  pl.BlockSpec((kvr, heads * (MLA_NOPE + MLA_V)), lambda i: (0, 0)),
                  pl.BlockSpec((tm, LANE), lambda i: (i, lay["k_pe"] // LANE)),
                  pl.BlockSpec((1, MLA_QK_PAD), lambda i: (0, 0)),
                  pl.BlockSpec((tm, LANE), lambda i: (i, 0)),
                  pl.BlockSpec((tm, LANE), lambda i: (i, 0))],
        out_specs=[pl.BlockSpec((heads, tm, MLA_QK_PAD), lambda i: (0, i, 0)),
                   pl.BlockSpec((heads, tm, v_ext), lambda i: (0, i, 0))],
        out_shape=[jax.ShapeDtypeStruct((heads, m, MLA_QK_PAD), BF16),
                   jax.ShapeDtypeStruct((heads, m, v_ext), BF16)],
        compiler_params=_params(("arbitrary",)),
        name="mla_kv",
    )(proj, ckv_g.reshape(1, kvr), w_ukv_b, proj, kg, cos_t, sin_t)

    meta_blk = m // BLOCK - 1
    tq = _pick(seq, 512, 128)
    a = pl.pallas_call(
        functools.partial(_mla_attn_kernel, tq=tq, batch=batch),
        grid=(heads, batch + 1),
        in_specs=[pl.BlockSpec((None, seq, MLA_QK_PAD), lambda h, b: (h, b, 0)),
                  pl.BlockSpec((None, seq, MLA_QK_PAD), lambda h, b: (h, b, 0)),
                  pl.BlockSpec((None, seq, v_ext), lambda h, b: (h, b, 0)),
                  pl.BlockSpec((None, BLOCK, MLA_QK_PAD), lambda h, b: (h, meta_blk, 0)),
                  pl.BlockSpec((None, BLOCK, v_ext), lambda h, b: (h, meta_blk, 0))],
        out_specs=pl.BlockSpec((seq, MLA_V), lambda h, b: (b, h)),
        out_shape=jax.ShapeDtypeStruct((m, heads * MLA_V), BF16),
        compiler_params=_params(("arbitrary", "arbitrary")),
        name="mla_attn",
    )(q, k, v, k, v)
    return a


def _t5_bucket(rel, buckets):
    n = jnp.maximum(rel, 0)
    max_exact = buckets // 2
    nf = jnp.maximum(n, 1).astype(F32)
    large = max_exact + (jnp.log(nf / max_exact) / math.log(REL_MAX_DIST / max_exact)
                         * (buckets - max_exact)).astype(jnp.int32)
    large = jnp.minimum(large, buckets - 1)
    return jnp.where(n < max_exact, n, large)


def _bias_kernel(rb_ref, bucket_ref, o_ref, *, buckets):
    pr = pl.program_id(0)
    bucket = bucket_ref[...]
    qi = lax.broadcasted_iota(jnp.int32, bucket.shape, 0)
    sj = lax.broadcasted_iota(jnp.int32, bucket.shape, 1)
    band = (sj - qi >= 1) & (sj - qi <= WINDOW)
    halves = []
    for t in range(2):
        acc = jnp.zeros(bucket.shape, F32)
        for b in range(buckets):
            acc = jnp.where(bucket == b, rb_ref[b, 2 * pr + t], acc)
        halves.append(jnp.where(band, acc * LOG2E, NEG_INF))
    o_ref[...] = jnp.concatenate(halves, axis=1)


def _bias_table(rel_bias):
    buckets, hq = rel_bias.shape
    qi = jnp.arange(BLOCK)[:, None]
    sj = jnp.arange(2 * BLOCK)[None, :]
    bucket = _t5_bucket(qi + BLOCK - sj, buckets).astype(jnp.int32)
    return pl.pallas_call(
        functools.partial(_bias_kernel, buckets=buckets),
        grid=(hq // 2,),
        in_specs=[pl.BlockSpec(memory_space=pltpu.SMEM),
                  pl.BlockSpec((BLOCK, 2 * BLOCK), lambda h: (0, 0))],
        out_specs=pl.BlockSpec((None, BLOCK, 4 * BLOCK), lambda h: (h, 0, 0)),
        out_shape=jax.ShapeDtypeStruct((hq // 2, BLOCK, 4 * BLOCK), F32),
        compiler_params=_params(("arbitrary",)),
        name="rel_bias_table",
    )(rel_bias, bucket)


def _seg_rms(x, seg_ref, segt_ref, gain):
    ss = _dot((x * x).astype(BF16), seg_ref[...])
    r = lax.rsqrt(ss * (1.0 / SWA_HEAD_DIM) + EPS)
    return x * _dot(r.astype(BF16), segt_ref[...]) * gain


def _swa_kernel(prev_ref, plo_ref, clo_ref, q_ref, kc_ref, kp_ref, vc_ref, vp_ref, bias_ref, sink_ref,
                qg_ref, kg_ref, segq_ref, segqt_ref, segk_ref, segkt_ref, ones_ref, o_ref, *, hq, hkv):
    del prev_ref
    t = pl.program_id(0)
    plo = plo_ref[t]
    clo = clo_ref[t]
    half = LANE // 2
    pairs_per_group = hq // hkv // 2

    sj = lax.broadcasted_iota(jnp.int32, (BLOCK, 4 * BLOCK), 1) % (2 * BLOCK)
    in_prev = sj < BLOCK
    krow = jnp.where(in_prev, sj, sj - BLOCK)
    key_bias = jnp.where(krow >= jnp.where(in_prev, plo, clo), 0.0, NEG_INF)

    qn = _seg_rms(q_ref[...].astype(F32), segq_ref, segqt_ref, qg_ref[...]).astype(BF16)
    kf = jnp.concatenate([kp_ref[...], kc_ref[...]], axis=0).astype(F32)
    kn = _seg_rms(kf, segk_ref, segkt_ref, kg_ref[...])
    vf = jnp.concatenate([vp_ref[...], vc_ref[...]], axis=0).astype(F32)

    lane2 = lax.broadcasted_iota(jnp.int32, (2 * BLOCK, LANE), 1)
    lane1 = lax.broadcasted_iota(jnp.int32, (BLOCK, LANE), 1)

    def block_diag(col, odd):
        other = pltpu.roll(col, half, 1)
        on_lo, on_hi = (other, col) if odd else (col, other)
        return jnp.concatenate([jnp.where(lane2 < half, on_lo, 0.0), jnp.where(lane2 < half, 0.0, on_hi)],
                               axis=0).astype(BF16)

    outs = []
    for g in range(hkv):
        c = g // 2
        kbd = block_diag(kn[:, c * LANE:(c + 1) * LANE], g % 2)
        vext = jnp.concatenate([block_diag(vf[:, c * LANE:(c + 1) * LANE], g % 2), ones_ref[...]], axis=1)
        for jj in range(pairs_per_group):
            pr = g * pairs_per_group + jj
            s = _dot_nt(qn[:, pr * LANE:(pr + 1) * LANE], kbd) + bias_ref[pr] + key_bias
            sa, sb = s[:, :2 * BLOCK], s[:, 2 * BLOCK:]
            ma = jnp.maximum(jnp.max(sa, axis=-1, keepdims=True), sink_ref[2 * pr])
            mb = jnp.maximum(jnp.max(sb, axis=-1, keepdims=True), sink_ref[2 * pr + 1])
            e = jnp.concatenate([jnp.exp2(sa - ma), jnp.exp2(sb - mb)], axis=1).astype(BF16)
            oe = _dot(e, vext)
            sink_term = jnp.where(lane1 < half, jnp.exp2(sink_ref[2 * pr] - ma),
                                  jnp.exp2(sink_ref[2 * pr + 1] - mb))
            outs.append(oe[:, :LANE] / (oe[:, LANE:] + sink_term))
    o_ref[...] = jnp.concatenate(outs, axis=-1).astype(o_ref.dtype)


def _swa_branch(proj, lay, bias_tab, sinks, qg, kg, batch, seq):
    m = proj.shape[0]
    hq = sinks.shape[0]
    hd = SWA_HEAD_DIM
    qw = hq * hd
    kw = lay["v_s"] - lay["k_s"]
    hkv = kw // hd
    nblk = m // BLOCK
    per = seq // BLOCK
    t = jnp.arange(nblk, dtype=jnp.int32)
    is_meta = t == nblk - 1
    first = (t % per) == 0
    prev = jnp.where(is_meta | first, nblk - 1, t - 1).astype(jnp.int32)
    plo = jnp.where(is_meta, BLOCK, jnp.where(first, META_LO, 0)).astype(jnp.int32)
    clo = jnp.where(is_meta, META_LO, 0).astype(jnp.int32)
    assert (hq // hkv) % 2 == 0 and kw % LANE == 0 and hq <= LANE

    def seg(width):
        return (jnp.arange(width)[:, None] // hd == jnp.arange(LANE)[None, :]).astype(BF16)

    segq, segk = seg(qw), seg(kw)
    qg_t = (jnp.tile(qg, hq) * (hd ** -0.5 * LOG2E)).reshape(1, qw)
    kg_t = jnp.tile(kg, hkv).reshape(1, kw)
    ones_bd = ((jnp.arange(4 * BLOCK)[:, None] < 2 * BLOCK)
               == (jnp.arange(LANE)[None, :] < LANE // 2)).astype(BF16)

    def const(shape):
        return pl.BlockSpec(shape, lambda i, pr, pl_, cl: (0,) * len(shape))

    grid_spec = pltpu.PrefetchScalarGridSpec(
        num_scalar_prefetch=3,
        grid=(nblk,),
        in_specs=[pl.BlockSpec((BLOCK, qw), lambda i, pr, pl_, cl: (i, lay["q_s"] // qw)),
                  pl.BlockSpec((BLOCK, kw), lambda i, pr, pl_, cl: (i, lay["k_s"] // kw)),
                  pl.BlockSpec((BLOCK, kw), lambda i, pr, pl_, cl: (pr[i], lay["k_s"] // kw)),
                  pl.BlockSpec((BLOCK, kw), lambda i, pr, pl_, cl: (i, lay["v_s"] // kw)),
                  pl.BlockSpec((BLOCK, kw), lambda i, pr, pl_, cl: (pr[i], lay["v_s"] // kw)),
                  const((hq // 2, BLOCK, 4 * BLOCK)),
                  pl.BlockSpec(memory_space=pltpu.SMEM),
                  const((1, qw)), const((1, kw)),
                  const((qw, LANE)), const((LANE, qw)), const((kw, LANE)), const((LANE, kw)),
                  const((4 * BLOCK, LANE))],
        out_specs=pl.BlockSpec((BLOCK, qw), lambda i, pr, pl_, cl: (i, 0)),
    )
    return pl.pallas_call(
        functools.partial(_swa_kernel, hq=hq, hkv=hkv),
        grid_spec=grid_spec,
        out_shape=jax.ShapeDtypeStruct((m, qw), BF16),
        compiler_params=_params(("arbitrary",)),
        name="swa_attn",
    )(prev, plo, clo, proj, proj, proj, proj, proj, bias_tab, sinks * LOG2E, qg_t, kg_t,
      segq, segq.T, segk, segk.T, ones_bd)


def _merge_kernel(a_ref, b_ref, wa_hbm, wb_hbm, ga_ref, gb_ref, o_ref, waf_ref, wab_ref, sema, wbf_ref, wbb_ref,
                  semb, *, layer, tn):
    _resident_weights([_col_window(wa_hbm, (layer,), 0, wa_hbm.shape[1], tn)], waf_ref, wab_ref, sema)
    _resident_weights([_col_window(wb_hbm, (layer,), 0, wb_hbm.shape[1], tn)], wbf_ref, wbb_ref, semb)
    ya = _dot(a_ref[...], wab_ref[0])
    yb = _dot(b_ref[...], wbb_ref[0])
    ga = _sigmoid(ga_ref[...].astype(F32))
    gb = _sigmoid(gb_ref[...].astype(F32))
    o_ref[...] = (ga * ya + gb * yb).astype(o_ref.dtype)


def _merge(a, b, w_a, w_b, layer, proj, lay, tn):
    m, ka = a.shape
    kb = b.shape[1]
    d = w_a.shape[-1]
    tm = _pick(m, 1664, 128)
    ga0 = lay["g_a"] // tn
    gb0 = lay["g_b"] // tn
    hbm = pl.BlockSpec(memory_space=pl.ANY)
    return pl.pallas_call(
        functools.partial(_merge_kernel, layer=layer, tn=tn),
        grid=(d // tn, m // tm),
        in_specs=[pl.BlockSpec((tm, ka), lambda j, i: (i, 0)),
                  pl.BlockSpec((tm, kb), lambda j, i: (i, 0)),
                  hbm, hbm,
                  pl.BlockSpec((tm, tn), lambda j, i: (i, ga0 + j)),
                  pl.BlockSpec((tm, tn), lambda j, i: (i, gb0 + j))],
        out_specs=pl.BlockSpec((tm, tn), lambda j, i: (i, j)),
        out_shape=jax.ShapeDtypeStruct((m, d), BF16),
        scratch_shapes=_staged(1, ka, tn) + _staged(1, kb, tn),
        compiler_params=_params(("arbitrary", "arbitrary"), VMEM_LIMIT_BIG),
        name="branch_merge",
    )(a, b, w_a, w_b, proj, proj)


def _ffn_up_kernel(x_ref, w1_ref, w3_ref, o_ref, w1b_ref, w3b_ref):
    @pl.when(pl.program_id(1) == 0)
    def _():
        _cast_weight(w1_ref, w1b_ref)
        _cast_weight(w3_ref, w3b_ref)

    x = x_ref[...]
    u = _dot(x, w1b_ref[...])
    g = _dot(x, w3b_ref[...])
    o_ref[...] = (u * _sigmoid(u) * g).astype(o_ref.dtype)


def _ffn_up(hn, w1, w3, idx):
    m, d = hn.shape
    ff = w1.shape[-1]
    tn = _pick(ff, 256, 128)
    tm = _pick(m, 1664, 128)
    return pl.pallas_call(
        _ffn_up_kernel,
        grid=(ff // tn, m // tm),
        in_specs=[pl.BlockSpec((tm, d), lambda j, i: (i, 0)),
                  pl.BlockSpec((None, d, tn), lambda j, i: (idx, 0, j)),
                  pl.BlockSpec((None, d, tn), lambda j, i: (idx, 0, j))],
        out_specs=pl.BlockSpec((tm, tn), lambda j, i: (i, j)),
        out_shape=jax.ShapeDtypeStruct((m, ff), BF16),
        scratch_shapes=[pltpu.VMEM((d, tn), BF16), pltpu.VMEM((d, tn), BF16)],
        compiler_params=_params(("arbitrary", "arbitrary"), VMEM_LIMIT_BIG),
        name="ffn_up",
    )(hn, w1, w3)


def _dense_ffn(h_res, gain, w1, w3, w2, idx):
    hn = _rms_norm(h_res, gain)
    t = _ffn_up(hn, w1, w3, idx)
    ff = w1.shape[-1]
    nk = 1
    for cand in (1, 2, 4):
        if ff % cand == 0 and (ff // cand) % LANE == 0 and ff // cand <= 6144:
            nk = cand
            break
    kc = ff // nk
    out = h_res
    for kblk in range(nk):
        out = _mm_res(t, w2, (idx,), kblk, kc, out, 640, "ffn_down")
    return out


def _router_kernel(x_ref, g_ref, w_ref, o_ref, *, n_exp):
    x = x_ref[...]
    ms = jnp.mean(x * x, axis=-1, keepdims=True)
    xn = (x * lax.rsqrt(ms + EPS) * g_ref[...]).astype(BF16)
    logits = _dot(xn, w_ref[...].astype(BF16))
    lane = lax.broadcasted_iota(jnp.int32, logits.shape, 1)
    logits = jnp.where(lane < n_exp, logits, -jnp.inf)
    lane_f = lane.astype(F32)
    m1 = jnp.max(logits, axis=-1, keepdims=True)
    i1 = jnp.min(jnp.where(logits == m1, lane_f, float(LANE)), axis=-1, keepdims=True)
    rest = jnp.where(lane_f == i1, -jnp.inf, logits)
    m2 = jnp.max(rest, axis=-1, keepdims=True)
    i2 = jnp.min(jnp.where(rest == m2, lane_f, float(LANE)), axis=-1, keepdims=True)
    e2 = jnp.exp(m2 - m1)
    w1 = 1.0 / (1.0 + e2)
    w2 = e2 / (1.0 + e2)
    out = jnp.where(lane == 0, i1, jnp.where(lane == 1, i2,
                                             jnp.where(lane == 2, w1, jnp.where(lane == 3, w2, 0.0))))
    o_ref[...] = out


def _router(h_res, gain, router_w):
    m, d = h_res.shape
    n_exp = router_w.shape[-1]
    tm = _pick(m, 640, 128)
    wpad = jnp.pad(router_w, ((0, 0), (0, LANE - n_exp)))
    return pl.pallas_call(
        functools.partial(_router_kernel, n_exp=n_exp),
        grid=(m // tm,),
        in_specs=[pl.BlockSpec((tm, d), lambda i: (i, 0)),
                  pl.BlockSpec((1, d), lambda i: (0, 0)),
                  pl.BlockSpec((d, LANE), lambda i: (0, 0))],
        out_specs=pl.BlockSpec((tm, LANE), lambda i: (i, 0)),
        out_shape=jax.ShapeDtypeStruct((m, LANE), F32),
        compiler_params=_params(("arbitrary",)),
        name="moe_router",
    )(h_res, gain.reshape(1, d), wpad)


def _row_copy(src_hbm, buf, sem, slot, src_row, dst_row):
    return pltpu.make_async_copy(src_hbm.at[pl.ds(src_row, 1)], buf.at[slot, pl.ds(dst_row, 1)],
                                 sem.at[slot])


def _start_row_gather(src_hbm, buf, sem, slot, idx_ref, base, count):
    def body(r2, c):
        for p in range(2):
            r = 2 * r2 + p
            _row_copy(src_hbm, buf, sem, slot, idx_ref[base + r], r).start(priority=p)
        return c
    lax.fori_loop(0, count // 2, body, 0)


def _wait_row_gather(src_hbm, buf, sem, slot, count):
    def body(r, c):
        _row_copy(src_hbm, buf, sem, slot, 0, r).wait()
        return c
    lax.fori_loop(0, count, body, 0)


def _gather_norm_kernel(tok_ref, live_ref, h_hbm, g_ref, o_ref, buf, sem, *, rows):
    s = pl.program_id(0)
    n = pl.num_programs(0)

    @pl.when((s == 0) & (live_ref[0] == 1))
    def _():
        _start_row_gather(h_hbm, buf, sem, 0, tok_ref, 0, rows)

    nxt = jnp.minimum(s + 1, n - 1)

    @pl.when((s + 1 < n) & (live_ref[nxt] == 1))
    def _():
        _start_row_gather(h_hbm, buf, sem, (s + 1) % 2, tok_ref, (s + 1) * rows, rows)

    slot = s % 2

    @pl.when(live_ref[s] == 1)
    def _():
        _wait_row_gather(h_hbm, buf, sem, slot, rows)
        x = buf[slot]
        ms = jnp.mean(x * x, axis=-1, keepdims=True)
        o_ref[...] = (x * lax.rsqrt(ms + EPS) * g_ref[...]).astype(o_ref.dtype)

    @pl.when(live_ref[s] == 0)
    def _():
        o_ref[...] = jnp.zeros(o_ref.shape, o_ref.dtype)


def _gather_norm(h_res, gain, row_tok, live, rows):
    d = h_res.shape[1]
    rp = row_tok.shape[0]
    grid_spec = pltpu.PrefetchScalarGridSpec(
        num_scalar_prefetch=2,
        grid=(rp // rows,),
        in_specs=[pl.BlockSpec(memory_space=pl.ANY),
                  pl.BlockSpec((1, d), lambda i, tok, lv: (0, 0))],
        out_specs=pl.BlockSpec((rows, d), lambda i, tok, lv: (i, 0)),
        scratch_shapes=[pltpu.VMEM((2, rows, d), F32), pltpu.SemaphoreType.DMA((2,))],
    )
    return pl.pallas_call(
        functools.partial(_gather_norm_kernel, rows=rows),
        grid_spec=grid_spec,
        out_shape=jax.ShapeDtypeStruct((rp, d), BF16),
        compiler_params=_params(("arbitrary",)),
        name="moe_gather_norm",
    )(row_tok, live, h_res, gain.reshape(1, d))


def _for_live_subtiles(valid_ref, o_ref, sub, compute):
    i = pl.program_id(1)
    for s in range(o_ref.shape[0] // sub):
        rows = slice(s * sub, (s + 1) * sub)

        @pl.when(valid_ref[i] > s)
        def _(rows=rows):
            o_ref[rows, :] = compute(rows)

        @pl.when(valid_ref[i] <= s)
        def _(rows=rows):
            o_ref[rows, :] = jnp.zeros((sub, o_ref.shape[1]), o_ref.dtype)


def _expert_weights(sched, w_hbms, wf_ref, wb_ref, sem, *, layer, tn):
    be_ref, first_ref, nf_ref = sched
    j = pl.program_id(0)
    i = pl.program_id(1)
    nj = pl.num_programs(0)
    n_w = len(w_hbms)

    def copy(which, e, jj):
        src = w_hbms[which].at[layer, e, :, pl.ds(pl.multiple_of(jj * tn, tn), tn)]
        return pltpu.make_async_copy(src, wf_ref.at[which], sem.at[which])

    @pl.when(first_ref[i] == 1)
    def _():
        @pl.when((j == 0) & (i == 0))
        def _():
            for which in range(n_w):
                copy(which, be_ref[0], 0).start()

        for which in range(n_w):
            copy(which, 0, 0).wait()
            _cast_weight(wf_ref.at[which], wb_ref.at[which])

        more_here = nf_ref[i] >= 0
        nxt_e = jnp.where(more_here, be_ref[jnp.maximum(nf_ref[i], 0)], be_ref[0])
        nxt_j = jnp.where(more_here, j, j + 1)

        @pl.when(more_here | (j + 1 < nj))
        def _():
            for which in range(n_w):
                copy(which, nxt_e, nxt_j).start()


def _moe_up_kernel(be_ref, first_ref, nf_ref, valid_ref, xb_ref, x_ref, w1_hbm, w3_hbm,
                   o_ref, wf_ref, wb_ref, sem, *, sub, layer, tn):
    _expert_weights((be_ref, first_ref, nf_ref), (w1_hbm, w3_hbm), wf_ref, wb_ref, sem,
                    layer=layer, tn=tn)

    def compute(rows):
        x = x_ref[rows, :]
        u = _dot(x, wb_ref[0])
        g = _dot(x, wb_ref[1])
        return (u * _sigmoid(u) * g).astype(o_ref.dtype)

    _for_live_subtiles(valid_ref, o_ref, sub, compute)


def _moe_down_kernel(be_ref, first_ref, nf_ref, valid_ref, xb_ref, x_ref, w_hbm,
                     o_ref, wf_ref, wb_ref, sem, *, sub, layer, tn):
    _expert_weights((be_ref, first_ref, nf_ref), (w_hbm,), wf_ref, wb_ref, sem, layer=layer, tn=tn)
    _for_live_subtiles(valid_ref, o_ref, sub, lambda rows: _dot(x_ref[rows, :], wb_ref[0]))


def _combine_kernel(pos_ref, h_ref, rt_ref, y_hbm, o_ref, buf, sem, *, rows):
    s = pl.program_id(0)
    n = pl.num_programs(0)

    @pl.when(s == 0)
    def _():
        _start_row_gather(y_hbm, buf, sem, 0, pos_ref, 0, 2 * rows)

    @pl.when(s + 1 < n)
    def _():
        _start_row_gather(y_hbm, buf, sem, (s + 1) % 2, pos_ref, (s + 1) * 2 * rows, 2 * rows)

    slot = s % 2
    _wait_row_gather(y_hbm, buf, sem, slot, 2 * rows)

    rt = rt_ref[...]
    y0 = buf[slot, 0:rows, :]
    y1 = buf[slot, rows:2 * rows, :]
    o_ref[...] = h_ref[...] + rt[:, 2:3] * y0 + rt[:, 3:4] * y1


def _moe_ffn(h_res, gain, router_w, w1, w3, w2, idx, n_out):
    m, d = h_res.shape
    n_exp = router_w.shape[-1]
    ff = w1.shape[-1]
    rt = _router(h_res, gain, router_w)

    sub = 256
    tme = 4 * sub
    grows = 512
    r_tot = TOP_K * m
    nb = -(-(r_tot + n_exp * (tme - 1)) // tme)
    rp = nb * tme
    e = rt[:, :TOP_K].astype(jnp.int32).reshape(-1)
    onehot = (e[:, None] == jnp.arange(n_exp, dtype=jnp.int32)[None, :]).astype(jnp.int32)
    csum = jnp.cumsum(onehot, axis=0)
    rank = jnp.sum(onehot * csum, axis=1) - 1
    counts = csum[-1]
    padded = (counts + tme - 1) // tme * tme
    gend = jnp.cumsum(padded)
    goff = gend - padded
    pos = (jnp.sum(onehot * goff[None, :], axis=1) + rank).astype(jnp.int32)
    row_tok = jnp.zeros((rp,), jnp.int32).at[pos].set(jnp.arange(r_tot, dtype=jnp.int32) // TOP_K)
    used = gend[-1] // tme
    blk = jnp.arange(nb, dtype=jnp.int32)
    xb = jnp.minimum(blk, used - 1).astype(jnp.int32)
    be = jnp.minimum(jnp.sum((xb[:, None] * tme >= gend[None, :]).astype(jnp.int32), axis=1),
                     n_exp - 1).astype(jnp.int32)
    first = jnp.concatenate([jnp.ones((1,), jnp.int32), (be[1:] != be[:-1]).astype(jnp.int32)])
    nxt_change = lax.cummin(jnp.where(first == 1, blk, nb)[::-1])[::-1]
    nf = jnp.concatenate([nxt_change[1:], jnp.full((1,), nb, jnp.int32)])
    nf = jnp.where(nf >= nb, -1, nf).astype(jnp.int32)

    def live_tiles(tile, per_block):
        start = jnp.arange(rp // tile, dtype=jnp.int32) * tile
        e_of = jnp.repeat(be, tme // tile)
        real_end = jnp.where(jnp.repeat(blk < used, tme // tile), (goff + counts)[e_of], 0)
        live = (start < real_end).astype(jnp.int32)
        return live if per_block is None else jnp.sum(live.reshape(nb, per_block), axis=1).astype(jnp.int32)

    valid = live_tiles(sub, tme // sub)
    glive = live_tiles(grows, None)

    xs = _gather_norm(h_res, gain, row_tok, glive, grows)

    sched = (be, first, nf, valid, xb)

    def x_map(j, i, *s):
        return (s[4][i], 0)

    def o_map(j, i, *s):
        return (i, j)

    hbm = pl.BlockSpec(memory_space=pl.ANY)
    tn_up = _pick(ff, 512, 128)
    up_spec = pltpu.PrefetchScalarGridSpec(
        num_scalar_prefetch=len(sched),
        grid=(ff // tn_up, nb),
        in_specs=[pl.BlockSpec((tme, d), x_map), hbm, hbm],
        out_specs=pl.BlockSpec((tme, tn_up), o_map),
        scratch_shapes=_staged(2, d, tn_up),
    )
    t = pl.pallas_call(
        functools.partial(_moe_up_kernel, sub=sub, layer=idx, tn=tn_up),
        grid_spec=up_spec,
        out_shape=jax.ShapeDtypeStruct((rp, ff), BF16),
        compiler_params=_params(("arbitrary", "arbitrary"), VMEM_LIMIT_BIG),
        name="moe_up",
    )(*sched, xs, w1, w3)

    tn_dn = _pick(d, 512, 128)
    dn_spec = pltpu.PrefetchScalarGridSpec(
        num_scalar_prefetch=len(sched),
        grid=(d // tn_dn, nb),
        in_specs=[pl.BlockSpec((tme, ff), x_map), hbm],
        out_specs=pl.BlockSpec((tme, tn_dn), o_map),
        scratch_shapes=_staged(1, ff, tn_dn),
    )
    y = pl.pallas_call(
        functools.partial(_moe_down_kernel, sub=sub, layer=idx, tn=tn_dn),
        grid_spec=dn_spec,
        out_shape=jax.ShapeDtypeStruct((rp, d), F32),
        compiler_params=_params(("arbitrary", "arbitrary"), VMEM_LIMIT_BIG),
        name="moe_down",
    )(*sched, t, w2)

    rows = 256 if n_out % 256 == 0 else 128
    cmb_spec = pltpu.PrefetchScalarGridSpec(
        num_scalar_prefetch=1,
        grid=(n_out // rows,),
        in_specs=[pl.BlockSpec((rows, d), lambda i, p: (i, 0)),
                  pl.BlockSpec((rows, LANE), lambda i, p: (i, 0)),
                  pl.BlockSpec(memory_space=pl.ANY)],
        out_specs=pl.BlockSpec((rows, d), lambda i, p: (i, 0)),
        scratch_shapes=[pltpu.VMEM((2, 2 * rows, d), F32), pltpu.SemaphoreType.DMA((2,))],
    )
    pos_blk = jnp.transpose(pos[:TOP_K * n_out].reshape(n_out // rows, rows, TOP_K), (0, 2, 1)).reshape(-1)
    return pl.pallas_call(
        functools.partial(_combine_kernel, rows=rows),
        grid_spec=cmb_spec,
        out_shape=jax.ShapeDtypeStruct((n_out, d), F32),
        compiler_params=_params(("arbitrary",)),
        name="moe_combine",
    )(pos_blk, h_res, rt, y)


def _in_layout(d, qr, kvr, qw, kw):
    head = qr + kvr
    rest = qw + 2 * kw + 2 * d
    tn = max(t for t in (512, 256, 128) if head % t == 0 and rest % t == 0)
    lay = {"c_q": 0, "c_kv": qr, "k_pe": head}
    cur = head + tn
    for name, width in (("q_s", qw), ("k_s", kw), ("v_s", kw), ("g_a", d), ("g_b", d)):
        lay[name] = cur
        cur += width
    lay["total"] = cur
    assert lay["c_kv"] % kvr == 0 and lay["q_s"] % qw == 0 and lay["k_s"] % kw == 0 and lay["v_s"] % kw == 0
    assert lay["g_a"] % tn == 0 and lay["g_b"] % tn == 0 and d % tn == 0
    return lay, tn


def kernel(x, meta_tokens, rel_bias, attn_norm, w_in, mla_cq_norm, mla_ckv_norm, mla_w_uq, mla_w_ukv,
           mla_q_norm, mla_k_norm, swa_q_norm, swa_k_norm, swa_sinks, w_branch_mla, w_branch_swa, w_out,
           ffn_norm, dense_w1, dense_w3, dense_w2, moe_router, moe_w1, moe_w3, moe_w2):
    batch, seq, d = x.shape
    depth = w_in.shape[0]
    qr = mla_cq_norm.shape[1]
    kvr = mla_ckv_norm.shape[1]
    hq = swa_sinks.shape[1]
    qw = hq * SWA_HEAD_DIM
    kw = (w_in.shape[2] - qr - kvr - MLA_ROPE - qw - 2 * d) // 2
    assert seq % BLOCK == 0 and meta_tokens.shape[0] == N_META
    n_tok = batch * seq
    m = n_tok + BLOCK

    lay, tn_in = _in_layout(d, qr, kvr, qw, kw)

    h_res = jnp.concatenate([x.reshape(n_tok, d), jnp.zeros((META_LO, d), x.dtype),
                             meta_tokens.astype(x.dtype)], axis=0)

    pos = jnp.concatenate([jnp.tile(N_META + jnp.arange(seq), batch), jnp.arange(BLOCK) - META_LO])
    half = MLA_ROPE // 2
    inv_freq = ROPE_THETA ** (-jnp.arange(half, dtype=F32) / half)
    ang = pos.astype(F32)[:, None] * inv_freq[None, :]
    cos, sin = jnp.cos(ang), jnp.sin(ang)
    cos_t = jnp.concatenate([cos, cos, jnp.ones((m, LANE - MLA_ROPE), F32)], axis=1)
    sin_t = jnp.concatenate([-sin, sin, jnp.zeros((m, LANE - MLA_ROPE), F32)], axis=1)

    bias_tab = _bias_table(rel_bias)
    w_in_t = jnp.swapaxes(w_in, 1, 2)

    for i in range(depth):
        hn = _rms_norm(h_res, attn_norm[i])
        proj = _in_proj(hn, w_in_t, i, lay, tn_in)
        a = _mla_branch(proj, lay, mla_cq_norm[i], mla_ckv_norm[i], mla_w_uq[i], mla_w_ukv[i],
                        mla_q_norm[i], mla_k_norm[i], cos_t, sin_t, batch, seq)
        b = _swa_branch(proj, lay, bias_tab, swa_sinks[i], swa_q_norm[i], swa_k_norm[i], batch, seq)
        merged = _merge(a, b, w_branch_mla, w_branch_swa, i, proj, lay, tn_in)
        h_res = _mm_res(merged, w_out, (i,), 0, d, h_res, 1664, "out_proj")
        last = i == depth - 1
        if i % 2 == 0:
            h_res = _dense_ffn(h_res, ffn_norm[i], dense_w1, dense_w3, dense_w2, i // 2)
            if last:
                h_res = h_res[:n_tok]
        else:
            h_res = _moe_ffn(h_res, ffn_norm[i], moe_router[i // 2], moe_w1, moe_w3, moe_w2, i // 2,
                             n_tok if last else m)
    return h_res.reshape(batch, seq, d)
```

```python
import functools
import math

import jax
import jax.numpy as jnp
from jax import lax
from jax.experimental import pallas as pl
from jax.experimental.pallas import tpu as pltpu

F32 = jnp.float32
BF16 = jnp.bfloat16

N_META = 16
BLOCK = 128
WINDOW = 128
MLA_NOPE = 128
MLA_ROPE = 64
MLA_V = 128
MLA_QK = MLA_NOPE + MLA_ROPE
MLA_QK_PAD = 256
ROPE_THETA = 10000.0
SWA_HEAD_DIM = 64
REL_MAX_DIST = 128
TOP_K = 2
EPS = 1e-6
NEG_INF = -1e30
LOG2E = 1.4426950408889634
LANE = 128
META_LO = BLOCK - N_META

VMEM_LIMIT_BIG = 58 * 1024 * 1024
VMEM_LIMIT_MID = 44 * 1024 * 1024


def _params(sem, vmem=VMEM_LIMIT_MID):
    return pltpu.CompilerParams(dimension_semantics=sem, vmem_limit_bytes=vmem)


def _pick(n, target, mult):
    best = None
    d = mult
    while d <= min(n, target):
        if n % d == 0:
            best = d
        d += mult
    return best if best is not None else n


def _round_up(a, b):
    return (a + b - 1) // b * b


def _cast_weight(w_ref, wb_ref):
    k = w_ref.shape[0]
    ch = 512 if k % 512 == 0 else (256 if k % 256 == 0 else 128)
    if k % ch != 0:
        wb_ref[...] = w_ref[...].astype(BF16)
        return

    def body(c, carry):
        r = pl.multiple_of(c * ch, ch)
        wb_ref[pl.ds(r, ch), :] = w_ref[pl.ds(r, ch), :].astype(BF16)
        return carry

    lax.fori_loop(0, k // ch, body, 0)


def _sigmoid(x):
    return 1.0 / (1.0 + jnp.exp(-x))


def _dot(a, b):
    return jnp.dot(a, b, preferred_element_type=F32)


def _dot_nt(a, b):
    return lax.dot_general(a, b, (((1,), (1,)), ((), ())), preferred_element_type=F32)


def _rms_kernel(x_ref, g_ref, o_ref):
    x = x_ref[...]
    ms = jnp.mean(x * x, axis=-1, keepdims=True)
    o_ref[...] = (x * lax.rsqrt(ms + EPS) * g_ref[...]).astype(o_ref.dtype)


def _rms_norm(h, gain):
    m, d = h.shape
    tm = _pick(m, 640, 128)
    return pl.pallas_call(
        _rms_kernel,
        grid=(m // tm,),
        in_specs=[pl.BlockSpec((tm, d), lambda i: (i, 0)),
                  pl.BlockSpec((1, d), lambda i: (0, 0))],
        out_specs=pl.BlockSpec((tm, d), lambda i: (i, 0)),
        out_shape=jax.ShapeDtypeStruct((m, d), BF16),
        compiler_params=_params(("arbitrary",)),
        name="rms_norm",
    )(h, gain.reshape(1, d))


def _in_proj_kernel(x_ref, a_ref, b_ref, o_ref, wb_ref, *, nh):
    j = pl.program_id(0)
    tn, k = a_ref.shape
    half = LANE // 2

    @pl.when(pl.program_id(1) == 0)
    def _():
        @pl.when(j < nh)
        def _():
            wb_ref[...] = a_ref[...].astype(BF16)

        @pl.when(j == nh)
        def _():
            wb_ref[0:half, :] = a_ref[0:half, :].astype(BF16)
            wb_ref[half:tn, :] = jnp.zeros((tn - half, k), BF16)

        @pl.when(j > nh)
        def _():
            wb_ref[0:tn - half, :] = a_ref[half:tn, :].astype(BF16)
            wb_ref[tn - half:tn, :] = b_ref[0:half, :].astype(BF16)

    o_ref[...] = _dot_nt(x_ref[...], wb_ref[...]).astype(o_ref.dtype)


def _in_proj(hn, w_in_t, layer, lay, tn):
    m, k = hn.shape
    nh = lay["k_pe"] // tn
    nblk = lay["total"] // tn
    tm = _pick(m, 1664, 128)
    sub = tn // LANE
    last_b = (w_in_t.shape[1] - 1) // LANE

    def a_map(j, i):
        return (layer, jnp.where(j > nh, j - 1, j), 0)

    def b_map(j, i):
        return (layer, jnp.minimum(jnp.maximum(j, nh + 1) * sub, last_b), 0)

    return pl.pallas_call(
        functools.partial(_in_proj_kernel, nh=nh),
        grid=(nblk, m // tm),
        in_specs=[pl.BlockSpec((tm, k), lambda j, i: (i, 0)),
                  pl.BlockSpec((None, tn, k), a_map),
                  pl.BlockSpec((None, LANE, k), b_map)],
        out_specs=pl.BlockSpec((tm, tn), lambda j, i: (i, j)),
        out_shape=jax.ShapeDtypeStruct((m, lay["total"]), BF16),
        scratch_shapes=[pltpu.VMEM((tn, k), BF16)],
        compiler_params=_params(("arbitrary", "arbitrary"), VMEM_LIMIT_BIG),
        name="in_proj",
    )(hn, w_in_t, w_in_t)


def _resident_weights(windows, wf_ref, wb_ref, sem):
    j = pl.program_id(0)
    nj = pl.num_programs(0)
    n_w = len(windows)

    def copy(which, jj):
        return pltpu.make_async_copy(windows[which](jj), wf_ref.at[which], sem.at[which])

    @pl.when(pl.program_id(1) == 0)
    def _():
        @pl.when(j == 0)
        def _():
            for which in range(n_w):
                copy(which, 0).start()

        for which in range(n_w):
            copy(which, 0).wait()
            _cast_weight(wf_ref.at[which], wb_ref.at[which])

        @pl.when(j + 1 < nj)
        def _():
            for which in range(n_w):
                copy(which, j + 1).start()


def _staged(n_w, k, tn):
    return [pltpu.VMEM((n_w, k, tn), F32), pltpu.VMEM((n_w, k, tn), BF16), pltpu.SemaphoreType.DMA((n_w,))]


def _col_window(w_hbm, lead, k0, kc, tn):
    return lambda jj: w_hbm.at[(*lead, pl.ds(k0, kc), pl.ds(pl.multiple_of(jj * tn, tn), tn))]


def _mm_res_kernel(x_ref, w_hbm, r_ref, o_ref, wf_ref, wb_ref, sem, *, lead, k0, kc, tn):
    _resident_weights([_col_window(w_hbm, lead, k0, kc, tn)], wf_ref, wb_ref, sem)
    o_ref[...] = r_ref[...] + _dot(x_ref[...], wb_ref[0])


def _mm_res(x, w, lead, kblk, kc, res, tm_target, name):
    m = x.shape[0]
    n = w.shape[-1]
    tn = _pick(n, 512, 128)
    tm = _pick(m, tm_target, 128)
    return pl.pallas_call(
        functools.partial(_mm_res_kernel, lead=lead, k0=kblk * kc, kc=kc, tn=tn),
        grid=(n // tn, m // tm),
        in_specs=[pl.BlockSpec((tm, kc), lambda j, i: (i, kblk)),
                  pl.BlockSpec(memory_space=pl.ANY),
                  pl.BlockSpec((tm, tn), lambda j, i: (i, j))],
        out_specs=pl.BlockSpec((tm, tn), lambda j, i: (i, j)),
        out_shape=jax.ShapeDtypeStruct((m, n), F32),
        scratch_shapes=_staged(1, kc, tn),
        compiler_params=_params(("arbitrary", "arbitrary"), VMEM_LIMIT_BIG),
        name=name,
    )(x, w, res)


def _rope(hi, cos, sin):
    lane = lax.broadcasted_iota(jnp.int32, hi.shape, 1)
    half = MLA_ROPE // 2
    rot = jnp.where(lane < half, pltpu.roll(hi, LANE - half, 1), pltpu.roll(hi, half, 1))
    return hi * cos + rot * sin


def _row_rms(x_ref, g_ref):
    x = x_ref[...].astype(F32)
    ms = jnp.mean(x * x, axis=-1, keepdims=True)
    return (x * lax.rsqrt(ms + EPS) * g_ref[...]).astype(BF16)


def _mla_q_kernel(x_ref, g_ref, w_ref, qg_ref, cos_ref, sin_ref, o_ref, *, heads):
    xn = _row_rms(x_ref, g_ref)
    qg = qg_ref[...]
    cos = cos_ref[...]
    sin = sin_ref[...]
    for h in range(heads):
        q = _dot(xn, w_ref[h])
        ss = jnp.sum(q * q, axis=-1, keepdims=True)
        qn = q * lax.rsqrt(ss * (1.0 / MLA_QK) + EPS) * qg
        hi = _rope(qn[:, LANE:], cos, sin)
        o_ref[h] = jnp.concatenate([qn[:, :LANE], hi], axis=-1).astype(o_ref.dtype)


def _mla_kv_kernel(x_ref, g_ref, w_ref, pe_ref, kg_ref, cos_ref, sin_ref, k_ref, v_ref, *, heads):
    xn = _row_rms(x_ref, g_ref)
    kg = kg_ref[...]
    pe = pe_ref[...].astype(F32)
    pe_ss = jnp.sum(pe * pe, axis=-1, keepdims=True)
    pe_rot = _rope(pe * kg[:, LANE:], cos_ref[...], sin_ref[...])
    ones = jnp.ones((xn.shape[0], MLA_V), v_ref.dtype)
    width = MLA_NOPE + MLA_V
    for h in range(heads):
        kv = _dot(xn, w_ref[:, h * width:(h + 1) * width])
        kn = kv[:, :MLA_NOPE]
        r = lax.rsqrt((jnp.sum(kn * kn, axis=-1, keepdims=True) + pe_ss) * (1.0 / MLA_QK) + EPS)
        k_ref[h] = jnp.concatenate([kn * r * kg[:, :LANE], pe_rot * r], axis=-1).astype(k_ref.dtype)
        v_ref[h] = jnp.concatenate([kv[:, MLA_NOPE:].astype(v_ref.dtype), ones], axis=-1)


def _normalise(oe):
    return oe[:, :MLA_V] / oe[:, MLA_V:]


def _mla_attn_kernel(q_ref, k_ref, v_ref, km_ref, vm_ref, o_ref, *, tq, batch):
    seq = q_ref.shape[0]
    nq = seq // tq
    b = pl.program_id(1)

    @pl.when(b < batch)
    def _():
        km = km_ref[...]
        vm = vm_ref[...]
        mcol = lax.broadcasted_iota(jnp.int32, (tq, BLOCK), 1)
        meta_bias = jnp.where(mcol >= META_LO, 0.0, NEG_INF)
        row = lax.broadcasted_iota(jnp.int32, (tq, tq), 0)
        col = lax.broadcasted_iota(jnp.int32, (tq, tq), 1)
        causal_bias = jnp.where(row >= col, 0.0, NEG_INF)
        for qi in range(nq):
            lo, hi = qi * tq, (qi + 1) * tq
            q = q_ref[lo:hi, :]
            s0 = _dot_nt(q, km) + meta_bias
            sd = _dot_nt(q, k_ref[lo:hi, :]) + causal_bias
            m = jnp.maximum(jnp.max(s0, axis=-1, keepdims=True), jnp.max(sd, axis=-1, keepdims=True))
            if qi > 0:
                sf = _dot_nt(q, k_ref[0:lo, :])
                m = jnp.maximum(m, jnp.max(sf, axis=-1, keepdims=True))
            oe = _dot(jnp.exp2(s0 - m).astype(BF16), vm)
            oe = oe + _dot(jnp.exp2(sd - m).astype(BF16), v_ref[lo:hi, :])
            if qi > 0:
                oe = oe + _dot(jnp.exp2(sf - m).astype(BF16), v_ref[0:lo, :])
            o_ref[lo:hi, :] = _normalise(oe).astype(o_ref.dtype)

    @pl.when(b == batch)
    def _():
        row = lax.broadcasted_iota(jnp.int32, (BLOCK, BLOCK), 0)
        col = lax.broadcasted_iota(jnp.int32, (BLOCK, BLOCK), 1)
        s = _dot_nt(q_ref[0:BLOCK, :], km_ref[...])
        s = jnp.where((col >= META_LO) & (row >= col), s, NEG_INF)
        m = jnp.max(s, axis=-1, keepdims=True)
        oe = _dot(jnp.exp2(s - m).astype(BF16), vm_ref[...])
        o_ref[0:BLOCK, :] = _normalise(oe).astype(o_ref.dtype)


def _mla_branch(proj, lay, cq_g, ckv_g, w_uq, w_ukv, qn_g, kn_g, cos_t, sin_t, batch, seq):
    m = proj.shape[0]
    qr = cq_g.shape[0]
    kvr = ckv_g.shape[0]
    heads = w_uq.shape[1] // MLA_QK
    tm = _pick(m, 640, 128)
    scale = MLA_QK ** -0.5 * LOG2E

    w_uq_h = jnp.pad(w_uq.reshape(qr, heads, MLA_QK), ((0, 0), (0, 0), (0, MLA_QK_PAD - MLA_QK)))
    w_uq_h = jnp.transpose(w_uq_h, (1, 0, 2)).astype(BF16)
    w_ukv_b = w_ukv.astype(BF16)
    qg = (jnp.pad(qn_g, (0, MLA_QK_PAD - MLA_QK)) * scale).reshape(1, MLA_QK_PAD)
    kg = jnp.pad(kn_g, (0, MLA_QK_PAD - MLA_QK)).reshape(1, MLA_QK_PAD)
    v_ext = MLA_V + MLA_V

    q = pl.pallas_call(
        functools.partial(_mla_q_kernel, heads=heads),
        grid=(m // tm,),
        in_specs=[pl.BlockSpec((tm, qr), lambda i: (i, lay["c_q"] // qr)),
                  pl.BlockSpec((1, qr), lambda i: (0, 0)),
                  pl.BlockSpec((heads, qr, MLA_QK_PAD), lambda i: (0, 0, 0)),
                  pl.BlockSpec((1, MLA_QK_PAD), lambda i: (0, 0)),
                  pl.BlockSpec((tm, LANE), lambda i: (i, 0)),
                  pl.BlockSpec((tm, LANE), lambda i: (i, 0))],
        out_specs=pl.BlockSpec((heads, tm, MLA_QK_PAD), lambda i: (0, i, 0)),
        out_shape=jax.ShapeDtypeStruct((heads, m, MLA_QK_PAD), BF16),
        compiler_params=_params(("arbitrary",)),
        name="mla_q",
    )(proj, cq_g.reshape(1, qr), w_uq_h, qg, cos_t, sin_t)

    k, v = pl.pallas_call(
        functools.partial(_mla_kv_kernel, heads=heads),
        grid=(m // tm,),
        in_specs=[pl.BlockSpec((tm, kvr), lambda i: (i, lay["c_kv"] // kvr)),
                  pl.BlockSpec((1, kvr), lambda i: (0, 0)),
                  pl.BlockSpec((kvr, heads * (MLA_NOPE + MLA_V)), lambda i: (0, 0)),
                  pl.BlockSpec((tm, LANE), lambda i: (i, lay["k_pe"] // LANE)),
                  pl.BlockSpec((1, MLA_QK_PAD), lambda i: (0, 0)),
                  pl.BlockSpec((tm, LANE), lambda i: (i, 0)),
                  pl.BlockSpec((tm, LANE), lambda i: (i, 0))],
        out_specs=[pl.BlockSpec((heads, tm, MLA_QK_PAD), lambda i: (0, i, 0)),
                   pl.BlockSpec((heads, tm, v_ext), lambda i: (0, i, 0))],
        out_shape=[jax.ShapeDtypeStruct((heads, m, MLA_QK_PAD), BF16),
                   jax.ShapeDtypeStruct((heads, m, v_ext), BF16)],
        compiler_params=_params(("arbitrary",)),
        name="mla_kv",
    )(proj, ckv_g.reshape(1, kvr), w_ukv_b, proj, kg, cos_t, sin_t)

    meta_blk = m // BLOCK - 1
    tq = _pick(seq, 512, 128)
    a = pl.pallas_call(
        functools.partial(_mla_attn_kernel, tq=tq, batch=batch),
        grid=(heads, batch + 1),
        in_specs=[pl.BlockSpec((None, seq, MLA_QK_PAD), lambda h, b: (h, b, 0)),
                  pl.BlockSpec((None, seq, MLA_QK_PAD), lambda h, b: (h, b, 0)),
                  pl.BlockSpec((None, seq, v_ext), lambda h, b: (h, b, 0)),
                  pl.BlockSpec((None, BLOCK, MLA_QK_PAD), lambda h, b: (h, meta_blk, 0)),
                  pl.BlockSpec((None, BLOCK, v_ext), lambda h, b: (h, meta_blk, 0))],
        out_specs=pl.BlockSpec((seq, MLA_V), lambda h, b: (b, h)),
        out_shape=jax.ShapeDtypeStruct((m, heads * MLA_V), BF16),
        compiler_params=_params(("arbitrary", "arbitrary")),
        name="mla_attn",
    )(q, k, v, k, v)
    return a


def _t5_bucket(rel, buckets):
    n = jnp.maximum(rel, 0)
    max_exact = buckets // 2
    nf = jnp.maximum(n, 1).astype(F32)
    large = max_exact + (jnp.log(nf / max_exact) / math.log(REL_MAX_DIST / max_exact)
                         * (buckets - max_exact)).astype(jnp.int32)
    large = jnp.minimum(large, buckets - 1)
    return jnp.where(n < max_exact, n, large)


def _bias_kernel(rb_ref, bucket_ref, o_ref, *, buckets):
    pr = pl.program_id(0)
    bucket = bucket_ref[...]
    qi = lax.broadcasted_iota(jnp.int32, bucket.shape, 0)
    sj = lax.broadcasted_iota(jnp.int32, bucket.shape, 1)
    band = (sj - qi >= 1) & (sj - qi <= WINDOW)
    halves = []
    for t in range(2):
        acc = jnp.zeros(bucket.shape, F32)
        for b in range(buckets):
            acc = jnp.where(bucket == b, rb_ref[b, 2 * pr + t], acc)
        halves.append(jnp.where(band, acc * LOG2E, NEG_INF))
    o_ref[...] = jnp.concatenate(halves, axis=1)


def _bias_table(rel_bias):
    buckets, hq = rel_bias.shape
    qi = jnp.arange(BLOCK)[:, None]
    sj = jnp.arange(2 * BLOCK)[None, :]
    bucket = _t5_bucket(qi + BLOCK - sj, buckets).astype(jnp.int32)
    return pl.pallas_call(
        functools.partial(_bias_kernel, buckets=buckets),
        grid=(hq // 2,),
        in_specs=[pl.BlockSpec(memory_space=pltpu.SMEM),
                  pl.BlockSpec((BLOCK, 2 * BLOCK), lambda h: (0, 0))],
        out_specs=pl.BlockSpec((None, BLOCK, 4 * BLOCK), lambda h: (h, 0, 0)),
        out_shape=jax.ShapeDtypeStruct((hq // 2, BLOCK, 4 * BLOCK), F32),
        compiler_params=_params(("arbitrary",)),
        name="rel_bias_table",
    )(rel_bias, bucket)


def _seg_rms(x, seg_ref, segt_ref, gain):
    ss = _dot((x * x).astype(BF16), seg_ref[...])
    r = lax.rsqrt(ss * (1.0 / SWA_HEAD_DIM) + EPS)
    return x * _dot(r.astype(BF16), segt_ref[...]) * gain


def _swa_kernel(prev_ref, plo_ref, clo_ref, q_ref, kc_ref, kp_ref, vc_ref, vp_ref, bias_ref, sink_ref,
                qg_ref, kg_ref, segq_ref, segqt_ref, segk_ref, segkt_ref, ones_ref, o_ref, *, hq, hkv):
    del prev_ref
    t = pl.program_id(0)
    plo = plo_ref[t]
    clo = clo_ref[t]
    half = LANE // 2
    pairs_per_group = hq // hkv // 2

    sj = lax.broadcasted_iota(jnp.int32, (BLOCK, 4 * BLOCK), 1) % (2 * BLOCK)
    in_prev = sj < BLOCK
    krow = jnp.where(in_prev, sj, sj - BLOCK)
    key_bias = jnp.where(krow >= jnp.where(in_prev, plo, clo), 0.0, NEG_INF)

    qn = _seg_rms(q_ref[...].astype(F32), segq_ref, segqt_ref, qg_ref[...]).astype(BF16)
    kf = jnp.concatenate([kp_ref[...], kc_ref[...]], axis=0).astype(F32)
    kn = _seg_rms(kf, segk_ref, segkt_ref, kg_ref[...])
    vf = jnp.concatenate([vp_ref[...], vc_ref[...]], axis=0).astype(F32)

    lane2 = lax.broadcasted_iota(jnp.int32, (2 * BLOCK, LANE), 1)
    lane1 = lax.broadcasted_iota(jnp.int32, (BLOCK, LANE), 1)

    def block_diag(col, odd):
        other = pltpu.roll(col, half, 1)
        on_lo, on_hi = (other, col) if odd else (col, other)
        return jnp.concatenate([jnp.where(lane2 < half, on_lo, 0.0), jnp.where(lane2 < half, 0.0, on_hi)],
                               axis=0).astype(BF16)

    outs = []
    for g in range(hkv):
        c = g // 2
        kbd = block_diag(kn[:, c * LANE:(c + 1) * LANE], g % 2)
        vext = jnp.concatenate([block_diag(vf[:, c * LANE:(c + 1) * LANE], g % 2), ones_ref[...]], axis=1)
        for jj in range(pairs_per_group):
            pr = g * pairs_per_group + jj
            s = _dot_nt(qn[:, pr * LANE:(pr + 1) * LANE], kbd) + bias_ref[pr] + key_bias
            sa, sb = s[:, :2 * BLOCK], s[:, 2 * BLOCK:]
            ma = jnp.maximum(jnp.max(sa, axis=-1, keepdims=True), sink_ref[2 * pr])
            mb = jnp.maximum(jnp.max(sb, axis=-1, keepdims=True), sink_ref[2 * pr + 1])
            e = jnp.concatenate([jnp.exp2(sa - ma), jnp.exp2(sb - mb)], axis=1).astype(BF16)
            oe = _dot(e, vext)
            sink_term = jnp.where(lane1 < half, jnp.exp2(sink_ref[2 * pr] - ma),
                                  jnp.exp2(sink_ref[2 * pr + 1] - mb))
            outs.append(oe[:, :LANE] / (oe[:, LANE:] + sink_term))
    o_ref[...] = jnp.concatenate(outs, axis=-1).astype(o_ref.dtype)


def _swa_branch(proj, lay, bias_tab, sinks, qg, kg, batch, seq):
    m = proj.shape[0]
    hq = sinks.shape[0]
    hd = SWA_HEAD_DIM
    qw = hq * hd
    kw = lay["v_s"] - lay["k_s"]
    hkv = kw // hd
    nblk = m // BLOCK
    per = seq // BLOCK
    t = jnp.arange(nblk, dtype=jnp.int32)
    is_meta = t == nblk - 1
    first = (t % per) == 0
    prev = jnp.where(is_meta | first, nblk - 1, t - 1).astype(jnp.int32)
    plo = jnp.where(is_meta, BLOCK, jnp.where(first, META_LO, 0)).astype(jnp.int32)
    clo = jnp.where(is_meta, META_LO, 0).astype(jnp.int32)
    assert (hq // hkv) % 2 == 0 and kw % LANE == 0 and hq <= LANE

    def seg(width):
        return (jnp.arange(width)[:, None] // hd == jnp.arange(LANE)[None, :]).astype(BF16)

    segq, segk = seg(qw), seg(kw)
    qg_t = (jnp.tile(qg, hq) * (hd ** -0.5 * LOG2E)).reshape(1, qw)
    kg_t = jnp.tile(kg, hkv).reshape(1, kw)
    ones_bd = ((jnp.arange(4 * BLOCK)[:, None] < 2 * BLOCK)
               == (jnp.arange(LANE)[None, :] < LANE // 2)).astype(BF16)

    def const(shape):
        return pl.BlockSpec(shape, lambda i, pr, pl_, cl: (0,) * len(shape))

    grid_spec = pltpu.PrefetchScalarGridSpec(
        num_scalar_prefetch=3,
        grid=(nblk,),
        in_specs=[pl.BlockSpec((BLOCK, qw), lambda i, pr, pl_, cl: (i, lay["q_s"] // qw)),
                  pl.BlockSpec((BLOCK, kw), lambda i, pr, pl_, cl: (i, lay["k_s"] // kw)),
                  pl.BlockSpec((BLOCK, kw), lambda i, pr, pl_, cl: (pr[i], lay["k_s"] // kw)),
                  pl.BlockSpec((BLOCK, kw), lambda i, pr, pl_, cl: (i, lay["v_s"] // kw)),
                  pl.BlockSpec((BLOCK, kw), lambda i, pr, pl_, cl: (pr[i], lay["v_s"] // kw)),
                  const((hq // 2, BLOCK, 4 * BLOCK)),
                  pl.BlockSpec(memory_space=pltpu.SMEM),
                  const((1, qw)), const((1, kw)),
                  const((qw, LANE)), const((LANE, qw)), const((kw, LANE)), const((LANE, kw)),
                  const((4 * BLOCK, LANE))],
        out_specs=pl.BlockSpec((BLOCK, qw), lambda i, pr, pl_, cl: (i, 0)),
    )
    return pl.pallas_call(
        functools.partial(_swa_kernel, hq=hq, hkv=hkv),
        grid_spec=grid_spec,
        out_shape=jax.ShapeDtypeStruct((m, qw), BF16),
        compiler_params=_params(("arbitrary",)),
        name="swa_attn",
    )(prev, plo, clo, proj, proj, proj, proj, proj, bias_tab, sinks * LOG2E, qg_t, kg_t,
      segq, segq.T, segk, segk.T, ones_bd)


def _merge_kernel(a_ref, b_ref, wa_hbm, wb_hbm, ga_ref, gb_ref, o_ref, waf_ref, wab_ref, sema, wbf_ref, wbb_ref,
                  semb, *, layer, tn):
    _resident_weights([_col_window(wa_hbm, (layer,), 0, wa_hbm.shape[1], tn)], waf_ref, wab_ref, sema)
    _resident_weights([_col_window(wb_hbm, (layer,), 0, wb_hbm.shape[1], tn)], wbf_ref, wbb_ref, semb)
    ya = _dot(a_ref[...], wab_ref[0])
    yb = _dot(b_ref[...], wbb_ref[0])
    ga = _sigmoid(ga_ref[...].astype(F32))
    gb = _sigmoid(gb_ref[...].astype(F32))
    o_ref[...] = (ga * ya + gb * yb).astype(o_ref.dtype)


def _merge(a, b, w_a, w_b, layer, proj, lay, tn):
    m, ka = a.shape
    kb = b.shape[1]
    d = w_a.shape[-1]
    tm = _pick(m, 1664, 128)
    ga0 = lay["g_a"] // tn
    gb0 = lay["g_b"] // tn
    hbm = pl.BlockSpec(memory_space=pl.ANY)
    return pl.pallas_call(
        functools.partial(_merge_kernel, layer=layer, tn=tn),
        grid=(d // tn, m // tm),
        in_specs=[pl.BlockSpec((tm, ka), lambda j, i: (i, 0)),
                  pl.BlockSpec((tm, kb), lambda j, i: (i, 0)),
                  hbm, hbm,
                  pl.BlockSpec((tm, tn), lambda j, i: (i, ga0 + j)),
                  pl.BlockSpec((tm, tn), lambda j, i: (i, gb0 + j))],
        out_specs=pl.BlockSpec((tm, tn), lambda j, i: (i, j)),
        out_shape=jax.ShapeDtypeStruct((m, d), BF16),
        scratch_shapes=_staged(1, ka, tn) + _staged(1, kb, tn),
        compiler_params=_params(("arbitrary", "arbitrary"), VMEM_LIMIT_BIG),
        name="branch_merge",
    )(a, b, w_a, w_b, proj, proj)


def _ffn_up_kernel(x_ref, w1_ref, w3_ref, o_ref, w1b_ref, w3b_ref):
    @pl.when(pl.program_id(1) == 0)
    def _():
        _cast_weight(w1_ref, w1b_ref)
        _cast_weight(w3_ref, w3b_ref)

    x = x_ref[...]
    u = _dot(x, w1b_ref[...])
    g = _dot(x, w3b_ref[...])
    o_ref[...] = (u * _sigmoid(u) * g).astype(o_ref.dtype)


def _ffn_up(hn, w1, w3, idx):
    m, d = hn.shape
    ff = w1.shape[-1]
    tn = _pick(ff, 256, 128)
    tm = _pick(m, 1664, 128)
    return pl.pallas_call(
        _ffn_up_kernel,
        grid=(ff // tn, m // tm),
        in_specs=[pl.BlockSpec((tm, d), lambda j, i: (i, 0)),
                  pl.BlockSpec((None, d, tn), lambda j, i: (idx, 0, j)),
                  pl.BlockSpec((None, d, tn), lambda j, i: (idx, 0, j))],
        out_specs=pl.BlockSpec((tm, tn), lambda j, i: (i, j)),
        out_shape=jax.ShapeDtypeStruct((m, ff), BF16),
        scratch_shapes=[pltpu.VMEM((d, tn), BF16), pltpu.VMEM((d, tn), BF16)],
        compiler_params=_params(("arbitrary", "arbitrary"), VMEM_LIMIT_BIG),
        name="ffn_up",
    )(hn, w1, w3)


def _dense_ffn(h_res, gain, w1, w3, w2, idx):
    hn = _rms_norm(h_res, gain)
    t = _ffn_up(hn, w1, w3, idx)
    ff = w1.shape[-1]
    nk = 1
    for cand in (1, 2, 4):
        if ff % cand == 0 and (ff // cand) % LANE == 0 and ff // cand <= 6144:
            nk = cand
            break
    kc = ff // nk
    out = h_res
    for kblk in range(nk):
        out = _mm_res(t, w2, (idx,), kblk, kc, out, 640, "ffn_down")
    return out


def _router_kernel(x_ref, g_ref, w_ref, o_ref, *, n_exp):
    x = x_ref[...]
    ms = jnp.mean(x * x, axis=-1, keepdims=True)
    xn = (x * lax.rsqrt(ms + EPS) * g_ref[...]).astype(BF16)
    logits = _dot(xn, w_ref[...].astype(BF16))
    lane = lax.broadcasted_iota(jnp.int32, logits.shape, 1)
    logits = jnp.where(lane < n_exp, logits, -jnp.inf)
    lane_f = lane.astype(F32)
    m1 = jnp.max(logits, axis=-1, keepdims=True)
    i1 = jnp.min(jnp.where(logits == m1, lane_f, float(LANE)), axis=-1, keepdims=True)
    rest = jnp.where(lane_f == i1, -jnp.inf, logits)
    m2 = jnp.max(rest, axis=-1, keepdims=True)
    i2 = jnp.min(jnp.where(rest == m2, lane_f, float(LANE)), axis=-1, keepdims=True)
    e2 = jnp.exp(m2 - m1)
    w1 = 1.0 / (1.0 + e2)
    w2 = e2 / (1.0 + e2)
    out = jnp.where(lane == 0, i1, jnp.where(lane == 1, i2,
                                             jnp.where(lane == 2, w1, jnp.where(lane == 3, w2, 0.0))))
    o_ref[...] = out


def _router(h_res, gain, router_w):
    m, d = h_res.shape
    n_exp = router_w.shape[-1]
    tm = _pick(m, 640, 128)
    wpad = jnp.pad(router_w, ((0, 0), (0, LANE - n_exp)))
    return pl.pallas_call(
        functools.partial(_router_kernel, n_exp=n_exp),
        grid=(m // tm,),
        in_specs=[pl.BlockSpec((tm, d), lambda i: (i, 0)),
                  pl.BlockSpec((1, d), lambda i: (0, 0)),
                  pl.BlockSpec((d, LANE), lambda i: (0, 0))],
        out_specs=pl.BlockSpec((tm, LANE), lambda i: (i, 0)),
        out_shape=jax.ShapeDtypeStruct((m, LANE), F32),
        compiler_params=_params(("arbitrary",)),
        name="moe_router",
    )(h_res, gain.reshape(1, d), wpad)


def _row_copy(src_hbm, buf, sem, slot, src_row, dst_row):
    return pltpu.make_async_copy(src_hbm.at[pl.ds(src_row, 1)], buf.at[slot, pl.ds(dst_row, 1)],
                                 sem.at[slot])


def _start_row_gather(src_hbm, buf, sem, slot, idx_ref, base, count):
    def body(r2, c):
        for p in range(2):
            r = 2 * r2 + p
            _row_copy(src_hbm, buf, sem, slot, idx_ref[base + r], r).start(priority=p)
        return c
    lax.fori_loop(0, count // 2, body, 0)


def _wait_row_gather(src_hbm, buf, sem, slot, count):
    def body(r, c):
        _row_copy(src_hbm, buf, sem, slot, 0, r).wait()
        return c
    lax.fori_loop(0, count, body, 0)


def _gather_norm_kernel(tok_ref, live_ref, h_hbm, g_ref, o_ref, buf, sem, *, rows):
    s = pl.program_id(0)
    n = pl.num_programs(0)

    @pl.when((s == 0) & (live_ref[0] == 1))
    def _():
        _start_row_gather(h_hbm, buf, sem, 0, tok_ref, 0, rows)

    nxt = jnp.minimum(s + 1, n - 1)

    @pl.when((s + 1 < n) & (live_ref[nxt] == 1))
    def _():
        _start_row_gather(h_hbm, buf, sem, (s + 1) % 2, tok_ref, (s + 1) * rows, rows)

    slot = s % 2

    @pl.when(live_ref[s] == 1)
    def _():
        _wait_row_gather(h_hbm, buf, sem, slot, rows)
        x = buf[slot]
        ms = jnp.mean(x * x, axis=-1, keepdims=True)
        o_ref[...] = (x * lax.rsqrt(ms + EPS) * g_ref[...]).astype(o_ref.dtype)

    @pl.when(live_ref[s] == 0)
    def _():
        o_ref[...] = jnp.zeros(o_ref.shape, o_ref.dtype)


def _gather_norm(h_res, gain, row_tok, live, rows):
    d = h_res.shape[1]
    rp = row_tok.shape[0]
    grid_spec = pltpu.PrefetchScalarGridSpec(
        num_scalar_prefetch=2,
        grid=(rp // rows,),
        in_specs=[pl.BlockSpec(memory_space=pl.ANY),
                  pl.BlockSpec((1, d), lambda i, tok, lv: (0, 0))],
        out_specs=pl.BlockSpec((rows, d), lambda i, tok, lv: (i, 0)),
        scratch_shapes=[pltpu.VMEM((2, rows, d), F32), pltpu.SemaphoreType.DMA((2,))],
    )
    return pl.pallas_call(
        functools.partial(_gather_norm_kernel, rows=rows),
        grid_spec=grid_spec,
        out_shape=jax.ShapeDtypeStruct((rp, d), BF16),
        compiler_params=_params(("arbitrary",)),
        name="moe_gather_norm",
    )(row_tok, live, h_res, gain.reshape(1, d))


def _for_live_subtiles(valid_ref, o_ref, sub, compute):
    i = pl.program_id(1)
    for s in range(o_ref.shape[0] // sub):
        rows = slice(s * sub, (s + 1) * sub)

        @pl.when(valid_ref[i] > s)
        def _(rows=rows):
            o_ref[rows, :] = compute(rows)

        @pl.when(valid_ref[i] <= s)
        def _(rows=rows):
            o_ref[rows, :] = jnp.zeros((sub, o_ref.shape[1]), o_ref.dtype)


def _expert_weights(sched, w_hbms, wf_ref, wb_ref, sem, *, layer, tn):
    be_ref, first_ref, nf_ref = sched
    j = pl.program_id(0)
    i = pl.program_id(1)
    nj = pl.num_programs(0)
    n_w = len(w_hbms)

    def copy(which, e, jj):
        src = w_hbms[which].at[layer, e, :, pl.ds(pl.multiple_of(jj * tn, tn), tn)]
        return pltpu.make_async_copy(src, wf_ref.at[which], sem.at[which])

    @pl.when(first_ref[i] == 1)
    def _():
        @pl.when((j == 0) & (i == 0))
        def _():
            for which in range(n_w):
                copy(which, be_ref[0], 0).start()

        for which in range(n_w):
            copy(which, 0, 0).wait()
            _cast_weight(wf_ref.at[which], wb_ref.at[which])

        more_here = nf_ref[i] >= 0
        nxt_e = jnp.where(more_here, be_ref[jnp.maximum(nf_ref[i], 0)], be_ref[0])
        nxt_j = jnp.where(more_here, j, j + 1)

        @pl.when(more_here | (j + 1 < nj))
        def _():
            for which in range(n_w):
                copy(which, nxt_e, nxt_j).start()


def _moe_up_kernel(be_ref, first_ref, nf_ref, valid_ref, xb_ref, x_ref, w1_hbm, w3_hbm,
                   o_ref, wf_ref, wb_ref, sem, *, sub, layer, tn):
    _expert_weights((be_ref, first_ref, nf_ref), (w1_hbm, w3_hbm), wf_ref, wb_ref, sem,
                    layer=layer, tn=tn)

    def compute(rows):
        x = x_ref[rows, :]
        u = _dot(x, wb_ref[0])
        g = _dot(x, wb_ref[1])
        return (u * _sigmoid(u) * g).astype(o_ref.dtype)

    _for_live_subtiles(valid_ref, o_ref, sub, compute)


def _moe_down_kernel(be_ref, first_ref, nf_ref, valid_ref, xb_ref, x_ref, w_hbm,
                     o_ref, wf_ref, wb_ref, sem, *, sub, layer, tn):
    _expert_weights((be_ref, first_ref, nf_ref), (w_hbm,), wf_ref, wb_ref, sem, layer=layer, tn=tn)
    _for_live_subtiles(valid_ref, o_ref, sub, lambda rows: _dot(x_ref[rows, :], wb_ref[0]))


def _combine_kernel(pos_ref, h_ref, rt_ref, y_hbm, o_ref, buf, sem, *, rows):
    s = pl.program_id(0)
    n = pl.num_programs(0)

    @pl.when(s == 0)
    def _():
        _start_row_gather(y_hbm, buf, sem, 0, pos_ref, 0, 2 * rows)

    @pl.when(s + 1 < n)
    def _():
        _start_row_gather(y_hbm, buf, sem, (s + 1) % 2, pos_ref, (s + 1) * 2 * rows, 2 * rows)

    slot = s % 2
    _wait_row_gather(y_hbm, buf, sem, slot, 2 * rows)

    rt = rt_ref[...]
    y0 = buf[slot, 0:rows, :]
    y1 = buf[slot, rows:2 * rows, :]
    o_ref[...] = h_ref[...] + rt[:, 2:3] * y0 + rt[:, 3:4] * y1


def _moe_ffn(h_res, gain, router_w, w1, w3, w2, idx, n_out):
    m, d = h_res.shape
    n_exp = router_w.shape[-1]
    ff = w1.shape[-1]
    rt = _router(h_res, gain, router_w)

    sub = 256
    tme = 2 * sub
    grows = 128
    r_tot = TOP_K * m
    nb = -(-(r_tot + n_exp * (tme - 1)) // tme)
    rp = nb * tme
    e = rt[:, :TOP_K].astype(jnp.int32).reshape(-1)
    onehot = (e[:, None] == jnp.arange(n_exp, dtype=jnp.int32)[None, :]).astype(jnp.int32)
    csum = jnp.cumsum(onehot, axis=0)
    rank = jnp.sum(onehot * csum, axis=1) - 1
    counts = csum[-1]
    padded = (counts + tme - 1) // tme * tme
    gend = jnp.cumsum(padded)
    goff = gend - padded
    pos = (jnp.sum(onehot * goff[None, :], axis=1) + rank).astype(jnp.int32)
    row_tok = jnp.zeros((rp,), jnp.int32).at[pos].set(jnp.arange(r_tot, dtype=jnp.int32) // TOP_K)
    used = gend[-1] // tme
    blk = jnp.arange(nb, dtype=jnp.int32)
    xb = jnp.minimum(blk, used - 1).astype(jnp.int32)
    be = jnp.minimum(jnp.sum((xb[:, None] * tme >= gend[None, :]).astype(jnp.int32), axis=1),
                     n_exp - 1).astype(jnp.int32)
    first = jnp.concatenate([jnp.ones((1,), jnp.int32), (be[1:] != be[:-1]).astype(jnp.int32)])
    nxt_change = lax.cummin(jnp.where(first == 1, blk, nb)[::-1])[::-1]
    nf = jnp.concatenate([nxt_change[1:], jnp.full((1,), nb, jnp.int32)])
    nf = jnp.where(nf >= nb, -1, nf).astype(jnp.int32)

    def live_tiles(tile, per_block):
        start = jnp.arange(rp // tile, dtype=jnp.int32) * tile
        e_of = jnp.repeat(be, tme // tile)
        real_end = jnp.where(jnp.repeat(blk < used, tme // tile), (goff + counts)[e_of], 0)
        live = (start < real_end).astype(jnp.int32)
        return live if per_block is None else jnp.sum(live.reshape(nb, per_block), axis=1).astype(jnp.int32)

    valid = live_tiles(sub, tme // sub)
    glive = live_tiles(grows, None)

    xs = _gather_norm(h_res, gain, row_tok, glive, grows)

    sched = (be, first, nf, valid, xb)

    def x_map(j, i, *s):
        return (s[4][i], 0)

    def o_map(j, i, *s):
        return (i, j)

    hbm = pl.BlockSpec(memory_space=pl.ANY)
    tn_up = _pick(ff, 512, 128)
    up_spec = pltpu.PrefetchScalarGridSpec(
        num_scalar_prefetch=len(sched),
        grid=(ff // tn_up, nb),
        in_specs=[pl.BlockSpec((tme, d), x_map), hbm, hbm],
        out_specs=pl.BlockSpec((tme, tn_up), o_map),
        scratch_shapes=_staged(2, d, tn_up),
    )
    t = pl.pallas_call(
        functools.partial(_moe_up_kernel, sub=sub, layer=idx, tn=tn_up),
        grid_spec=up_spec,
        out_shape=jax.ShapeDtypeStruct((rp, ff), BF16),
        compiler_params=_params(("arbitrary", "arbitrary"), VMEM_LIMIT_BIG),
        name="moe_up",
    )(*sched, xs, w1, w3)

    tn_dn = _pick(d, 1024, 128)
    dn_spec = pltpu.PrefetchScalarGridSpec(
        num_scalar_prefetch=len(sched),
        grid=(d // tn_dn, nb),
        in_specs=[pl.BlockSpec((tme, ff), x_map), hbm],
        out_specs=pl.BlockSpec((tme, tn_dn), o_map),
        scratch_shapes=_staged(1, ff, tn_dn),
    )
    y = pl.pallas_call(
        functools.partial(_moe_down_kernel, sub=sub, layer=idx, tn=tn_dn),
        grid_spec=dn_spec,
        out_shape=jax.ShapeDtypeStruct((rp, d), F32),
        compiler_params=_params(("arbitrary", "arbitrary"), VMEM_LIMIT_BIG),
        name="moe_down",
    )(*sched, t, w2)

    rows = 256 if n_out % 256 == 0 else 128
    cmb_spec = pltpu.PrefetchScalarGridSpec(
        num_scalar_prefetch=1,
        grid=(n_out // rows,),
        in_specs=[pl.BlockSpec((rows, d), lambda i, p: (i, 0)),
                  pl.BlockSpec((rows, LANE), lambda i, p: (i, 0)),
                  pl.BlockSpec(memory_space=pl.ANY)],
        out_specs=pl.BlockSpec((rows, d), lambda i, p: (i, 0)),
        scratch_shapes=[pltpu.VMEM((2, 2 * rows, d), F32), pltpu.SemaphoreType.DMA((2,))],
    )
    pos_blk = jnp.transpose(pos[:TOP_K * n_out].reshape(n_out // rows, rows, TOP_K), (0, 2, 1)).reshape(-1)
    return pl.pallas_call(
        functools.partial(_combine_kernel, rows=rows),
        grid_spec=cmb_spec,
        out_shape=jax.ShapeDtypeStruct((n_out, d), F32),
        compiler_params=_params(("arbitrary",)),
        name="moe_combine",
    )(pos_blk, h_res, rt, y)


def _in_layout(d, qr, kvr, qw, kw):
    head = qr + kvr
    rest = qw + 2 * kw + 2 * d
    tn = max(t for t in (512, 256, 128) if head % t == 0 and rest % t == 0)
    lay = {"c_q": 0, "c_kv": qr, "k_pe": head}
    cur = head + tn
    for name, width in (("q_s", qw), ("k_s", kw), ("v_s", kw), ("g_a", d), ("g_b", d)):
        lay[name] = cur
        cur += width
    lay["total"] = cur
    assert lay["c_kv"] % kvr == 0 and lay["q_s"] % qw == 0 and lay["k_s"] % kw == 0 and lay["v_s"] % kw == 0
    assert lay["g_a"] % tn == 0 and lay["g_b"] % tn == 0 and d % tn == 0
    return lay, tn


def kernel(x, meta_tokens, rel_bias, attn_norm, w_in, mla_cq_norm, mla_ckv_norm, mla_w_uq, mla_w_ukv,
           mla_q_norm, mla_k_norm, swa_q_norm, swa_k_norm, swa_sinks, w_branch_mla, w_branch_swa, w_out,
           ffn_norm, dense_w1, dense_w3, dense_w2, moe_router, moe_w1, moe_w3, moe_w2):
    batch, seq, d = x.shape
    depth = w_in.shape[0]
    qr = mla_cq_norm.shape[1]
    kvr = mla_ckv_norm.shape[1]
    hq = swa_sinks.shape[1]
    qw = hq * SWA_HEAD_DIM
    kw = (w_in.shape[2] - qr - kvr - MLA_ROPE - qw - 2 * d) // 2
    assert seq % BLOCK == 0 and meta_tokens.shape[0] == N_META
    n_tok = batch * seq
    m = n_tok + BLOCK

    lay, tn_in = _in_layout(d, qr, kvr, qw, kw)

    h_res = jnp.concatenate([x.reshape(n_tok, d), jnp.zeros((META_LO, d), x.dtype),
                             meta_tokens.astype(x.dtype)], axis=0)

    pos = jnp.concatenate([jnp.tile(N_META + jnp.arange(seq), batch), jnp.arange(BLOCK) - META_LO])
    half = MLA_ROPE // 2
    inv_freq = ROPE_THETA ** (-jnp.arange(half, dtype=F32) / half)
    ang = pos.astype(F32)[:, None] * inv_freq[None, :]
    cos, sin = jnp.cos(ang), jnp.sin(ang)
    cos_t = jnp.concatenate([cos, cos, jnp.ones((m, LANE - MLA_ROPE), F32)], axis=1)
    sin_t = jnp.concatenate([-sin, sin, jnp.zeros((m, LANE - MLA_ROPE), F32)], axis=1)

    bias_tab = _bias_table(rel_bias)
    w_in_t = jnp.swapaxes(w_in, 1, 2)

    for i in range(depth):
        hn = _rms_norm(h_res, attn_norm[i])
        proj = _in_proj(hn, w_in_t, i, lay, tn_in)
        a = _mla_branch(proj, lay, mla_cq_norm[i], mla_ckv_norm[i], mla_w_uq[i], mla_w_ukv[i],
                        mla_q_norm[i], mla_k_norm[i], cos_t, sin_t, batch, seq)
        b = _swa_branch(proj, lay, bias_tab, swa_sinks[i], swa_q_norm[i], swa_k_norm[i], batch, seq)
        merged = _merge(a, b, w_branch_mla, w_branch_swa, i, proj, lay, tn_in)
        h_res = _mm_res(merged, w_out, (i,), 0, d, h_res, 1664, "out_proj")
        last = i == depth - 1
        if i % 2 == 0:
            h_res = _dense_ffn(h_res, ffn_norm[i], dense_w1, dense_w3, dense_w2, i // 2)
            if last:
                h_res = h_res[:n_tok]
        else:
            h_res = _moe_ffn(h_res, ffn_norm[i], moe_router[i // 2], moe_w1, moe_w3, moe_w2, i // 2,
                             n_tok if last else m)
    return h_res.reshape(batch, seq, d)
```

```python
import functools
import math

import jax
import jax.numpy as jnp
from jax import lax
from jax.experimental import pallas as pl
from jax.experimental.pallas import tpu as pltpu

F32 = jnp.float32
BF16 = jnp.bfloat16

N_META = 16
BLOCK = 128
WINDOW = 128
MLA_NOPE = 128
MLA_ROPE = 64
MLA_V = 128
MLA_QK = MLA_NOPE + MLA_ROPE
MLA_QK_PAD = 256
ROPE_THETA = 10000.0
SWA_HEAD_DIM = 64
REL_MAX_DIST = 128
TOP_K = 2
EPS = 1e-6
NEG_INF = -1e30
LOG2E = 1.4426950408889634
LANE = 128
META_LO = BLOCK - N_META

VMEM_LIMIT_BIG = 58 * 1024 * 1024
VMEM_LIMIT_MID = 44 * 1024 * 1024


def _params(sem, vmem=VMEM_LIMIT_MID):
    return pltpu.CompilerParams(dimension_semantics=sem, vmem_limit_bytes=vmem)


def _pick(n, target, mult):
    best = None
    d = mult
    while d <= min(n, target):
        if n % d == 0:
            best = d
        d += mult
    return best if best is not None else n


def _round_up(a, b):
    return (a + b - 1) // b * b


def _cast_weight(w_ref, wb_ref):
    k = w_ref.shape[0]
    ch = 512 if k % 512 == 0 else (256 if k % 256 == 0 else 128)
    if k % ch != 0:
        wb_ref[...] = w_ref[...].astype(BF16)
        return

    def body(c, carry):
        r = pl.multiple_of(c * ch, ch)
        wb_ref[pl.ds(r, ch), :] = w_ref[pl.ds(r, ch), :].astype(BF16)
        return carry

    lax.fori_loop(0, k // ch, body, 0)


def _sigmoid(x):
    return 1.0 / (1.0 + jnp.exp(-x))


def _dot(a, b):
    return jnp.dot(a, b, preferred_element_type=F32)


def _dot_nt(a, b):
    return lax.dot_general(a, b, (((1,), (1,)), ((), ())), preferred_element_type=F32)


def _rms_kernel(x_ref, g_ref, o_ref):
    x = x_ref[...]
    ms = jnp.mean(x * x, axis=-1, keepdims=True)
    o_ref[...] = (x * lax.rsqrt(ms + EPS) * g_ref[...]).astype(o_ref.dtype)


def _rms_norm(h, gain):
    m, d = h.shape
    tm = _pick(m, 640, 128)
    return pl.pallas_call(
        _rms_kernel,
        grid=(m // tm,),
        in_specs=[pl.BlockSpec((tm, d), lambda i: (i, 0)),
                  pl.BlockSpec((1, d), lambda i: (0, 0))],
        out_specs=pl.BlockSpec((tm, d), lambda i: (i, 0)),
        out_shape=jax.ShapeDtypeStruct((m, d), BF16),
        compiler_params=_params(("arbitrary",)),
        name="rms_norm",
    )(h, gain.reshape(1, d))


def _embed_norm_kernel(x_ref, meta_ref, g_ref, h_ref, hn_ref, *, x_rows_last):
    i = pl.program_id(0)
    last = pl.num_programs(0) - 1

    tm = h_ref.shape[0]

    def emit(rows, r0, r1):
        h_ref[r0:r1, :] = rows
        ms = jnp.mean(rows * rows, axis=-1, keepdims=True)
        hn_ref[r0:r1, :] = (rows * lax.rsqrt(ms + EPS) * g_ref[...]).astype(hn_ref.dtype)

    def emit_x(rows_end):
        for r0 in range(0, rows_end, BLOCK):
            emit(x_ref[r0:r0 + BLOCK, :], r0, r0 + BLOCK)

    @pl.when(i < last)
    def _():
        emit_x(tm)

    @pl.when(i == last)
    def _():
        emit_x(x_rows_last)
        emit(meta_ref[...], x_rows_last, tm)


def _embed_norm(x2d, meta_blk, gain):
    n_tok, d = x2d.shape
    m = n_tok + BLOCK
    tm = _pick(m, 640, 128)
    x_rows_last = tm - BLOCK
    assert n_tok - (m // tm - 1) * tm == x_rows_last
    x_last = max((n_tok - 1) // tm, 0)
    return pl.pallas_call(
        functools.partial(_embed_norm_kernel, x_rows_last=x_rows_last),
        grid=(m // tm,),
        in_specs=[pl.BlockSpec((tm, d), lambda i: (jnp.minimum(i, x_last), 0)),
                  pl.BlockSpec((BLOCK, d), lambda i: (0, 0)),
                  pl.BlockSpec((1, d), lambda i: (0, 0))],
        out_specs=[pl.BlockSpec((tm, d), lambda i: (i, 0)), pl.BlockSpec((tm, d), lambda i: (i, 0))],
        out_shape=[jax.ShapeDtypeStruct((m, d), F32), jax.ShapeDtypeStruct((m, d), BF16)],
        compiler_params=_params(("arbitrary",), VMEM_LIMIT_BIG),
        name="embed_norm",
    )(x2d, meta_blk, gain.reshape(1, d))


def _in_proj_kernel(x_ref, a_ref, b_ref, o_ref, wb_ref, *, nh):
    j = pl.program_id(0)
    tn, k = a_ref.shape
    half = LANE // 2

    @pl.when(pl.program_id(1) == 0)
    def _():
        @pl.when(j < nh)
        def _():
            wb_ref[...] = a_ref[...].astype(BF16)

        @pl.when(j == nh)
        def _():
            wb_ref[0:half, :] = a_ref[0:half, :].astype(BF16)
            wb_ref[half:tn, :] = jnp.zeros((tn - half, k), BF16)

        @pl.when(j > nh)
        def _():
            wb_ref[0:tn - half, :] = a_ref[half:tn, :].astype(BF16)
            wb_ref[tn - half:tn, :] = b_ref[0:half, :].astype(BF16)

    o_ref[...] = _dot_nt(x_ref[...], wb_ref[...]).astype(o_ref.dtype)


def _in_proj(hn, w_in_t, layer, lay, tn):
    m, k = hn.shape
    nh = lay["k_pe"] // tn
    nblk = lay["total"] // tn
    tm = _pick(m, 1664, 128)
    sub = tn // LANE
    last_b = (w_in_t.shape[1] - 1) // LANE

    def a_map(j, i):
        return (layer, jnp.where(j > nh, j - 1, j), 0)

    def b_map(j, i):
        return (layer, jnp.minimum(jnp.maximum(j, nh + 1) * sub, last_b), 0)

    return pl.pallas_call(
        functools.partial(_in_proj_kernel, nh=nh),
        grid=(nblk, m // tm),
        in_specs=[pl.BlockSpec((tm, k), lambda j, i: (i, 0)),
                  pl.BlockSpec((None, tn, k), a_map),
                  pl.BlockSpec((None, LANE, k), b_map)],
        out_specs=pl.BlockSpec((tm, tn), lambda j, i: (i, j)),
        out_shape=jax.ShapeDtypeStruct((m, lay["total"]), BF16),
        scratch_shapes=[pltpu.VMEM((tn, k), BF16)],
        compiler_params=_params(("arbitrary", "arbitrary"), VMEM_LIMIT_BIG),
        name="in_proj",
    )(hn, w_in_t, w_in_t)


def _resident_weights(windows, wf_ref, wb_ref, sem):
    j = pl.program_id(0)
    nj = pl.num_programs(0)
    n_w = len(windows)

    def copy(which, jj):
        return pltpu.make_async_copy(windows[which](jj), wf_ref.at[which], sem.at[which])

    @pl.when(pl.program_id(1) == 0)
    def _():
        @pl.when(j == 0)
        def _():
            for which in range(n_w):
                copy(which, 0).start()

        for which in range(n_w):
            copy(which, 0).wait()
            _cast_weight(wf_ref.at[which], wb_ref.at[which])

        @pl.when(j + 1 < nj)
        def _():
            for which in range(n_w):
                copy(which, j + 1).start()


def _staged(n_w, k, tn):
    return [pltpu.VMEM((n_w, k, tn), F32), pltpu.VMEM((n_w, k, tn), BF16), pltpu.SemaphoreType.DMA((n_w,))]


def _col_window(w_hbm, lead, k0, kc, tn):
    return lambda jj: w_hbm.at[(*lead, pl.ds(k0, kc), pl.ds(pl.multiple_of(jj * tn, tn), tn))]


def _mm_res_kernel(x_ref, w_hbm, r_ref, o_ref, wf_ref, wb_ref, sem, *, lead, k0, kc, tn):
    _resident_weights([_col_window(w_hbm, lead, k0, kc, tn)], wf_ref, wb_ref, sem)
    o_ref[...] = r_ref[...] + _dot(x_ref[...], wb_ref[0])


def _mm_res(x, w, lead, kblk, kc, res, tm_target, name):
    m = x.shape[0]
    n = w.shape[-1]
    tn = _pick(n, 512, 128)
    tm = _pick(m, tm_target, 128)
    return pl.pallas_call(
        functools.partial(_mm_res_kernel, lead=lead, k0=kblk * kc, kc=kc, tn=tn),
        grid=(n // tn, m // tm),
        in_specs=[pl.BlockSpec((tm, kc), lambda j, i: (i, kblk)),
                  pl.BlockSpec(memory_space=pl.ANY),
                  pl.BlockSpec((tm, tn), lambda j, i: (i, j))],
        out_specs=pl.BlockSpec((tm, tn), lambda j, i: (i, j)),
        out_shape=jax.ShapeDtypeStruct((m, n), F32),
        scratch_shapes=_staged(1, kc, tn),
        compiler_params=_params(("arbitrary", "arbitrary"), VMEM_LIMIT_BIG),
        name=name,
    )(x, w, res)


def _rope(hi, cos, sin):
    lane = lax.broadcasted_iota(jnp.int32, hi.shape, 1)
    half = MLA_ROPE // 2
    rot = jnp.where(lane < half, pltpu.roll(hi, LANE - half, 1), pltpu.roll(hi, half, 1))
    return hi * cos + rot * sin


def _row_rms(x_ref, g_ref):
    x = x_ref[...].astype(F32)
    ms = jnp.mean(x * x, axis=-1, keepdims=True)
    return (x * lax.rsqrt(ms + EPS) * g_ref[...]).astype(BF16)


def _mla_q_kernel(x_ref, g_ref, w_ref, qg_ref, cos_ref, sin_ref, o_ref, *, heads):
    xn = _row_rms(x_ref, g_ref)
    qg = qg_ref[...]
    cos = cos_ref[...]
    sin = sin_ref[...]
    for h in range(heads):
        q = _dot(xn, w_ref[h])
        ss = jnp.sum(q * q, axis=-1, keepdims=True)
        qn = q * lax.rsqrt(ss * (1.0 / MLA_QK) + EPS) * qg
        hi = _rope(qn[:, LANE:], cos, sin)
        o_ref[h] = jnp.concatenate([qn[:, :LANE], hi], axis=-1).astype(o_ref.dtype)


def _mla_kv_kernel(x_ref, g_ref, w_ref, pe_ref, kg_ref, cos_ref, sin_ref, k_ref, v_ref, *, heads):
    xn = _row_rms(x_ref, g_ref)
    kg = kg_ref[...]
    pe = pe_ref[...].astype(F32)
    pe_ss = jnp.sum(pe * pe, axis=-1, keepdims=True)
    pe_rot = _rope(pe * kg[:, LANE:], cos_ref[...], sin_ref[...])
    ones = jnp.ones((xn.shape[0], MLA_V), v_ref.dtype)
    width = MLA_NOPE + MLA_V
    for h in range(heads):
        kv = _dot(xn, w_ref[:, h * width:(h + 1) * width])
        kn = kv[:, :MLA_NOPE]
        r = lax.rsqrt((jnp.sum(kn * kn, axis=-1, keepdims=True) + pe_ss) * (1.0 / MLA_QK) + EPS)
        k_ref[h] = jnp.concatenate([kn * r * kg[:, :LANE], pe_rot * r], axis=-1).astype(k_ref.dtype)
        v_ref[h] = jnp.concatenate([kv[:, MLA_NOPE:].astype(v_ref.dtype), ones], axis=-1)


def _normalise(oe):
    return oe[:, :MLA_V] / oe[:, MLA_V:]


def _mla_attn_kernel(q_ref, k_ref, v_ref, km_ref, vm_ref, o_ref, *, tq, batch):
    seq = q_ref.shape[0]
    nq = seq // tq
    b = pl.program_id(1)

    @pl.when(b < batch)
    def _():
        km = km_ref[...]
        vm = vm_ref[...]
        mcol = lax.broadcasted_iota(jnp.int32, (tq, BLOCK), 1)
        meta_bias = jnp.where(mcol >= META_LO, 0.0, NEG_INF)
        row = lax.broadcasted_iota(jnp.int32, (tq, tq), 0)
        col = lax.broadcasted_iota(jnp.int32, (tq, tq), 1)
        causal_bias = jnp.where(row >= col, 0.0, NEG_INF)
        for qi in range(nq):
            lo, hi = qi * tq, (qi + 1) * tq
            q = q_ref[lo:hi, :]
            s0 = _dot_nt(q, km) + meta_bias
            sd = _dot_nt(q, k_ref[lo:hi, :]) + causal_bias
            m = jnp.maximum(jnp.max(s0, axis=-1, keepdims=True), jnp.max(sd, axis=-1, keepdims=True))
            if qi > 0:
                sf = _dot_nt(q, k_ref[0:lo, :])
                m = jnp.maximum(m, jnp.max(sf, axis=-1, keepdims=True))
            oe = _dot(jnp.exp2(s0 - m).astype(BF16), vm)
            oe = oe + _dot(jnp.exp2(sd - m).astype(BF16), v_ref[lo:hi, :])
            if qi > 0:
                oe = oe + _dot(jnp.exp2(sf - m).astype(BF16), v_ref[0:lo, :])
            o_ref[lo:hi, :] = _normalise(oe).astype(o_ref.dtype)

    @pl.when(b == batch)
    def _():
        row = lax.broadcasted_iota(jnp.int32, (BLOCK, BLOCK), 0)
        col = lax.broadcasted_iota(jnp.int32, (BLOCK, BLOCK), 1)
        s = _dot_nt(q_ref[0:BLOCK, :], km_ref[...])
        s = jnp.where((col >= META_LO) & (row >= col), s, NEG_INF)
        m = jnp.max(s, axis=-1, keepdims=True)
        oe = _dot(jnp.exp2(s - m).astype(BF16), vm_ref[...])
        o_ref[0:BLOCK, :] = _normalise(oe).astype(o_ref.dtype)


def _mla_branch(proj, lay, cq_g, ckv_g, w_uq, w_ukv, qn_g, kn_g, cos_t, sin_t, batch, seq):
    m = proj.shape[0]
    qr = cq_g.shape[0]
    kvr = ckv_g.shape[0]
    heads = w_uq.shape[1] // MLA_QK
    tm = _pick(m, 640, 128)
    scale = MLA_QK ** -0.5 * LOG2E

    w_uq_h = jnp.pad(w_uq.reshape(qr, heads, MLA_QK), ((0, 0), (0, 0), (0, MLA_QK_PAD - MLA_QK)))
    w_uq_h = jnp.transpose(w_uq_h, (1, 0, 2)).astype(BF16)
    w_ukv_b = w_ukv.astype(BF16)
    qg = (jnp.pad(qn_g, (0, MLA_QK_PAD - MLA_QK)) * scale).reshape(1, MLA_QK_PAD)
    kg = jnp.pad(kn_g, (0, MLA_QK_PAD - MLA_QK)).reshape(1, MLA_QK_PAD)
    v_ext = MLA_V + MLA_V

    q = pl.pallas_call(
        functools.partial(_mla_q_kernel, heads=heads),
        grid=(m // tm,),
        in_specs=[pl.BlockSpec((tm, qr), lambda i: (i, lay["c_q"] // qr)),
                  pl.BlockSpec((1, qr), lambda i: (0, 0)),
                  pl.BlockSpec((heads, qr, MLA_QK_PAD), lambda i: (0, 0, 0)),
                  pl.BlockSpec((1, MLA_QK_PAD), lambda i: (0, 0)),
                  pl.BlockSpec((tm, LANE), lambda i: (i, 0)),
                  pl.BlockSpec((tm, LANE), lambda i: (i, 0))],
        out_specs=pl.BlockSpec((heads, tm, MLA_QK_PAD), lambda i: (0, i, 0)),
        out_shape=jax.ShapeDtypeStruct((heads, m, MLA_QK_PAD), BF16),
        compiler_params=_params(("arbitrary",)),
        name="mla_q",
    )(proj, cq_g.reshape(1, qr), w_uq_h, qg, cos_t, sin_t)

    k, v = pl.pallas_call(
        functools.partial(_mla_kv_kernel, heads=heads),
        grid=(m // tm,),
        in_specs=[pl.BlockSpec((tm, kvr), lambda i: (i, lay["c_kv"] // kvr)),
                  pl.BlockSpec((1, kvr), lambda i: (0, 0)),
                  pl.BlockSpec((kvr, heads * (MLA_NOPE + MLA_V)), lambda i: (0, 0)),
                  pl.BlockSpec((tm, LANE), lambda i: (i, lay["k_pe"] // LANE)),
                  pl.BlockSpec((1, MLA_QK_PAD), lambda i: (0, 0)),
                  pl.BlockSpec((tm, LANE), lambda i: (i, 0)),
                  pl.BlockSpec((tm, LANE), lambda i: (i, 0))],
        out_specs=[pl.BlockSpec((heads, tm, MLA_QK_PAD), lambda i: (0, i, 0)),
                   pl.BlockSpec((heads, tm, v_ext), lambda i: (0, i, 0))],
        out_shape=[jax.ShapeDtypeStruct((heads, m, MLA_QK_PAD), BF16),
                   jax.ShapeDtypeStruct((heads, m, v_ext), BF16)],
        compiler_params=_params(("arbitrary",)),
        name="mla_kv",
    )(proj, ckv_g.reshape(1, kvr), w_ukv_b, proj, kg, cos_t, sin_t)

    meta_blk = m // BLOCK - 1
    tq = _pick(seq, 512, 128)
    a = pl.pallas_call(
        functools.partial(_mla_attn_kernel, tq=tq, batch=batch),
        grid=(heads, batch + 1),
        in_specs=[pl.BlockSpec((None, seq, MLA_QK_PAD), lambda h, b: (h, b, 0)),
                  pl.BlockSpec((None, seq, MLA_QK_PAD), lambda h, b: (h, b, 0)),
                  pl.BlockSpec((None, seq, v_ext), lambda h, b: (h, b, 0)),
                  pl.BlockSpec((None, BLOCK, MLA_QK_PAD), lambda h, b: (h, meta_blk, 0)),
                  pl.BlockSpec((None, BLOCK, v_ext), lambda h, b: (h, meta_blk, 0))],
        out_specs=pl.BlockSpec((seq, MLA_V), lambda h, b: (b, h)),
        out_shape=jax.ShapeDtypeStruct((m, heads * MLA_V), BF16),
        compiler_params=_params(("arbitrary", "arbitrary")),
        name="mla_attn",
    )(q, k, v, k, v)
    return a


def _t5_bucket(rel, buckets):
    n = jnp.maximum(rel, 0)
    max_exact = buckets // 2
    nf = jnp.maximum(n, 1).astype(F32)
    large = max_exact + (jnp.log(nf / max_exact) / math.log(REL_MAX_DIST / max_exact)
                         * (buckets - max_exact)).astype(jnp.int32)
    large = jnp.minimum(large, buckets - 1)
    return jnp.where(n < max_exact, n, large)


def _bias_kernel(rb_ref, bucket_ref, o_ref, *, buckets):
    pr = pl.program_id(0)
    bucket = bucket_ref[...]
    qi = lax.broadcasted_iota(jnp.int32, bucket.shape, 0)
    sj = lax.broadcasted_iota(jnp.int32, bucket.shape, 1)
    band = (sj - qi >= 1) & (sj - qi <= WINDOW)
    halves = []
    for t in range(2):
        acc = jnp.zeros(bucket.shape, F32)
        for b in range(buckets):
            acc = jnp.where(bucket == b, rb_ref[b, 2 * pr + t], acc)
        halves.append(jnp.where(band, acc * LOG2E, NEG_INF))
    o_ref[...] = jnp.concatenate(halves, axis=1)


def _bias_table(rel_bias):
    buckets, hq = rel_bias.shape
    qi = jnp.arange(BLOCK)[:, None]
    sj = jnp.arange(2 * BLOCK)[None, :]
    bucket = _t5_bucket(qi + BLOCK - sj, buckets).astype(jnp.int32)
    return pl.pallas_call(
        functools.partial(_bias_kernel, buckets=buckets),
        grid=(hq // 2,),
        in_specs=[pl.BlockSpec(memory_space=pltpu.SMEM),
                  pl.BlockSpec((BLOCK, 2 * BLOCK), lambda h: (0, 0))],
        out_specs=pl.BlockSpec((None, BLOCK, 4 * BLOCK), lambda h: (h, 0, 0)),
        out_shape=jax.ShapeDtypeStruct((hq // 2, BLOCK, 4 * BLOCK), F32),
        compiler_params=_params(("arbitrary",)),
        name="rel_bias_table",
    )(rel_bias, bucket)


def _seg_rms(x, seg_ref, segt_ref, gain):
    ss = _dot((x * x).astype(BF16), seg_ref[...])
    r = lax.rsqrt(ss * (1.0 / SWA_HEAD_DIM) + EPS)
    return x * _dot(r.astype(BF16), segt_ref[...]) * gain


def _swa_kernel(prev_ref, plo_ref, clo_ref, q_ref, kc_ref, kp_ref, vc_ref, vp_ref, bias_ref, sink_ref,
                qg_ref, kg_ref, segq_ref, segqt_ref, segk_ref, segkt_ref, ones_ref, o_ref, *, hq, hkv):
    del prev_ref
    t = pl.program_id(0)
    plo = plo_ref[t]
    clo = clo_ref[t]
    half = LANE // 2
    pairs_per_group = hq // hkv // 2

    sj = lax.broadcasted_iota(jnp.int32, (BLOCK, 4 * BLOCK), 1) % (2 * BLOCK)
    in_prev = sj < BLOCK
    krow = jnp.where(in_prev, sj, sj - BLOCK)
    key_bias = jnp.where(krow >= jnp.where(in_prev, plo, clo), 0.0, NEG_INF)

    qn = _seg_rms(q_ref[...].astype(F32), segq_ref, segqt_ref, qg_ref[...]).astype(BF16)
    kf = jnp.concatenate([kp_ref[...], kc_ref[...]], axis=0).astype(F32)
    kn = _seg_rms(kf, segk_ref, segkt_ref, kg_ref[...])
    vf = jnp.concatenate([vp_ref[...], vc_ref[...]], axis=0).astype(F32)

    lane2 = lax.broadcasted_iota(jnp.int32, (2 * BLOCK, LANE), 1)
    lane1 = lax.broadcasted_iota(jnp.int32, (BLOCK, LANE), 1)

    def block_diag(col, odd):
        other = pltpu.roll(col, half, 1)
        on_lo, on_hi = (other, col) if odd else (col, other)
        return jnp.concatenate([jnp.where(lane2 < half, on_lo, 0.0), jnp.where(lane2 < half, 0.0, on_hi)],
                               axis=0).astype(BF16)

    outs = []
    for g in range(hkv):
        c = g // 2
        kbd = block_diag(kn[:, c * LANE:(c + 1) * LANE], g % 2)
        vext = jnp.concatenate([block_diag(vf[:, c * LANE:(c + 1) * LANE], g % 2), ones_ref[...]], axis=1)
        for jj in range(pairs_per_group):
            pr = g * pairs_per_group + jj
            s = _dot_nt(qn[:, pr * LANE:(pr + 1) * LANE], kbd) + bias_ref[pr] + key_bias
            sa, sb = s[:, :2 * BLOCK], s[:, 2 * BLOCK:]
            ma = jnp.maximum(jnp.max(sa, axis=-1, keepdims=True), sink_ref[2 * pr])
            mb = jnp.maximum(jnp.max(sb, axis=-1, keepdims=True), sink_ref[2 * pr + 1])
            e = jnp.concatenate([jnp.exp2(sa - ma), jnp.exp2(sb - mb)], axis=1).astype(BF16)
            oe = _dot(e, vext)
            sink_term = jnp.where(lane1 < half, jnp.exp2(sink_ref[2 * pr] - ma),
                                  jnp.exp2(sink_ref[2 * pr + 1] - mb))
            outs.append(oe[:, :LANE] / (oe[:, LANE:] + sink_term))
    o_ref[...] = jnp.concatenate(outs, axis=-1).astype(o_ref.dtype)


def _swa_branch(proj, lay, bias_tab, sinks, qg, kg, batch, seq):
    m = proj.shape[0]
    hq = sinks.shape[0]
    hd = SWA_HEAD_DIM
    qw = hq * hd
    kw = lay["v_s"] - lay["k_s"]
    hkv = kw // hd
    nblk = m // BLOCK
    per = seq // BLOCK
    t = jnp.arange(nblk, dtype=jnp.int32)
    is_meta = t == nblk - 1
    first = (t % per) == 0
    prev = jnp.where(is_meta | first, nblk - 1, t - 1).astype(jnp.int32)
    plo = jnp.where(is_meta, BLOCK, jnp.where(first, META_LO, 0)).astype(jnp.int32)
    clo = jnp.where(is_meta, META_LO, 0).astype(jnp.int32)
    assert (hq // hkv) % 2 == 0 and kw % LANE == 0 and hq <= LANE

    def seg(width):
        return (jnp.arange(width)[:, None] // hd == jnp.arange(LANE)[None, :]).astype(BF16)

    segq, segk = seg(qw), seg(kw)
    qg_t = (jnp.tile(qg, hq) * (hd ** -0.5 * LOG2E)).reshape(1, qw)
    kg_t = jnp.tile(kg, hkv).reshape(1, kw)
    ones_bd = ((jnp.arange(4 * BLOCK)[:, None] < 2 * BLOCK)
               == (jnp.arange(LANE)[None, :] < LANE // 2)).astype(BF16)

    def const(shape):
        return pl.BlockSpec(shape, lambda i, pr, pl_, cl: (0,) * len(shape))

    grid_spec = pltpu.PrefetchScalarGridSpec(
        num_scalar_prefetch=3,
        grid=(nblk,),
        in_specs=[pl.BlockSpec((BLOCK, qw), lambda i, pr, pl_, cl: (i, lay["q_s"] // qw)),
                  pl.BlockSpec((BLOCK, kw), lambda i, pr, pl_, cl: (i, lay["k_s"] // kw)),
                  pl.BlockSpec((BLOCK, kw), lambda i, pr, pl_, cl: (pr[i], lay["k_s"] // kw)),
                  pl.BlockSpec((BLOCK, kw), lambda i, pr, pl_, cl: (i, lay["v_s"] // kw)),
                  pl.BlockSpec((BLOCK, kw), lambda i, pr, pl_, cl: (pr[i], lay["v_s"] // kw)),
                  const((hq // 2, BLOCK, 4 * BLOCK)),
                  pl.BlockSpec(memory_space=pltpu.SMEM),
                  const((1, qw)), const((1, kw)),
                  const((qw, LANE)), const((LANE, qw)), const((kw, LANE)), const((LANE, kw)),
                  const((4 * BLOCK, LANE))],
        out_specs=pl.BlockSpec((BLOCK, qw), lambda i, pr, pl_, cl: (i, 0)),
    )
    return pl.pallas_call(
        functools.partial(_swa_kernel, hq=hq, hkv=hkv),
        grid_spec=grid_spec,
        out_shape=jax.ShapeDtypeStruct((m, qw), BF16),
        compiler_params=_params(("arbitrary",)),
        name="swa_attn",
    )(prev, plo, clo, proj, proj, proj, proj, proj, bias_tab, sinks * LOG2E, qg_t, kg_t,
      segq, segq.T, segk, segk.T, ones_bd)


def _merge_kernel(a_ref, b_ref, wa_hbm, wb_hbm, ga_ref, gb_ref, o_ref, waf_ref, wab_ref, sema, wbf_ref, wbb_ref,
                  semb, *, layer, tn):
    _resident_weights([_col_window(wa_hbm, (layer,), 0, wa_hbm.shape[1], tn)], waf_ref, wab_ref, sema)
    _resident_weights([_col_window(wb_hbm, (layer,), 0, wb_hbm.shape[1], tn)], wbf_ref, wbb_ref, semb)
    ya = _dot(a_ref[...], wab_ref[0])
    yb = _dot(b_ref[...], wbb_ref[0])
    ga = _sigmoid(ga_ref[...].astype(F32))
    gb = _sigmoid(gb_ref[...].astype(F32))
    o_ref[...] = (ga * ya + gb * yb).astype(o_ref.dtype)


def _merge(a, b, w_a, w_b, layer, proj, lay, tn):
    m, ka = a.shape
    kb = b.shape[1]
    d = w_a.shape[-1]
    tm = _pick(m, 1664, 128)
    ga0 = lay["g_a"] // tn
    gb0 = lay["g_b"] // tn
    hbm = pl.BlockSpec(memory_space=pl.ANY)
    return pl.pallas_call(
        functools.partial(_merge_kernel, layer=layer, tn=tn),
        grid=(d // tn, m // tm),
        in_specs=[pl.BlockSpec((tm, ka), lambda j, i: (i, 0)),
                  pl.BlockSpec((tm, kb), lambda j, i: (i, 0)),
                  hbm, hbm,
                  pl.BlockSpec((tm, tn), lambda j, i: (i, ga0 + j)),
                  pl.BlockSpec((tm, tn), lambda j, i: (i, gb0 + j))],
        out_specs=pl.BlockSpec((tm, tn), lambda j, i: (i, j)),
        out_shape=jax.ShapeDtypeStruct((m, d), BF16),
        scratch_shapes=_staged(1, ka, tn) + _staged(1, kb, tn),
        compiler_params=_params(("arbitrary", "arbitrary"), VMEM_LIMIT_BIG),
        name="branch_merge",
    )(a, b, w_a, w_b, proj, proj)


def _ffn_up_kernel(x_ref, w1_ref, w3_ref, o_ref, w1b_ref, w3b_ref):
    @pl.when(pl.program_id(1) == 0)
    def _():
        _cast_weight(w1_ref, w1b_ref)
        _cast_weight(w3_ref, w3b_ref)

    x = x_ref[...]
    u = _dot(x, w1b_ref[...])
    g = _dot(x, w3b_ref[...])
    o_ref[...] = (u * _sigmoid(u) * g).astype(o_ref.dtype)


def _ffn_up(hn, w1, w3, idx):
    m, d = hn.shape
    ff = w1.shape[-1]
    tn = _pick(ff, 256, 128)
    tm = _pick(m, 1664, 128)
    return pl.pallas_call(
        _ffn_up_kernel,
        grid=(ff // tn, m // tm),
        in_specs=[pl.BlockSpec((tm, d), lambda j, i: (i, 0)),
                  pl.BlockSpec((None, d, tn), lambda j, i: (idx, 0, j)),
                  pl.BlockSpec((None, d, tn), lambda j, i: (idx, 0, j))],
        out_specs=pl.BlockSpec((tm, tn), lambda j, i: (i, j)),
        out_shape=jax.ShapeDtypeStruct((m, ff), BF16),
        scratch_shapes=[pltpu.VMEM((d, tn), BF16), pltpu.VMEM((d, tn), BF16)],
        compiler_params=_params(("arbitrary", "arbitrary"), VMEM_LIMIT_BIG),
        name="ffn_up",
    )(hn, w1, w3)


def _dense_ffn(h_res, gain, w1, w3, w2, idx):
    hn = _rms_norm(h_res, gain)
    t = _ffn_up(hn, w1, w3, idx)
    ff = w1.shape[-1]
    nk = 1
    for cand in (1, 2, 4):
        if ff % cand == 0 and (ff // cand) % LANE == 0 and ff // cand <= 6144:
            nk = cand
            break
    kc = ff // nk
    out = h_res
    for kblk in range(nk):
        out = _mm_res(t, w2, (idx,), kblk, kc, out, 640, "ffn_down")
    return out


def _router_kernel(x_ref, g_ref, w_ref, o_ref, *, n_exp):
    x = x_ref[...]
    ms = jnp.mean(x * x, axis=-1, keepdims=True)
    xn = (x * lax.rsqrt(ms + EPS) * g_ref[...]).astype(BF16)
    logits = _dot(xn, w_ref[...].astype(BF16))
    lane = lax.broadcasted_iota(jnp.int32, logits.shape, 1)
    logits = jnp.where(lane < n_exp, logits, -jnp.inf)
    lane_f = lane.astype(F32)
    m1 = jnp.max(logits, axis=-1, keepdims=True)
    i1 = jnp.min(jnp.where(logits == m1, lane_f, float(LANE)), axis=-1, keepdims=True)
    rest = jnp.where(lane_f == i1, -jnp.inf, logits)
    m2 = jnp.max(rest, axis=-1, keepdims=True)
    i2 = jnp.min(jnp.where(rest == m2, lane_f, float(LANE)), axis=-1, keepdims=True)
    e2 = jnp.exp(m2 - m1)
    w1 = 1.0 / (1.0 + e2)
    w2 = e2 / (1.0 + e2)
    out = jnp.where(lane == 0, i1, jnp.where(lane == 1, i2,
                                             jnp.where(lane == 2, w1, jnp.where(lane == 3, w2, 0.0))))
    o_ref[...] = out


def _router(h_res, gain, router_w):
    m, d = h_res.shape
    n_exp = router_w.shape[-1]
    tm = _pick(m, 640, 128)
    wpad = jnp.pad(router_w, ((0, 0), (0, LANE - n_exp)))
    return pl.pallas_call(
        functools.partial(_router_kernel, n_exp=n_exp),
        grid=(m // tm,),
        in_specs=[pl.BlockSpec((tm, d), lambda i: (i, 0)),
                  pl.BlockSpec((1, d), lambda i: (0, 0)),
                  pl.BlockSpec((d, LANE), lambda i: (0, 0))],
        out_specs=pl.BlockSpec((tm, LANE), lambda i: (i, 0)),
        out_shape=jax.ShapeDtypeStruct((m, LANE), F32),
        compiler_params=_params(("arbitrary",)),
        name="moe_router",
    )(h_res, gain.reshape(1, d), wpad)


def _row_copy(src_hbm, buf, sem, slot, src_row, dst_row):
    return pltpu.make_async_copy(src_hbm.at[pl.ds(src_row, 1)], buf.at[slot, pl.ds(dst_row, 1)],
                                 sem.at[slot])


def _start_row_gather(src_hbm, buf, sem, slot, idx_ref, base, count):
    def body(r2, c):
        for p in range(2):
            r = 2 * r2 + p
            _row_copy(src_hbm, buf, sem, slot, idx_ref[base + r], r).start(priority=p)
        return c
    lax.fori_loop(0, count // 2, body, 0)


def _wait_row_gather(src_hbm, buf, sem, slot, count):
    def body(r, c):
        _row_copy(src_hbm, buf, sem, slot, 0, r).wait()
        return c
    lax.fori_loop(0, count, body, 0)


def _gather_norm_kernel(tok_ref, live_ref, h_hbm, g_ref, o_ref, buf, sem, *, rows):
    s = pl.program_id(0)
    n = pl.num_programs(0)

    @pl.when((s == 0) & (live_ref[0] == 1))
    def _():
        _start_row_gather(h_hbm, buf, sem, 0, tok_ref, 0, rows)

    nxt = jnp.minimum(s + 1, n - 1)

    @pl.when((s + 1 < n) & (live_ref[nxt] == 1))
    def _():
        _start_row_gather(h_hbm, buf, sem, (s + 1) % 2, tok_ref, (s + 1) * rows, rows)

    slot = s % 2

    @pl.when(live_ref[s] == 1)
    def _():
        _wait_row_gather(h_hbm, buf, sem, slot, rows)
        x = buf[slot]
        ms = jnp.mean(x * x, axis=-1, keepdims=True)
        o_ref[...] = (x * lax.rsqrt(ms + EPS) * g_ref[...]).astype(o_ref.dtype)

    @pl.when(live_ref[s] == 0)
    def _():
        o_ref[...] = jnp.zeros(o_ref.shape, o_ref.dtype)


def _gather_norm(h_res, gain, row_tok, live, rows):
    d = h_res.shape[1]
    rp = row_tok.shape[0]
    grid_spec = pltpu.PrefetchScalarGridSpec(
        num_scalar_prefetch=2,
        grid=(rp // rows,),
        in_specs=[pl.BlockSpec(memory_space=pl.ANY),
                  pl.BlockSpec((1, d), lambda i, tok, lv: (0, 0))],
        out_specs=pl.BlockSpec((rows, d), lambda i, tok, lv: (i, 0)),
        scratch_shapes=[pltpu.VMEM((2, rows, d), F32), pltpu.SemaphoreType.DMA((2,))],
    )
    return pl.pallas_call(
        functools.partial(_gather_norm_kernel, rows=rows),
        grid_spec=grid_spec,
        out_shape=jax.ShapeDtypeStruct((rp, d), BF16),
        compiler_params=_params(("arbitrary",)),
        name="moe_gather_norm",
    )(row_tok, live, h_res, gain.reshape(1, d))


def _for_live_subtiles(valid_ref, o_ref, sub, compute):
    i = pl.program_id(1)
    for s in range(o_ref.shape[0] // sub):
        rows = slice(s * sub, (s + 1) * sub)

        @pl.when(valid_ref[i] > s)
        def _(rows=rows):
            o_ref[rows, :] = compute(rows)

        @pl.when(valid_ref[i] <= s)
        def _(rows=rows):
            o_ref[rows, :] = jnp.zeros((sub, o_ref.shape[1]), o_ref.dtype)


def _expert_weights(sched, w_hbms, wf_ref, wb_ref, sem, *, layer, tn):
    be_ref, first_ref, nf_ref = sched
    j = pl.program_id(0)
    i = pl.program_id(1)
    nj = pl.num_programs(0)
    n_w = len(w_hbms)

    def copy(which, e, jj):
        src = w_hbms[which].at[layer, e, :, pl.ds(pl.multiple_of(jj * tn, tn), tn)]
        return pltpu.make_async_copy(src, wf_ref.at[which], sem.at[which])

    @pl.when(first_ref[i] == 1)
    def _():
        @pl.when((j == 0) & (i == 0))
        def _():
            for which in range(n_w):
                copy(which, be_ref[0], 0).start()

        for which in range(n_w):
            copy(which, 0, 0).wait()
            _cast_weight(wf_ref.at[which], wb_ref.at[which])

        more_here = nf_ref[i] >= 0
        nxt_e = jnp.where(more_here, be_ref[jnp.maximum(nf_ref[i], 0)], be_ref[0])
        nxt_j = jnp.where(more_here, j, j + 1)

        @pl.when(more_here | (j + 1 < nj))
        def _():
            for which in range(n_w):
                copy(which, nxt_e, nxt_j).start()


def _moe_up_kernel(be_ref, first_ref, nf_ref, valid_ref, xb_ref, x_ref, w1_hbm, w3_hbm,
                   o_ref, wf_ref, wb_ref, sem, *, sub, layer, tn):
    _expert_weights((be_ref, first_ref, nf_ref), (w1_hbm, w3_hbm), wf_ref, wb_ref, sem,
                    layer=layer, tn=tn)

    def compute(rows):
        x = x_ref[rows, :]
        u = _dot(x, wb_ref[0])
        g = _dot(x, wb_ref[1])
        return (u * _sigmoid(u) * g).astype(o_ref.dtype)

    _for_live_subtiles(valid_ref, o_ref, sub, compute)


def _moe_down_kernel(be_ref, first_ref, nf_ref, valid_ref, xb_ref, x_ref, w_hbm,
                     o_ref, wf_ref, wb_ref, sem, *, sub, layer, tn):
    _expert_weights((be_ref, first_ref, nf_ref), (w_hbm,), wf_ref, wb_ref, sem, layer=layer, tn=tn)
    _for_live_subtiles(valid_ref, o_ref, sub, lambda rows: _dot(x_ref[rows, :], wb_ref[0]))


def _combine_kernel(pos_ref, h_ref, rt_ref, y_hbm, o_ref, buf, sem, *, rows):
    s = pl.program_id(0)
    n = pl.num_programs(0)

    @pl.when(s == 0)
    def _():
        _start_row_gather(y_hbm, buf, sem, 0, pos_ref, 0, 2 * rows)

    @pl.when(s + 1 < n)
    def _():
        _start_row_gather(y_hbm, buf, sem, (s + 1) % 2, pos_ref, (s + 1) * 2 * rows, 2 * rows)

    slot = s % 2
    _wait_row_gather(y_hbm, buf, sem, slot, 2 * rows)

    rt = rt_ref[...]
    y0 = buf[slot, 0:rows, :]
    y1 = buf[slot, rows:2 * rows, :]
    o_ref[...] = h_ref[...] + rt[:, 2:3] * y0 + rt[:, 3:4] * y1


def _moe_ffn(h_res, gain, router_w, w1, w3, w2, idx, n_out):
    m, d = h_res.shape
    n_exp = router_w.shape[-1]
    ff = w1.shape[-1]
    rt = _router(h_res, gain, router_w)

    sub = 256
    tme = 2 * sub
    grows = 128
    r_tot = TOP_K * m
    nb = -(-(r_tot + n_exp * (tme - 1)) // tme)
    rp = nb * tme
    e = rt[:, :TOP_K].astype(jnp.int32).reshape(-1)
    onehot = (e[:, None] == jnp.arange(n_exp, dtype=jnp.int32)[None, :]).astype(jnp.int32)
    csum = jnp.cumsum(onehot, axis=0)
    rank = jnp.sum(onehot * csum, axis=1) - 1
    counts = csum[-1]
    padded = (counts + tme - 1) // tme * tme
    gend = jnp.cumsum(padded)
    goff = gend - padded
    pos = (jnp.sum(onehot * goff[None, :], axis=1) + rank).astype(jnp.int32)
    row_tok = jnp.zeros((rp,), jnp.int32).at[pos].set(jnp.arange(r_tot, dtype=jnp.int32) // TOP_K)
    used = gend[-1] // tme
    blk = jnp.arange(nb, dtype=jnp.int32)
    xb = jnp.minimum(blk, used - 1).astype(jnp.int32)
    be = jnp.minimum(jnp.sum((xb[:, None] * tme >= gend[None, :]).astype(jnp.int32), axis=1),
                     n_exp - 1).astype(jnp.int32)
    first = jnp.concatenate([jnp.ones((1,), jnp.int32), (be[1:] != be[:-1]).astype(jnp.int32)])
    nxt_change = lax.cummin(jnp.where(first == 1, blk, nb)[::-1])[::-1]
    nf = jnp.concatenate([nxt_change[1:], jnp.full((1,), nb, jnp.int32)])
    nf = jnp.where(nf >= nb, -1, nf).astype(jnp.int32)

    def live_tiles(tile, per_block):
        start = jnp.arange(rp // tile, dtype=jnp.int32) * tile
        e_of = jnp.repeat(be, tme // tile)
        real_end = jnp.where(jnp.repeat(blk < used, tme // tile), (goff + counts)[e_of], 0)
        live = (start < real_end).astype(jnp.int32)
        return live if per_block is None else jnp.sum(live.reshape(nb, per_block), axis=1).astype(jnp.int32)

    valid = live_tiles(sub, tme // sub)
    glive = live_tiles(grows, None)

    xs = _gather_norm(h_res, gain, row_tok, glive, grows)

    sched = (be, first, nf, valid, xb)

    def x_map(j, i, *s):
        return (s[4][i], 0)

    def o_map(j, i, *s):
        return (i, j)

    hbm = pl.BlockSpec(memory_space=pl.ANY)
    tn_up = _pick(ff, 512, 128)
    up_spec = pltpu.PrefetchScalarGridSpec(
        num_scalar_prefetch=len(sched),
        grid=(ff // tn_up, nb),
        in_specs=[pl.BlockSpec((tme, d), x_map), hbm, hbm],
        out_specs=pl.BlockSpec((tme, tn_up), o_map),
        scratch_shapes=_staged(2, d, tn_up),
    )
    t = pl.pallas_call(
        functools.partial(_moe_up_kernel, sub=sub, layer=idx, tn=tn_up),
        grid_spec=up_spec,
        out_shape=jax.ShapeDtypeStruct((rp, ff), BF16),
        compiler_params=_params(("arbitrary", "arbitrary"), VMEM_LIMIT_BIG),
        name="moe_up",
    )(*sched, xs, w1, w3)

    tn_dn = _pick(d, 1024, 128)
    dn_spec = pltpu.PrefetchScalarGridSpec(
        num_scalar_prefetch=len(sched),
        grid=(d // tn_dn, nb),
        in_specs=[pl.BlockSpec((tme, ff), x_map), hbm],
        out_specs=pl.BlockSpec((tme, tn_dn), o_map),
        scratch_shapes=_staged(1, ff, tn_dn),
    )
    y = pl.pallas_call(
        functools.partial(_moe_down_kernel, sub=sub, layer=idx, tn=tn_dn),
        grid_spec=dn_spec,
        out_shape=jax.ShapeDtypeStruct((rp, d), F32),
        compiler_params=_params(("arbitrary", "arbitrary"), VMEM_LIMIT_BIG),
        name="moe_down",
    )(*sched, t, w2)

    rows = 256 if n_out % 256 == 0 else 128
    cmb_spec = pltpu.PrefetchScalarGridSpec(
        num_scalar_prefetch=1,
        grid=(n_out // rows,),
        in_specs=[pl.BlockSpec((rows, d), lambda i, p: (i, 0)),
                  pl.BlockSpec((rows, LANE), lambda i, p: (i, 0)),
                  pl.BlockSpec(memory_space=pl.ANY)],
        out_specs=pl.BlockSpec((rows, d), lambda i, p: (i, 0)),
        scratch_shapes=[pltpu.VMEM((2, 2 * rows, d), F32), pltpu.SemaphoreType.DMA((2,))],
    )
    pos_blk = jnp.transpose(pos[:TOP_K * n_out].reshape(n_out // rows, rows, TOP_K), (0, 2, 1)).reshape(-1)
    return pl.pallas_call(
        functools.partial(_combine_kernel, rows=rows),
        grid_spec=cmb_spec,
        out_shape=jax.ShapeDtypeStruct((n_out, d), F32),
        compiler_params=_params(("arbitrary",)),
        name="moe_combine",
    )(pos_blk, h_res, rt, y)


def _in_layout(d, qr, kvr, qw, kw):
    head = qr + kvr
    rest = qw + 2 * kw + 2 * d
    tn = max(t for t in (512, 256, 128) if head % t == 0 and rest % t == 0)
    lay = {"c_q": 0, "c_kv": qr, "k_pe": head}
    cur = head + tn
    for name, width in (("q_s", qw), ("k_s", kw), ("v_s", kw), ("g_a", d), ("g_b", d)):
        lay[name] = cur
        cur += width
    lay["total"] = cur
    assert lay["c_kv"] % kvr == 0 and lay["q_s"] % qw == 0 and lay["k_s"] % kw == 0 and lay["v_s"] % kw == 0
    assert lay["g_a"] % tn == 0 and lay["g_b"] % tn == 0 and d % tn == 0
    return lay, tn


def kernel(x, meta_tokens, rel_bias, attn_norm, w_in, mla_cq_norm, mla_ckv_norm, mla_w_uq, mla_w_ukv,
           mla_q_norm, mla_k_norm, swa_q_norm, swa_k_norm, swa_sinks, w_branch_mla, w_branch_swa, w_out,
           ffn_norm, dense_w1, dense_w3, dense_w2, moe_router, moe_w1, moe_w3, moe_w2):
    batch, seq, d = x.shape
    depth = w_in.shape[0]
    qr = mla_cq_norm.shape[1]
    kvr = mla_ckv_norm.shape[1]
    hq = swa_sinks.shape[1]
    qw = hq * SWA_HEAD_DIM
    kw = (w_in.shape[2] - qr - kvr - MLA_ROPE - qw - 2 * d) // 2
    assert seq % BLOCK == 0 and meta_tokens.shape[0] == N_META
    n_tok = batch * seq
    m = n_tok + BLOCK

    lay, tn_in = _in_layout(d, qr, kvr, qw, kw)

    meta_blk = jnp.concatenate([jnp.zeros((META_LO, d), x.dtype), meta_tokens.astype(x.dtype)], axis=0)
    h_res, hn0 = _embed_norm(x.reshape(n_tok, d), meta_blk, attn_norm[0])

    pos = jnp.concatenate([jnp.tile(N_META + jnp.arange(seq), batch), jnp.arange(BLOCK) - META_LO])
    half = MLA_ROPE // 2
    inv_freq = ROPE_THETA ** (-jnp.arange(half, dtype=F32) / half)
    ang = pos.astype(F32)[:, None] * inv_freq[None, :]
    cos, sin = jnp.cos(ang), jnp.sin(ang)
    cos_t = jnp.concatenate([cos, cos, jnp.ones((m, LANE - MLA_ROPE), F32)], axis=1)
    sin_t = jnp.concatenate([-sin, sin, jnp.zeros((m, LANE - MLA_ROPE), F32)], axis=1)

    bias_tab = _bias_table(rel_bias)
    w_in_t = jnp.swapaxes(w_in, 1, 2)

    for i in range(depth):
        hn = hn0 if i == 0 else _rms_norm(h_res, attn_norm[i])
        proj = _in_proj(hn, w_in_t, i, lay, tn_in)
        a = _mla_branch(proj, lay, mla_cq_norm[i], mla_ckv_norm[i], mla_w_uq[i], mla_w_ukv[i],
                        mla_q_norm[i], mla_k_norm[i], cos_t, sin_t, batch, seq)
        b = _swa_branch(proj, lay, bias_tab, swa_sinks[i], swa_q_norm[i], swa_k_norm[i], batch, seq)
        merged = _merge(a, b, w_branch_mla, w_branch_swa, i, proj, lay, tn_in)
        h_res = _mm_res(merged, w_out, (i,), 0, d, h_res, 1664, "out_proj")
        last = i == depth - 1
        if i % 2 == 0:
            h_res = _dense_ffn(h_res, ffn_norm[i], dense_w1, dense_w3, dense_w2, i // 2)
            if last:
                h_res = h_res[:n_tok]
        else:
            h_res = _moe_ffn(h_res, ffn_norm[i], moe_router[i // 2], moe_w1, moe_w3, moe_w2, i // 2,
                             n_tok if last else m)
    return h_res.reshape(batch, seq, d)
```

```python
import functools
import math

import jax
import jax.numpy as jnp
from jax import lax
from jax.experimental import pallas as pl
from jax.experimental.pallas import tpu as pltpu

F32 = jnp.float32
BF16 = jnp.bfloat16

N_META = 16
BLOCK = 128
WINDOW = 128
MLA_NOPE = 128
MLA_ROPE = 64
MLA_V = 128
MLA_QK = MLA_NOPE + MLA_ROPE
MLA_QK_PAD = 256
ROPE_THETA = 10000.0
SWA_HEAD_DIM = 64
REL_MAX_DIST = 128
TOP_K = 2
EPS = 1e-6
NEG_INF = -1e30
LOG2E = 1.4426950408889634
LANE = 128
META_LO = BLOCK - N_META

VMEM_LIMIT_BIG = 58 * 1024 * 1024
VMEM_LIMIT_MID = 44 * 1024 * 1024


def _params(sem, vmem=VMEM_LIMIT_MID):
    return pltpu.CompilerParams(dimension_semantics=sem, vmem_limit_bytes=vmem)


def _pick(n, target, mult):
    best = None
    d = mult
    while d <= min(n, target):
        if n % d == 0:
            best = d
        d += mult
    return best if best is not None else n


def _round_up(a, b):
    return (a + b - 1) // b * b


def _cast_weight(w_ref, wb_ref):
    k = w_ref.shape[0]
    ch = 512 if k % 512 == 0 else (256 if k % 256 == 0 else 128)
    if k % ch != 0:
        wb_ref[...] = w_ref[...].astype(BF16)
        return

    def body(c, carry):
        r = pl.multiple_of(c * ch, ch)
        wb_ref[pl.ds(r, ch), :] = w_ref[pl.ds(r, ch), :].astype(BF16)
        return carry

    lax.fori_loop(0, k // ch, body, 0)


def _sigmoid(x):
    return 1.0 / (1.0 + jnp.exp(-x))


def _dot(a, b):
    return jnp.dot(a, b, preferred_element_type=F32)


def _dot_nt(a, b):
    return lax.dot_general(a, b, (((1,), (1,)), ((), ())), preferred_element_type=F32)


def _rms_kernel(x_ref, g_ref, o_ref):
    x = x_ref[...]
    ms = jnp.mean(x * x, axis=-1, keepdims=True)
    o_ref[...] = (x * lax.rsqrt(ms + EPS) * g_ref[...]).astype(o_ref.dtype)


def _rms_norm(h, gain):
    m, d = h.shape
    tm = _pick(m, 640, 128)
    return pl.pallas_call(
        _rms_kernel,
        grid=(m // tm,),
        in_specs=[pl.BlockSpec((tm, d), lambda i: (i, 0)),
                  pl.BlockSpec((1, d), lambda i: (0, 0))],
        out_specs=pl.BlockSpec((tm, d), lambda i: (i, 0)),
        out_shape=jax.ShapeDtypeStruct((m, d), BF16),
        compiler_params=_params(("arbitrary",)),
        name="rms_norm",
    )(h, gain.reshape(1, d))


def _embed_norm_kernel(x_ref, meta_ref, g_ref, h_ref, hn_ref, *, x_rows_last):
    i = pl.program_id(0)
    last = pl.num_programs(0) - 1

    tm = h_ref.shape[0]

    def emit(rows, r0, r1):
        h_ref[r0:r1, :] = rows
        ms = jnp.mean(rows * rows, axis=-1, keepdims=True)
        hn_ref[r0:r1, :] = (rows * lax.rsqrt(ms + EPS) * g_ref[...]).astype(hn_ref.dtype)

    def emit_x(rows_end):
        for r0 in range(0, rows_end, BLOCK):
            emit(x_ref[r0:r0 + BLOCK, :], r0, r0 + BLOCK)

    @pl.when(i < last)
    def _():
        emit_x(tm)

    @pl.when(i == last)
    def _():
        emit_x(x_rows_last)
        emit(meta_ref[...], x_rows_last, tm)


def _embed_norm(x2d, meta_blk, gain):
    n_tok, d = x2d.shape
    m = n_tok + BLOCK
    tm = _pick(m, 640, 128)
    x_rows_last = tm - BLOCK
    assert n_tok - (m // tm - 1) * tm == x_rows_last
    x_last = max((n_tok - 1) // tm, 0)
    return pl.pallas_call(
        functools.partial(_embed_norm_kernel, x_rows_last=x_rows_last),
        grid=(m // tm,),
        in_specs=[pl.BlockSpec((tm, d), lambda i: (jnp.minimum(i, x_last), 0)),
                  pl.BlockSpec((BLOCK, d), lambda i: (0, 0)),
                  pl.BlockSpec((1, d), lambda i: (0, 0))],
        out_specs=[pl.BlockSpec((tm, d), lambda i: (i, 0)), pl.BlockSpec((tm, d), lambda i: (i, 0))],
        out_shape=[jax.ShapeDtypeStruct((m, d), F32), jax.ShapeDtypeStruct((m, d), BF16)],
        compiler_params=_params(("arbitrary",), VMEM_LIMIT_BIG),
        name="embed_norm",
    )(x2d, meta_blk, gain.reshape(1, d))


def _in_proj_kernel(x_ref, a_ref, b_ref, o_ref, wb_ref, *, nh):
    j = pl.program_id(0)
    tn, k = a_ref.shape
    half = LANE // 2

    @pl.when(pl.program_id(1) == 0)
    def _():
        @pl.when(j < nh)
        def _():
            wb_ref[...] = a_ref[...].astype(BF16)

        @pl.when(j == nh)
        def _():
            wb_ref[0:half, :] = a_ref[0:half, :].astype(BF16)
            wb_ref[half:tn, :] = jnp.zeros((tn - half, k), BF16)

        @pl.when(j > nh)
        def _():
            wb_ref[0:tn - half, :] = a_ref[half:tn, :].astype(BF16)
            wb_ref[tn - half:tn, :] = b_ref[0:half, :].astype(BF16)

    o_ref[...] = _dot_nt(x_ref[...], wb_ref[...]).astype(o_ref.dtype)


def _in_proj(hn, w_in_t, layer, lay, tn):
    m, k = hn.shape
    nh = lay["k_pe"] // tn
    nblk = lay["total"] // tn
    tm = _pick(m, 1664, 128)
    sub = tn // LANE
    last_b = (w_in_t.shape[1] - 1) // LANE

    def a_map(j, i):
        return (layer, jnp.where(j > nh, j - 1, j), 0)

    def b_map(j, i):
        return (layer, jnp.minimum(jnp.maximum(j, nh + 1) * sub, last_b), 0)

    return pl.pallas_call(
        functools.partial(_in_proj_kernel, nh=nh),
        grid=(nblk, m // tm),
        in_specs=[pl.BlockSpec((tm, k), lambda j, i: (i, 0)),
                  pl.BlockSpec((None, tn, k), a_map),
                  pl.BlockSpec((None, LANE, k), b_map)],
        out_specs=pl.BlockSpec((tm, tn), lambda j, i: (i, j)),
        out_shape=jax.ShapeDtypeStruct((m, lay["total"]), BF16),
        scratch_shapes=[pltpu.VMEM((tn, k), BF16)],
        compiler_params=_params(("arbitrary", "arbitrary"), VMEM_LIMIT_BIG),
        name="in_proj",
    )(hn, w_in_t, w_in_t)


def _resident_weights(windows, wf_ref, wb_ref, sem):
    j = pl.program_id(0)
    nj = pl.num_programs(0)
    n_w = len(windows)

    def copy(which, jj):
        return pltpu.make_async_copy(windows[which](jj), wf_ref.at[which], sem.at[which])

    @pl.when(pl.program_id(1) == 0)
    def _():
        @pl.when(j == 0)
        def _():
            for which in range(n_w):
                copy(which, 0).start()

        for which in range(n_w):
            copy(which, 0).wait()
            _cast_weight(wf_ref.at[which], wb_ref.at[which])

        @pl.when(j + 1 < nj)
        def _():
            for which in range(n_w):
                copy(which, j + 1).start()


def _staged(n_w, k, tn):
    return [pltpu.VMEM((n_w, k, tn), F32), pltpu.VMEM((n_w, k, tn), BF16), pltpu.SemaphoreType.DMA((n_w,))]


def _col_window(w_hbm, lead, k0, kc, tn):
    return lambda jj: w_hbm.at[(*lead, pl.ds(k0, kc), pl.ds(pl.multiple_of(jj * tn, tn), tn))]


def _mm_res_kernel(x_ref, w_hbm, r_ref, o_ref, wf_ref, wb_ref, sem, *, lead, k0, kc, tn):
    _resident_weights([_col_window(w_hbm, lead, k0, kc, tn)], wf_ref, wb_ref, sem)
    o_ref[...] = r_ref[...] + _dot(x_ref[...], wb_ref[0])


def _mm_res(x, w, lead, kblk, kc, res, tm_target, name):
    m = x.shape[0]
    n = w.shape[-1]
    tn = _pick(n, 512, 128)
    tm = _pick(m, tm_target, 128)
    return pl.pallas_call(
        functools.partial(_mm_res_kernel, lead=lead, k0=kblk * kc, kc=kc, tn=tn),
        grid=(n // tn, m // tm),
        in_specs=[pl.BlockSpec((tm, kc), lambda j, i: (i, kblk)),
                  pl.BlockSpec(memory_space=pl.ANY),
                  pl.BlockSpec((tm, tn), lambda j, i: (i, j))],
        out_specs=pl.BlockSpec((tm, tn), lambda j, i: (i, j)),
        out_shape=jax.ShapeDtypeStruct((m, n), F32),
        scratch_shapes=_staged(1, kc, tn),
        compiler_params=_params(("arbitrary", "arbitrary"), VMEM_LIMIT_BIG),
        name=name,
    )(x, w, res)


def _rope(hi, cos, sin):
    lane = lax.broadcasted_iota(jnp.int32, hi.shape, 1)
    half = MLA_ROPE // 2
    rot = jnp.where(lane < half, pltpu.roll(hi, LANE - half, 1), pltpu.roll(hi, half, 1))
    return hi * cos + rot * sin


def _row_rms(x_ref, g_ref):
    x = x_ref[...].astype(F32)
    ms = jnp.mean(x * x, axis=-1, keepdims=True)
    return (x * lax.rsqrt(ms + EPS) * g_ref[...]).astype(BF16)


def _mla_q_kernel(x_ref, g_ref, w_ref, qg_ref, cos_ref, sin_ref, o_ref, *, heads):
    xn = _row_rms(x_ref, g_ref)
    qg = qg_ref[...]
    cos = cos_ref[...]
    sin = sin_ref[...]
    tm = xn.shape[0]
    step = tm // 2 if tm % 32 == 0 else tm
    for h in range(heads):
        for r0 in range(0, tm, step):
            q = _dot(xn[r0:r0 + step], w_ref[h])
            ss = jnp.sum(q * q, axis=-1, keepdims=True)
            qn = q * lax.rsqrt(ss * (1.0 / MLA_QK) + EPS) * qg
            hi = _rope(qn[:, LANE:], cos[r0:r0 + step], sin[r0:r0 + step])
            o_ref[h, r0:r0 + step, :] = jnp.concatenate([qn[:, :LANE], hi], axis=-1).astype(o_ref.dtype)


def _mla_kv_kernel(x_ref, g_ref, w_ref, pe_ref, kg_ref, cos_ref, sin_ref, k_ref, v_ref, *, heads):
    xn = _row_rms(x_ref, g_ref)
    kg = kg_ref[...]
    pe = pe_ref[...].astype(F32)
    pe_ss = jnp.sum(pe * pe, axis=-1, keepdims=True)
    pe_rot = _rope(pe * kg[:, LANE:], cos_ref[...], sin_ref[...])
    ones = jnp.ones((xn.shape[0], MLA_V), v_ref.dtype)
    width = MLA_NOPE + MLA_V
    for h in range(heads):
        kv = _dot(xn, w_ref[:, h * width:(h + 1) * width])
        kn = kv[:, :MLA_NOPE]
        r = lax.rsqrt((jnp.sum(kn * kn, axis=-1, keepdims=True) + pe_ss) * (1.0 / MLA_QK) + EPS)
        k_ref[h] = jnp.concatenate([kn * r * kg[:, :LANE], pe_rot * r], axis=-1).astype(k_ref.dtype)
        v_ref[h] = jnp.concatenate([kv[:, MLA_NOPE:].astype(v_ref.dtype), ones], axis=-1)


def _normalise(oe):
    return oe[:, :MLA_V] / oe[:, MLA_V:]


def _mla_attn_kernel(q_ref, k_ref, v_ref, km_ref, vm_ref, o_ref, *, tq, batch):
    seq = q_ref.shape[0]
    nq = seq // tq
    b = pl.program_id(1)

    @pl.when(b < batch)
    def _():
        km = km_ref[...]
        vm = vm_ref[...]
        mcol = lax.broadcasted_iota(jnp.int32, (tq, BLOCK), 1)
        meta_bias = jnp.where(mcol >= META_LO, 0.0, NEG_INF)
        row = lax.broadcasted_iota(jnp.int32, (tq, tq), 0)
        col = lax.broadcasted_iota(jnp.int32, (tq, tq), 1)
        causal_bias = jnp.where(row >= col, 0.0, NEG_INF)
        for qi in range(nq):
            lo, hi = qi * tq, (qi + 1) * tq
            q = q_ref[lo:hi, :]
            s0 = _dot_nt(q, km) + meta_bias
            sd = _dot_nt(q, k_ref[lo:hi, :]) + causal_bias
            m = jnp.maximum(jnp.max(s0, axis=-1, keepdims=True), jnp.max(sd, axis=-1, keepdims=True))
            if qi > 0:
                sf = _dot_nt(q, k_ref[0:lo, :])
                m = jnp.maximum(m, jnp.max(sf, axis=-1, keepdims=True))
            oe = _dot(jnp.exp2(s0 - m).astype(BF16), vm)
            oe = oe + _dot(jnp.exp2(sd - m).astype(BF16), v_ref[lo:hi, :])
            if qi > 0:
                oe = oe + _dot(jnp.exp2(sf - m).astype(BF16), v_ref[0:lo, :])
            o_ref[lo:hi, :] = _normalise(oe).astype(o_ref.dtype)

    @pl.when(b == batch)
    def _():
        row = lax.broadcasted_iota(jnp.int32, (BLOCK, BLOCK), 0)
        col = lax.broadcasted_iota(jnp.int32, (BLOCK, BLOCK), 1)
        s = _dot_nt(q_ref[0:BLOCK, :], km_ref[...])
        s = jnp.where((col >= META_LO) & (row >= col), s, NEG_INF)
        m = jnp.max(s, axis=-1, keepdims=True)
        oe = _dot(jnp.exp2(s - m).astype(BF16), vm_ref[...])
        o_ref[0:BLOCK, :] = _normalise(oe).astype(o_ref.dtype)


def _mla_branch(proj, lay, cq_g, ckv_g, w_uq, w_ukv, qn_g, kn_g, cos_t, sin_t, batch, seq):
    m = proj.shape[0]
    qr = cq_g.shape[0]
    kvr = ckv_g.shape[0]
    heads = w_uq.shape[1] // MLA_QK
    tm = _pick(m, 640, 128)
    scale = MLA_QK ** -0.5 * LOG2E

    w_uq_h = jnp.pad(w_uq.reshape(qr, heads, MLA_QK), ((0, 0), (0, 0), (0, MLA_QK_PAD - MLA_QK)))
    w_uq_h = jnp.transpose(w_uq_h, (1, 0, 2)).astype(BF16)
    w_ukv_b = w_ukv.astype(BF16)
    qg = (jnp.pad(qn_g, (0, MLA_QK_PAD - MLA_QK)) * scale).reshape(1, MLA_QK_PAD)
    kg = jnp.pad(kn_g, (0, MLA_QK_PAD - MLA_QK)).reshape(1, MLA_QK_PAD)
    v_ext = MLA_V + MLA_V

    q = pl.pallas_call(
        functools.partial(_mla_q_kernel, heads=heads),
        grid=(m // tm,),
        in_specs=[pl.BlockSpec((tm, qr), lambda i: (i, lay["c_q"] // qr)),
                  pl.BlockSpec((1, qr), lambda i: (0, 0)),
                  pl.BlockSpec((heads, qr, MLA_QK_PAD), lambda i: (0, 0, 0)),
                  pl.BlockSpec((1, MLA_QK_PAD), lambda i: (0, 0)),
                  pl.BlockSpec((tm, LANE), lambda i: (i, 0)),
                  pl.BlockSpec((tm, LANE), lambda i: (i, 0))],
        out_specs=pl.BlockSpec((heads, tm, MLA_QK_PAD), lambda i: (0, i, 0)),
        out_shape=jax.ShapeDtypeStruct((heads, m, MLA_QK_PAD), BF16),
        compiler_params=_params(("arbitrary",)),
        name="mla_q",
    )(proj, cq_g.reshape(1, qr), w_uq_h, qg, cos_t, sin_t)

    k, v = pl.pallas_call(
        functools.partial(_mla_kv_kernel, heads=heads),
        grid=(m // tm,),
        in_specs=[pl.BlockSpec((tm, kvr), lambda i: (i, lay["c_kv"] // kvr)),
                  pl.BlockSpec((1, kvr), lambda i: (0, 0)),
                  pl.BlockSpec((kvr, heads * (MLA_NOPE + MLA_V)), lambda i: (0, 0)),
                  pl.BlockSpec((tm, LANE), lambda i: (i, lay["k_pe"] // LANE)),
                  pl.BlockSpec((1, MLA_QK_PAD), lambda i: (0, 0)),
                  pl.BlockSpec((tm, LANE), lambda i: (i, 0)),
                  pl.BlockSpec((tm, LANE), lambda i: (i, 0))],
        out_specs=[pl.BlockSpec((heads, tm, MLA_QK_PAD), lambda i: (0, i, 0)),
                   pl.BlockSpec((heads, tm, v_ext), lambda i: (0, i, 0))],
        out_shape=[jax.ShapeDtypeStruct((heads, m, MLA_QK_PAD), BF16),
                   jax.ShapeDtypeStruct((heads, m, v_ext), BF16)],
        compiler_params=_params(("arbitrary",)),
        name="mla_kv",
    )(proj, ckv_g.reshape(1, kvr), w_ukv_b, proj, kg, cos_t, sin_t)

    meta_blk = m // BLOCK - 1
    tq = _pick(seq, 512, 128)
    a = pl.pallas_call(
        functools.partial(_mla_attn_kernel, tq=tq, batch=batch),
        grid=(heads, batch + 1),
        in_specs=[pl.BlockSpec((None, seq, MLA_QK_PAD), lambda h, b: (h, b, 0)),
                  pl.BlockSpec((None, seq, MLA_QK_PAD), lambda h, b: (h, b, 0)),
                  pl.BlockSpec((None, seq, v_ext), lambda h, b: (h, b, 0)),
                  pl.BlockSpec((None, BLOCK, MLA_QK_PAD), lambda h, b: (h, meta_blk, 0)),
                  pl.BlockSpec((None, BLOCK, v_ext), lambda h, b: (h, meta_blk, 0))],
        out_specs=pl.BlockSpec((seq, MLA_V), lambda h, b: (b, h)),
        out_shape=jax.ShapeDtypeStruct((m, heads * MLA_V), BF16),
        compiler_params=_params(("arbitrary", "arbitrary")),
        name="mla_attn",
    )(q, k, v, k, v)
    return a


def _t5_bucket(rel, buckets):
    n = jnp.maximum(rel, 0)
    max_exact = buckets // 2
    nf = jnp.maximum(n, 1).astype(F32)
    large = max_exact + (jnp.log(nf / max_exact) / math.log(REL_MAX_DIST / max_exact)
                         * (buckets - max_exact)).astype(jnp.int32)
    large = jnp.minimum(large, buckets - 1)
    return jnp.where(n < max_exact, n, large)


def _bias_kernel(rb_ref, bucket_ref, o_ref, *, buckets):
    pr = pl.program_id(0)
    bucket = bucket_ref[...]
    qi = lax.broadcasted_iota(jnp.int32, bucket.shape, 0)
    sj = lax.broadcasted_iota(jnp.int32, bucket.shape, 1)
    band = (sj - qi >= 1) & (sj - qi <= WINDOW)
    halves = []
    for t in range(2):
        acc = jnp.zeros(bucket.shape, F32)
        for b in range(buckets):
            acc = jnp.where(bucket == b, rb_ref[b, 2 * pr + t], acc)
        halves.append(jnp.where(band, acc * LOG2E, NEG_INF))
    o_ref[...] = jnp.concatenate(halves, axis=1)


def _bias_table(rel_bias):
    buckets, hq = rel_bias.shape
    qi = jnp.arange(BLOCK)[:, None]
    sj = jnp.arange(2 * BLOCK)[None, :]
    bucket = _t5_bucket(qi + BLOCK - sj, buckets).astype(jnp.int32)
    return pl.pallas_call(
        functools.partial(_bias_kernel, buckets=buckets),
        grid=(hq // 2,),
        in_specs=[pl.BlockSpec(memory_space=pltpu.SMEM),
                  pl.BlockSpec((BLOCK, 2 * BLOCK), lambda h: (0, 0))],
        out_specs=pl.BlockSpec((None, BLOCK, 4 * BLOCK), lambda h: (h, 0, 0)),
        out_shape=jax.ShapeDtypeStruct((hq // 2, BLOCK, 4 * BLOCK), F32),
        compiler_params=_params(("arbitrary",)),
        name="rel_bias_table",
    )(rel_bias, bucket)


def _seg_rms(x, seg_ref, segt_ref, gain):
    ss = _dot((x * x).astype(BF16), seg_ref[...])
    r = lax.rsqrt(ss * (1.0 / SWA_HEAD_DIM) + EPS)
    return x * _dot(r.astype(BF16), segt_ref[...]) * gain


def _swa_kernel(prev_ref, plo_ref, clo_ref, q_ref, kc_ref, kp_ref, vc_ref, vp_ref, bias_ref, sink_ref,
                qg_ref, kg_ref, segq_ref, segqt_ref, segk_ref, segkt_ref, ones_ref, o_ref, *, hq, hkv):
    del prev_ref
    t = pl.program_id(0)
    plo = plo_ref[t]
    clo = clo_ref[t]
    half = LANE // 2
    pairs_per_group = hq // hkv // 2

    sj = lax.broadcasted_iota(jnp.int32, (BLOCK, 4 * BLOCK), 1) % (2 * BLOCK)
    in_prev = sj < BLOCK
    krow = jnp.where(in_prev, sj, sj - BLOCK)
    key_bias = jnp.where(krow >= jnp.where(in_prev, plo, clo), 0.0, NEG_INF)

    qn = _seg_rms(q_ref[...].astype(F32), segq_ref, segqt_ref, qg_ref[...]).astype(BF16)
    kf = jnp.concatenate([kp_ref[...], kc_ref[...]], axis=0).astype(F32)
    kn = _seg_rms(kf, segk_ref, segkt_ref, kg_ref[...])
    vf = jnp.concatenate([vp_ref[...], vc_ref[...]], axis=0).astype(F32)

    lane2 = lax.broadcasted_iota(jnp.int32, (2 * BLOCK, LANE), 1)
    lane1 = lax.broadcasted_iota(jnp.int32, (BLOCK, LANE), 1)

    def block_diag(col, odd):
        other = pltpu.roll(col, half, 1)
        on_lo, on_hi = (other, col) if odd else (col, other)
        return jnp.concatenate([jnp.where(lane2 < half, on_lo, 0.0), jnp.where(lane2 < half, 0.0, on_hi)],
                               axis=0).astype(BF16)

    outs = []
    for g in range(hkv):
        c = g // 2
        kbd = block_diag(kn[:, c * LANE:(c + 1) * LANE], g % 2)
        vext = jnp.concatenate([block_diag(vf[:, c * LANE:(c + 1) * LANE], g % 2), ones_ref[...]], axis=1)
        for jj in range(pairs_per_group):
            pr = g * pairs_per_group + jj
            s = _dot_nt(qn[:, pr * LANE:(pr + 1) * LANE], kbd) + bias_ref[pr] + key_bias
            sa, sb = s[:, :2 * BLOCK], s[:, 2 * BLOCK:]
            ma = jnp.maximum(jnp.max(sa, axis=-1, keepdims=True), sink_ref[2 * pr])
            mb = jnp.maximum(jnp.max(sb, axis=-1, keepdims=True), sink_ref[2 * pr + 1])
            e = jnp.concatenate([jnp.exp2(sa - ma), jnp.exp2(sb - mb)], axis=1).astype(BF16)
            oe = _dot(e, vext)
            sink_term = jnp.where(lane1 < half, jnp.exp2(sink_ref[2 * pr] - ma),
                                  jnp.exp2(sink_ref[2 * pr + 1] - mb))
            outs.append(oe[:, :LANE] / (oe[:, LANE:] + sink_term))
    o_ref[...] = jnp.concatenate(outs, axis=-1).astype(o_ref.dtype)


def _swa_branch(proj, lay, bias_tab, sinks, qg, kg, batch, seq):
    m = proj.shape[0]
    hq = sinks.shape[0]
    hd = SWA_HEAD_DIM
    qw = hq * hd
    kw = lay["v_s"] - lay["k_s"]
    hkv = kw // hd
    nblk = m // BLOCK
    per = seq // BLOCK
    t = jnp.arange(nblk, dtype=jnp.int32)
    is_meta = t == nblk - 1
    first = (t % per) == 0
    prev = jnp.where(is_meta | first, nblk - 1, t - 1).astype(jnp.int32)
    plo = jnp.where(is_meta, BLOCK, jnp.where(first, META_LO, 0)).astype(jnp.int32)
    clo = jnp.where(is_meta, META_LO, 0).astype(jnp.int32)
    assert (hq // hkv) % 2 == 0 and kw % LANE == 0 and hq <= LANE

    def seg(width):
        return (jnp.arange(width)[:, None] // hd == jnp.arange(LANE)[None, :]).astype(BF16)

    segq, segk = seg(qw), seg(kw)
    qg_t = (jnp.tile(qg, hq) * (hd ** -0.5 * LOG2E)).reshape(1, qw)
    kg_t = jnp.tile(kg, hkv).reshape(1, kw)
    ones_bd = ((jnp.arange(4 * BLOCK)[:, None] < 2 * BLOCK)
               == (jnp.arange(LANE)[None, :] < LANE // 2)).astype(BF16)

    def const(shape):
        return pl.BlockSpec(shape, lambda i, pr, pl_, cl: (0,) * len(shape))

    grid_spec = pltpu.PrefetchScalarGridSpec(
        num_scalar_prefetch=3,
        grid=(nblk,),
        in_specs=[pl.BlockSpec((BLOCK, qw), lambda i, pr, pl_, cl: (i, lay["q_s"] // qw)),
                  pl.BlockSpec((BLOCK, kw), lambda i, pr, pl_, cl: (i, lay["k_s"] // kw)),
                  pl.BlockSpec((BLOCK, kw), lambda i, pr, pl_, cl: (pr[i], lay["k_s"] // kw)),
                  pl.BlockSpec((BLOCK, kw), lambda i, pr, pl_, cl: (i, lay["v_s"] // kw)),
                  pl.BlockSpec((BLOCK, kw), lambda i, pr, pl_, cl: (pr[i], lay["v_s"] // kw)),
                  const((hq // 2, BLOCK, 4 * BLOCK)),
                  pl.BlockSpec(memory_space=pltpu.SMEM),
                  const((1, qw)), const((1, kw)),
                  const((qw, LANE)), const((LANE, qw)), const((kw, LANE)), const((LANE, kw)),
                  const((4 * BLOCK, LANE))],
        out_specs=pl.BlockSpec((BLOCK, qw), lambda i, pr, pl_, cl: (i, 0)),
    )
    return pl.pallas_call(
        functools.partial(_swa_kernel, hq=hq, hkv=hkv),
        grid_spec=grid_spec,
        out_shape=jax.ShapeDtypeStruct((m, qw), BF16),
        compiler_params=_params(("arbitrary",)),
        name="swa_attn",
    )(prev, plo, clo, proj, proj, proj, proj, proj, bias_tab, sinks * LOG2E, qg_t, kg_t,
      segq, segq.T, segk, segk.T, ones_bd)


def _merge_kernel(a_ref, b_ref, wa_hbm, wb_hbm, ga_ref, gb_ref, o_ref, waf_ref, wab_ref, sema, wbf_ref, wbb_ref,
                  semb, *, layer, tn):
    _resident_weights([_col_window(wa_hbm, (layer,), 0, wa_hbm.shape[1], tn)], waf_ref, wab_ref, sema)
    _resident_weights([_col_window(wb_hbm, (layer,), 0, wb_hbm.shape[1], tn)], wbf_ref, wbb_ref, semb)
    ya = _dot(a_ref[...], wab_ref[0])
    yb = _dot(b_ref[...], wbb_ref[0])
    ga = _sigmoid(ga_ref[...].astype(F32))
    gb = _sigmoid(gb_ref[...].astype(F32))
    o_ref[...] = (ga * ya + gb * yb).astype(o_ref.dtype)


def _merge(a, b, w_a, w_b, layer, proj, lay, tn):
    m, ka = a.shape
    kb = b.shape[1]
    d = w_a.shape[-1]
    tm = _pick(m, 1664, 128)
    ga0 = lay["g_a"] // tn
    gb0 = lay["g_b"] // tn
    hbm = pl.BlockSpec(memory_space=pl.ANY)
    return pl.pallas_call(
        functools.partial(_merge_kernel, layer=layer, tn=tn),
        grid=(d // tn, m // tm),
        in_specs=[pl.BlockSpec((tm, ka), lambda j, i: (i, 0)),
                  pl.BlockSpec((tm, kb), lambda j, i: (i, 0)),
                  hbm, hbm,
                  pl.BlockSpec((tm, tn), lambda j, i: (i, ga0 + j)),
                  pl.BlockSpec((tm, tn), lambda j, i: (i, gb0 + j))],
        out_specs=pl.BlockSpec((tm, tn), lambda j, i: (i, j)),
        out_shape=jax.ShapeDtypeStruct((m, d), BF16),
        scratch_shapes=_staged(1, ka, tn) + _staged(1, kb, tn),
        compiler_params=_params(("arbitrary", "arbitrary"), VMEM_LIMIT_BIG),
        name="branch_merge",
    )(a, b, w_a, w_b, proj, proj)


def _ffn_up_kernel(x_ref, w1_ref, w3_ref, o_ref, w1b_ref, w3b_ref):
    @pl.when(pl.program_id(1) == 0)
    def _():
        _cast_weight(w1_ref, w1b_ref)
        _cast_weight(w3_ref, w3b_ref)

    x = x_ref[...]
    u = _dot(x, w1b_ref[...])
    g = _dot(x, w3b_ref[...])
    o_ref[...] = (u * _sigmoid(u) * g).astype(o_ref.dtype)


def _ffn_up(hn, w1, w3, idx):
    m, d = hn.shape
    ff = w1.shape[-1]
    tn = _pick(ff, 256, 128)
    tm = _pick(m, 1664, 128)
    return pl.pallas_call(
        _ffn_up_kernel,
        grid=(ff // tn, m // tm),
        in_specs=[pl.BlockSpec((tm, d), lambda j, i: (i, 0)),
                  pl.BlockSpec((None, d, tn), lambda j, i: (idx, 0, j)),
                  pl.BlockSpec((None, d, tn), lambda j, i: (idx, 0, j))],
        out_specs=pl.BlockSpec((tm, tn), lambda j, i: (i, j)),
        out_shape=jax.ShapeDtypeStruct((m, ff), BF16),
        scratch_shapes=[pltpu.VMEM((d, tn), BF16), pltpu.VMEM((d, tn), BF16)],
        compiler_params=_params(("arbitrary", "arbitrary"), VMEM_LIMIT_BIG),
        name="ffn_up",
    )(hn, w1, w3)


def _dense_ffn(h_res, gain, w1, w3, w2, idx):
    hn = _rms_norm(h_res, gain)
    t = _ffn_up(hn, w1, w3, idx)
    ff = w1.shape[-1]
    nk = 1
    for cand in (1, 2, 4):
        if ff % cand == 0 and (ff // cand) % LANE == 0 and ff // cand <= 6144:
            nk = cand
            break
    kc = ff // nk
    out = h_res
    for kblk in range(nk):
        out = _mm_res(t, w2, (idx,), kblk, kc, out, 640, "ffn_down")
    return out


def _router_kernel(x_ref, g_ref, w_ref, o_ref, *, n_exp):
    x = x_ref[...]
    ms = jnp.mean(x * x, axis=-1, keepdims=True)
    xn = (x * lax.rsqrt(ms + EPS) * g_ref[...]).astype(BF16)
    logits = _dot(xn, w_ref[...].astype(BF16))
    lane = lax.broadcasted_iota(jnp.int32, logits.shape, 1)
    logits = jnp.where(lane < n_exp, logits, -jnp.inf)
    lane_f = lane.astype(F32)
    m1 = jnp.max(logits, axis=-1, keepdims=True)
    i1 = jnp.min(jnp.where(logits == m1, lane_f, float(LANE)), axis=-1, keepdims=True)
    rest = jnp.where(lane_f == i1, -jnp.inf, logits)
    m2 = jnp.max(rest, axis=-1, keepdims=True)
    i2 = jnp.min(jnp.where(rest == m2, lane_f, float(LANE)), axis=-1, keepdims=True)
    e2 = jnp.exp(m2 - m1)
    w1 = 1.0 / (1.0 + e2)
    w2 = e2 / (1.0 + e2)
    out = jnp.where(lane == 0, i1, jnp.where(lane == 1, i2,
                                             jnp.where(lane == 2, w1, jnp.where(lane == 3, w2, 0.0))))
    o_ref[...] = out


def _router(h_res, gain, router_w):
    m, d = h_res.shape
    n_exp = router_w.shape[-1]
    tm = _pick(m, 640, 128)
    wpad = jnp.pad(router_w, ((0, 0), (0, LANE - n_exp)))
    return pl.pallas_call(
        functools.partial(_router_kernel, n_exp=n_exp),
        grid=(m // tm,),
        in_specs=[pl.BlockSpec((tm, d), lambda i: (i, 0)),
                  pl.BlockSpec((1, d), lambda i: (0, 0)),
                  pl.BlockSpec((d, LANE), lambda i: (0, 0))],
        out_specs=pl.BlockSpec((tm, LANE), lambda i: (i, 0)),
        out_shape=jax.ShapeDtypeStruct((m, LANE), F32),
        compiler_params=_params(("arbitrary",)),
        name="moe_router",
    )(h_res, gain.reshape(1, d), wpad)


def _row_copy(src_hbm, buf, sem, slot, src_row, dst_row):
    return pltpu.make_async_copy(src_hbm.at[pl.ds(src_row, 1)], buf.at[slot, pl.ds(dst_row, 1)],
                                 sem.at[slot])


def _start_row_gather(src_hbm, buf, sem, slot, idx_ref, base, count):
    def body(r2, c):
        for p in range(2):
            r = 2 * r2 + p
            _row_copy(src_hbm, buf, sem, slot, idx_ref[base + r], r).start(priority=p)
        return c
    lax.fori_loop(0, count // 2, body, 0)


def _wait_row_gather(src_hbm, buf, sem, slot, count):
    def body(r, c):
        _row_copy(src_hbm, buf, sem, slot, 0, r).wait()
        return c
    lax.fori_loop(0, count, body, 0)


def _gather_norm_kernel(tok_ref, live_ref, h_hbm, g_ref, o_ref, buf, sem, *, rows):
    s = pl.program_id(0)
    n = pl.num_programs(0)

    @pl.when((s == 0) & (live_ref[0] == 1))
    def _():
        _start_row_gather(h_hbm, buf, sem, 0, tok_ref, 0, rows)

    nxt = jnp.minimum(s + 1, n - 1)

    @pl.when((s + 1 < n) & (live_ref[nxt] == 1))
    def _():
        _start_row_gather(h_hbm, buf, sem, (s + 1) % 2, tok_ref, (s + 1) * rows, rows)

    slot = s % 2

    @pl.when(live_ref[s] == 1)
    def _():
        _wait_row_gather(h_hbm, buf, sem, slot, rows)
        x = buf[slot]
        ms = jnp.mean(x * x, axis=-1, keepdims=True)
        o_ref[...] = (x * lax.rsqrt(ms + EPS) * g_ref[...]).astype(o_ref.dtype)

    @pl.when(live_ref[s] == 0)
    def _():
        o_ref[...] = jnp.zeros(o_ref.shape, o_ref.dtype)


def _gather_norm(h_res, gain, row_tok, live, rows):
    d = h_res.shape[1]
    rp = row_tok.shape[0]
    grid_spec = pltpu.PrefetchScalarGridSpec(
        num_scalar_prefetch=2,
        grid=(rp // rows,),
        in_specs=[pl.BlockSpec(memory_space=pl.ANY),
                  pl.BlockSpec((1, d), lambda i, tok, lv: (0, 0))],
        out_specs=pl.BlockSpec((rows, d), lambda i, tok, lv: (i, 0)),
        scratch_shapes=[pltpu.VMEM((2, rows, d), F32), pltpu.SemaphoreType.DMA((2,))],
    )
    return pl.pallas_call(
        functools.partial(_gather_norm_kernel, rows=rows),
        grid_spec=grid_spec,
        out_shape=jax.ShapeDtypeStruct((rp, d), BF16),
        compiler_params=_params(("arbitrary",)),
        name="moe_gather_norm",
    )(row_tok, live, h_res, gain.reshape(1, d))


def _for_live_subtiles(valid_ref, o_ref, sub, compute):
    i = pl.program_id(1)
    for s in range(o_ref.shape[0] // sub):
        rows = slice(s * sub, (s + 1) * sub)

        @pl.when(valid_ref[i] > s)
        def _(rows=rows):
            o_ref[rows, :] = compute(rows)

        @pl.when(valid_ref[i] <= s)
        def _(rows=rows):
            o_ref[rows, :] = jnp.zeros((sub, o_ref.shape[1]), o_ref.dtype)


def _expert_weights(sched, w_hbms, wf_ref, wb_ref, sem, *, layer, tn):
    be_ref, first_ref, nf_ref = sched
    j = pl.program_id(0)
    i = pl.program_id(1)
    nj = pl.num_programs(0)
    n_w = len(w_hbms)

    def copy(which, e, jj):
        src = w_hbms[which].at[layer, e, :, pl.ds(pl.multiple_of(jj * tn, tn), tn)]
        return pltpu.make_async_copy(src, wf_ref.at[which], sem.at[which])

    @pl.when(first_ref[i] == 1)
    def _():
        @pl.when((j == 0) & (i == 0))
        def _():
            for which in range(n_w):
                copy(which, be_ref[0], 0).start()

        for which in range(n_w):
            copy(which, 0, 0).wait()
            _cast_weight(wf_ref.at[which], wb_ref.at[which])

        more_here = nf_ref[i] >= 0
        nxt_e = jnp.where(more_here, be_ref[jnp.maximum(nf_ref[i], 0)], be_ref[0])
        nxt_j = jnp.where(more_here, j, j + 1)

        @pl.when(more_here | (j + 1 < nj))
        def _():
            for which in range(n_w):
                copy(which, nxt_e, nxt_j).start()


def _moe_up_kernel(be_ref, first_ref, nf_ref, valid_ref, xb_ref, x_ref, w1_hbm, w3_hbm,
                   o_ref, wf_ref, wb_ref, sem, *, sub, layer, tn):
    _expert_weights((be_ref, first_ref, nf_ref), (w1_hbm, w3_hbm), wf_ref, wb_ref, sem,
                    layer=layer, tn=tn)

    def compute(rows):
        x = x_ref[rows, :]
        u = _dot(x, wb_ref[0])
        g = _dot(x, wb_ref[1])
        return (u * _sigmoid(u) * g).astype(o_ref.dtype)

    _for_live_subtiles(valid_ref, o_ref, sub, compute)


def _moe_down_kernel(be_ref, first_ref, nf_ref, valid_ref, xb_ref, x_ref, w_hbm,
                     o_ref, wf_ref, wb_ref, sem, *, sub, layer, tn):
    _expert_weights((be_ref, first_ref, nf_ref), (w_hbm,), wf_ref, wb_ref, sem, layer=layer, tn=tn)
    _for_live_subtiles(valid_ref, o_ref, sub, lambda rows: _dot(x_ref[rows, :], wb_ref[0]))


def _combine_kernel(pos_ref, h_ref, rt_ref, y_hbm, o_ref, buf, sem, *, rows):
    s = pl.program_id(0)
    n = pl.num_programs(0)

    @pl.when(s == 0)
    def _():
        _start_row_gather(y_hbm, buf, sem, 0, pos_ref, 0, 2 * rows)

    @pl.when(s + 1 < n)
    def _():
        _start_row_gather(y_hbm, buf, sem, (s + 1) % 2, pos_ref, (s + 1) * 2 * rows, 2 * rows)

    slot = s % 2
    _wait_row_gather(y_hbm, buf, sem, slot, 2 * rows)

    rt = rt_ref[...]
    y0 = buf[slot, 0:rows, :]
    y1 = buf[slot, rows:2 * rows, :]
    o_ref[...] = h_ref[...] + rt[:, 2:3] * y0 + rt[:, 3:4] * y1


def _moe_ffn(h_res, gain, router_w, w1, w3, w2, idx, n_out):
    m, d = h_res.shape
    n_exp = router_w.shape[-1]
    ff = w1.shape[-1]
    rt = _router(h_res, gain, router_w)

    sub = 256
    tme = 2 * sub
    grows = 128
    r_tot = TOP_K * m
    nb = -(-(r_tot + n_exp * (tme - 1)) // tme)
    rp = nb * tme
    e = rt[:, :TOP_K].astype(jnp.int32).reshape(-1)
    onehot = (e[:, None] == jnp.arange(n_exp, dtype=jnp.int32)[None, :]).astype(jnp.int32)
    csum = jnp.cumsum(onehot, axis=0)
    rank = jnp.sum(onehot * csum, axis=1) - 1
    counts = csum[-1]
    padded = (counts + tme - 1) // tme * tme
    gend = jnp.cumsum(padded)
    goff = gend - padded
    pos = (jnp.sum(onehot * goff[None, :], axis=1) + rank).astype(jnp.int32)
    row_tok = jnp.zeros((rp,), jnp.int32).at[pos].set(jnp.arange(r_tot, dtype=jnp.int32) // TOP_K)
    used = gend[-1] // tme
    blk = jnp.arange(nb, dtype=jnp.int32)
    xb = jnp.minimum(blk, used - 1).astype(jnp.int32)
    be = jnp.minimum(jnp.sum((xb[:, None] * tme >= gend[None, :]).astype(jnp.int32), axis=1),
                     n_exp - 1).astype(jnp.int32)
    first = jnp.concatenate([jnp.ones((1,), jnp.int32), (be[1:] != be[:-1]).astype(jnp.int32)])
    nxt_change = lax.cummin(jnp.where(first == 1, blk, nb)[::-1])[::-1]
    nf = jnp.concatenate([nxt_change[1:], jnp.full((1,), nb, jnp.int32)])
    nf = jnp.where(nf >= nb, -1, nf).astype(jnp.int32)

    def live_tiles(tile, per_block):
        start = jnp.arange(rp // tile, dtype=jnp.int32) * tile
        e_of = jnp.repeat(be, tme // tile)
        real_end = jnp.where(jnp.repeat(blk < used, tme // tile), (goff + counts)[e_of], 0)
        live = (start < real_end).astype(jnp.int32)
        return live if per_block is None else jnp.sum(live.reshape(nb, per_block), axis=1).astype(jnp.int32)

    valid = live_tiles(sub, tme // sub)
    glive = live_tiles(grows, None)

    xs = _gather_norm(h_res, gain, row_tok, glive, grows)

    sched = (be, first, nf, valid, xb)

    def x_map(j, i, *s):
        return (s[4][i], 0)

    def o_map(j, i, *s):
        return (i, j)

    hbm = pl.BlockSpec(memory_space=pl.ANY)
    tn_up = _pick(ff, 512, 128)
    up_spec = pltpu.PrefetchScalarGridSpec(
        num_scalar_prefetch=len(sched),
        grid=(ff // tn_up, nb),
        in_specs=[pl.BlockSpec((tme, d), x_map), hbm, hbm],
        out_specs=pl.BlockSpec((tme, tn_up), o_map),
        scratch_shapes=_staged(2, d, tn_up),
    )
    t = pl.pallas_call(
        functools.partial(_moe_up_kernel, sub=sub, layer=idx, tn=tn_up),
        grid_spec=up_spec,
        out_shape=jax.ShapeDtypeStruct((rp, ff), BF16),
        compiler_params=_params(("arbitrary", "arbitrary"), VMEM_LIMIT_BIG),
        name="moe_up",
    )(*sched, xs, w1, w3)

    tn_dn = _pick(d, 1024, 128)
    dn_spec = pltpu.PrefetchScalarGridSpec(
        num_scalar_prefetch=len(sched),
        grid=(d // tn_dn, nb),
        in_specs=[pl.BlockSpec((tme, ff), x_map), hbm],
        out_specs=pl.BlockSpec((tme, tn_dn), o_map),
        scratch_shapes=_staged(1, ff, tn_dn),
    )
    y = pl.pallas_call(
        functools.partial(_moe_down_kernel, sub=sub, layer=idx, tn=tn_dn),
        grid_spec=dn_spec,
        out_shape=jax.ShapeDtypeStruct((rp, d), F32),
        compiler_params=_params(("arbitrary", "arbitrary"), VMEM_LIMIT_BIG),
        name="moe_down",
    )(*sched, t, w2)

    rows = 256 if n_out % 256 == 0 else 128
    cmb_spec = pltpu.PrefetchScalarGridSpec(
        num_scalar_prefetch=1,
        grid=(n_out // rows,),
        in_specs=[pl.BlockSpec((rows, d), lambda i, p: (i, 0)),
                  pl.BlockSpec((rows, LANE), lambda i, p: (i, 0)),
                  pl.BlockSpec(memory_space=pl.ANY)],
        out_specs=pl.BlockSpec((rows, d), lambda i, p: (i, 0)),
        scratch_shapes=[pltpu.VMEM((2, 2 * rows, d), F32), pltpu.SemaphoreType.DMA((2,))],
    )
    pos_blk = jnp.transpose(pos[:TOP_K * n_out].reshape(n_out // rows, rows, TOP_K), (0, 2, 1)).reshape(-1)
    return pl.pallas_call(
        functools.partial(_combine_kernel, rows=rows),
        grid_spec=cmb_spec,
        out_shape=jax.ShapeDtypeStruct((n_out, d), F32),
        compiler_params=_params(("arbitrary",)),
        name="moe_combine",
    )(pos_blk, h_res, rt, y)


def _in_layout(d, qr, kvr, qw, kw):
    head = qr + kvr
    rest = qw + 2 * kw + 2 * d
    tn = max(t for t in (512, 256, 128) if head % t == 0 and rest % t == 0)
    lay = {"c_q": 0, "c_kv": qr, "k_pe": head}
    cur = head + tn
    for name, width in (("q_s", qw), ("k_s", kw), ("v_s", kw), ("g_a", d), ("g_b", d)):
        lay[name] = cur
        cur += width
    lay["total"] = cur
    assert lay["c_kv"] % kvr == 0 and lay["q_s"] % qw == 0 and lay["k_s"] % kw == 0 and lay["v_s"] % kw == 0
    assert lay["g_a"] % tn == 0 and lay["g_b"] % tn == 0 and d % tn == 0
    return lay, tn


def kernel(x, meta_tokens, rel_bias, attn_norm, w_in, mla_cq_norm, mla_ckv_norm, mla_w_uq, mla_w_ukv,
           mla_q_norm, mla_k_norm, swa_q_norm, swa_k_norm, swa_sinks, w_branch_mla, w_branch_swa, w_out,
           ffn_norm, dense_w1, dense_w3, dense_w2, moe_router, moe_w1, moe_w3, moe_w2):
    batch, seq, d = x.shape
    depth = w_in.shape[0]
    qr = mla_cq_norm.shape[1]
    kvr = mla_ckv_norm.shape[1]
    hq = swa_sinks.shape[1]
    qw = hq * SWA_HEAD_DIM
    kw = (w_in.shape[2] - qr - kvr - MLA_ROPE - qw - 2 * d) // 2
    assert seq % BLOCK == 0 and meta_tokens.shape[0] == N_META
    n_tok = batch * seq
    m = n_tok + BLOCK

    lay, tn_in = _in_layout(d, qr, kvr, qw, kw)

    meta_blk = jnp.concatenate([jnp.zeros((META_LO, d), x.dtype), meta_tokens.astype(x.dtype)], axis=0)
    h_res, hn0 = _embed_norm(x.reshape(n_tok, d), meta_blk, attn_norm[0])

    pos = jnp.concatenate([jnp.tile(N_META + jnp.arange(seq), batch), jnp.arange(BLOCK) - META_LO])
    half = MLA_ROPE // 2
    inv_freq = ROPE_THETA ** (-jnp.arange(half, dtype=F32) / half)
    ang = pos.astype(F32)[:, None] * inv_freq[None, :]
    cos, sin = jnp.cos(ang), jnp.sin(ang)
    cos_t = jnp.concatenate([cos, cos, jnp.ones((m, LANE - MLA_ROPE), F32)], axis=1)
    sin_t = jnp.concatenate([-sin, sin, jnp.zeros((m, LANE - MLA_ROPE), F32)], axis=1)

    bias_tab = _bias_table(rel_bias)
    w_in_t = jnp.swapaxes(w_in, 1, 2)

    for i in range(depth):
        hn = hn0 if i == 0 else _rms_norm(h_res, attn_norm[i])
        proj = _in_proj(hn, w_in_t, i, lay, tn_in)
        a = _mla_branch(proj, lay, mla_cq_norm[i], mla_ckv_norm[i], mla_w_uq[i], mla_w_ukv[i],
                        mla_q_norm[i], mla_k_norm[i], cos_t, sin_t, batch, seq)
        b = _swa_branch(proj, lay, bias_tab, swa_sinks[i], swa_q_norm[i], swa_k_norm[i], batch, seq)
        merged = _merge(a, b, w_branch_mla, w_branch_swa, i, proj, lay, tn_in)
        h_res = _mm_res(merged, w_out, (i,), 0, d, h_res, 1664, "out_proj")
        last = i == depth - 1
        if i % 2 == 0:
            h_res = _dense_ffn(h_res, ffn_norm[i], dense_w1, dense_w3, dense_w2, i // 2)
            if last:
                h_res = h_res[:n_tok]
        else:
            h_res = _moe_ffn(h_res, ffn_norm[i], moe_router[i // 2], moe_w1, moe_w3, moe_w2, i // 2,
                             n_tok if last else m)
    return h_res.reshape(batch, seq, d)
```
